```python
import math, functools
import jax, jax.numpy as jnp
from jax import lax
import numpy as np

D_MODEL = 1024
BATCH = 16
SEQ = 256
DEPTH = 4
DEC_BATCH = 4
DEC_SEQ = 2048
PAST_LEN = 256

GRID_W = 64
MIX_HALF = D_MODEL // 2
H_RET = 4
DV_RET = MIX_HALF // H_RET
DK_RET = DV_RET // 2
RET_CHUNK = 128
H_DIFF = 4
DV_DIFF = MIX_HALF // H_DIFF
DH_DIFF = DV_DIFF // 2
Q_BLOCK = 128
H_SWA = 8
KV_SWA = 2
SWA_GROUP = H_SWA // KV_SWA
DH_SWA = MIX_HALF // H_SWA
WINDOW = 128
SWA_BLOCK = 128
H_GLA = 4
DV_GLA = MIX_HALF // H_GLA
DK_GLA = DV_GLA // 2
GLA_RANK = 16
GLA_TAU = 16.0
GLA_CHUNK = 16
D_FF = 4 * D_MODEL
ROPE_BASE = 10000.0
N_EVEN = (DEPTH + 1) // 2
N_ODD = DEPTH // 2
ALPHA = (2 * DEPTH) ** 0.25
BETA = (8 * DEPTH) ** -0.25
EVEN_SPLITS = (H_RET * DK_RET, H_RET * DK_RET, H_RET * DV_RET, H_RET * DV_RET,
               H_DIFF * 2 * DH_DIFF, H_DIFF * 2 * DH_DIFF, H_DIFF * DV_DIFF)
ODD_SPLITS = (H_SWA * DH_SWA, KV_SWA * DH_SWA, KV_SWA * DH_SWA, H_GLA * DK_GLA, H_GLA * DK_GLA,
              H_GLA * DV_GLA, H_GLA * DV_GLA, 2 * GLA_RANK)
EVEN_IN = sum(EVEN_SPLITS)
ODD_IN = sum(ODD_SPLITS)
EVEN_OUT = H_RET * DV_RET + H_DIFF * DV_DIFF
ODD_OUT = H_SWA * DH_SWA + H_GLA * DV_GLA

kernel_name = 'hybrid_diffusion_trunk_step'


def _split(t, sizes):
    return jnp.split(t, [int(i) for i in np.cumsum(sizes)[:-1]], axis=-1)


def _heads(t, n):
    B, T, _ = t.shape
    return t.reshape(B, T, n, -1).transpose(0, 2, 1, 3)


def _merge_heads(t):
    B, H, T, d = t.shape
    return t.transpose(0, 2, 1, 3).reshape(B, T, H * d)


def _diff_heads(t):
    B, T, _ = t.shape
    return t.reshape(B, T, H_DIFF, 2, DH_DIFF).transpose(0, 2, 3, 1, 4)


def _layer_norm(x, g, b, eps=1e-5):
    xf = x.astype(jnp.float32)
    mu = jnp.mean(xf, axis=-1, keepdims=True)
    var = jnp.mean(jnp.square(xf - mu), axis=-1, keepdims=True)
    y = (xf - mu) * lax.rsqrt(var + eps) * g.astype(jnp.float32) + b.astype(jnp.float32)
    return y.astype(x.dtype)


def _head_norm(t, eps=1e-5):
    tf = t.astype(jnp.float32)
    mu = jnp.mean(tf, axis=-1, keepdims=True)
    var = jnp.mean(jnp.square(tf - mu), axis=-1, keepdims=True)
    return (tf - mu) * lax.rsqrt(var + eps)


def _axial_rope(rows, dim):
    row = jnp.repeat(jnp.arange(rows), GRID_W).astype(jnp.float32)
    col = jnp.tile(jnp.arange(GRID_W), rows).astype(jnp.float32)
    half = dim // 2
    freqs = ROPE_BASE ** (-jnp.arange(0, half, 2, dtype=jnp.float32) / half)
    ar = row[:, None] * freqs
    ac = col[:, None] * freqs
    return (jnp.cos(ar), jnp.sin(ar), jnp.cos(ac), jnp.sin(ac))


def _rot_half(x, cos, sin):
    x1, x2 = jnp.split(x, 2, axis=-1)
    return jnp.concatenate([x1 * cos - x2 * sin, x2 * cos + x1 * sin], axis=-1)


def _apply_axial_rope(x, tabs):
    cr, sr, cc, sc = tabs
    xf = x.astype(jnp.float32)
    xr, xc = jnp.split(xf, 2, axis=-1)
    return jnp.concatenate([_rot_half(xr, cr, sr), _rot_half(xc, cc, sc)], axis=-1).astype(x.dtype)


def _sink_softmax(logits, sink):
    full = jnp.concatenate([logits, jnp.broadcast_to(sink, logits.shape[:-1] + (1,))], axis=-1)
    return jax.nn.softmax(full, axis=-1)[..., :-1]


def _retention_dir(q, k, v, log_gamma, s0):
    B, H, T, dk = q.shape
    dv = v.shape[-1]
    C = RET_CHUNK
    N = T // C
    qc = q.reshape(B, H, N, C, dk)
    kc = k.reshape(B, H, N, C, dk)
    vc = v.reshape(B, H, N, C, dv)
    pos = jnp.arange(C, dtype=jnp.float32)
    dist = pos[:, None] - pos[None, :]
    dmat = jnp.where(dist >= 0, jnp.exp(log_gamma[:, None, None] * jnp.maximum(dist, 0.0)), 0.0)
    scores = jnp.einsum('bhnid,bhnjd->bhnij', qc, kc) * dmat[None, :, None]
    o_intra = jnp.einsum('bhnij,bhnje->bhnie', scores, vc)
    k_w = jnp.exp(log_gamma[:, None] * (C - 1 - pos))
    u = jnp.einsum('bhnjd,hj,bhnje->bhnde', kc, k_w, vc)
    chunk_decay = jnp.exp(log_gamma * C)[None, :, None, None]

    def step(s, uu):
        return chunk_decay * s + uu, s

    s_fin, s_prev = lax.scan(step, s0, jnp.moveaxis(u, 2, 0))
    q_w = jnp.exp(log_gamma[:, None] * (pos + 1.0))
    o_inter = jnp.einsum('bhnid,hi,nbhde->bhnie', qc, q_w, s_prev)
    return (o_intra + o_inter).reshape(B, H, T, dv), s_fin


def _bidir_retention(q, k, v, log_g, s0f, s0b):
    f32 = jnp.float32
    q, k, v = q.astype(f32), k.astype(f32), v.astype(f32)
    o_f, s_f = _retention_dir(q, k, v, log_g[0], s0f.astype(f32))
    fl = lambda t: jnp.flip(t, axis=2)
    o_b, s_b = _retention_dir(fl(q), fl(k), fl(v), log_g[1], s0b.astype(f32))
    return o_f + fl(o_b), s_f, s_b


def _gla_dir(q, k, v, log_a, s0):
    B, H, T, dk = q.shape
    dv = v.shape[-1]
    C = GLA_CHUNK
    N = T // C
    qc = q.reshape(B, H, N, C, dk)
    kc = k.reshape(B, H, N, C, dk)
    vc = v.reshape(B, H, N, C, dv)
    b = jnp.cumsum(log_a.reshape(B, H, N, C, dk), axis=3)
    causal = jnp.tril(jnp.ones((C, C), dtype=bool))
    rel = jnp.where(causal[:, :, None], b[:, :, :, :, None, :] - b[:, :, :, None, :, :], -jnp.inf)
    attn = jnp.einsum('bhnid,bhnjd,bhnijd->bhnij', qc, kc, jnp.exp(rel))
    o_intra = jnp.einsum('bhnij,bhnje->bhnie', attn, vc)
    b_last = b[:, :, :, -1:, :]
    u = jnp.einsum('bhnjd,bhnje->bhnde', kc * jnp.exp(b_last - b), vc)
    decay = jnp.exp(b_last[:, :, :, 0, :])

    def step(s, xs):
        d, uu = xs
        return d[..., None] * s + uu, s

    s_fin, s_prev = lax.scan(step, s0, (jnp.moveaxis(decay, 2, 0), jnp.moveaxis(u, 2, 0)))
    o_inter = jnp.einsum('bhnid,nbhde->bhnie', qc * jnp.exp(b), s_prev)
    return (o_intra + o_inter).reshape(B, H, T, dv), s_fin


def _gla_log_gates(glr, w2, b2):
    B, T, _ = glr.shape
    z = jnp.einsum('btdr,drk->dbtk', glr.reshape(B, T, 2, GLA_RANK), w2) + b2[:, None, None, :]
    log_a = jax.nn.log_sigmoid(z.astype(jnp.float32)) / GLA_TAU
    return log_a.reshape(2, B, T, H_GLA, DK_GLA).transpose(0, 1, 3, 2, 4)


def _bidir_gla(q, k, v, log_a, s0f, s0b):
    f32 = jnp.float32
    q, k, v = q.astype(f32), k.astype(f32), v.astype(f32)
    o_f, s_f = _gla_dir(q, k, v, log_a[0], s0f.astype(f32))
    fl = lambda t: jnp.flip(t, axis=2)
    o_b, s_b = _gla_dir(fl(q), fl(k), fl(v), fl(log_a[1]), s0b.astype(f32))
    return o_f + fl(o_b), s_f, s_b


def _diff_lambda(p, lam_init):
    p = p.astype(jnp.float32)
    return jnp.exp(jnp.sum(p[0] * p[1])) - jnp.exp(jnp.sum(p[2] * p[3])) + lam_init


def _diff_core(q, k, v, lam):
    s = jnp.einsum('bhmqd,bhmkd->bhmqk', q, k).astype(jnp.float32) * DH_DIFF ** -0.5
    p = jax.nn.softmax(s, axis=-1)
    w = p[:, :, 0] - lam * p[:, :, 1]
    return jnp.einsum('bhqk,bhkd->bhqd', w.astype(v.dtype), v)


def _sink_attention_ctx(q, k, v, sink):
    B, KV, G, L, dh = q.shape
    s = jnp.einsum('bkgqd,bkld->bkgql', q, k).astype(jnp.float32) * dh ** -0.5
    p = _sink_softmax(s, sink.astype(jnp.float32).reshape(KV, G)[None, :, :, None, None])
    o = jnp.einsum('bkgql,bkld->bkgqd', p.astype(v.dtype), v)
    return o.reshape(B, KV * G, L, dh)


def _banded_sink_attention(q, k, v, k_ctx, v_ctx, sink):
    B, KV, G, T, dh = q.shape
    W = SWA_BLOCK
    NB = T // W
    L = k_ctx.shape[2]

    def band(t):
        tp = jnp.pad(t, ((0, 0), (0, 0), (W, W), (0, 0))).reshape(B, KV, NB + 2, W, dh)
        return jnp.concatenate([tp[:, :, :-2], tp[:, :, 1:-1], tp[:, :, 2:]], axis=3)

    kb, vb = band(k), band(v)
    qb = q.reshape(B, KV, G, NB, W, dh)
    scale = dh ** -0.5
    s_loc = jnp.einsum('bkgnqd,bknjd->bkgnqj', qb, kb).astype(jnp.float32) * scale
    s_ctx = jnp.einsum('bkgnqd,bkld->bkgnql', qb, k_ctx).astype(jnp.float32) * scale
    blk = jnp.arange(NB)[:, None, None]
    qpos = blk * W + jnp.arange(W)[None, :, None]
    kpos = (blk - 1) * W + jnp.arange(3 * W)[None, None, :]
    allowed = (kpos >= 0) & (kpos < T) & (jnp.abs(qpos - kpos) <= WINDOW)
    s_loc = jnp.where(allowed, s_loc, -jnp.inf)
    p = _sink_softmax(jnp.concatenate([s_ctx, s_loc], axis=-1),
                      sink.astype(jnp.float32).reshape(KV, G)[None, :, :, None, None, None])
    p = p.astype(v.dtype)
    o = (jnp.einsum('bkgnql,bkld->bkgnqd', p[..., :L], v_ctx)
         + jnp.einsum('bkgnqj,bknjd->bkgnqd', p[..., L:], vb))
    return o.reshape(B, KV * G, T, dh)


def _even_ctx(h, w_in, w_out, log_g, lam, lam_init):
    B, L, _ = h.shape
    rq, rk, rv, rg, dq, dk, dv = _split(h @ w_in, EVEN_SPLITS)
    zero = jnp.zeros((B, H_RET, DK_RET, DV_RET), jnp.float32)
    o_r, s_f, s_b = _bidir_retention(_heads(rq, H_RET), _heads(rk, H_RET) * DK_RET ** -0.5,
                                     _heads(rv, H_RET), log_g, zero, zero)
    ret = _merge_heads(_head_norm(o_r)).astype(h.dtype) * jax.nn.silu(rg)
    q2, k2, v2 = _diff_heads(dq), _diff_heads(dk), _heads(dv, H_DIFF)
    o_d = _diff_core(q2, k2, v2, lam)
    diff = _merge_heads(_head_norm(o_d)).astype(h.dtype) * (1.0 - lam_init)
    out = jnp.concatenate([ret, diff], axis=-1) @ w_out
    return out, (jnp.stack([s_f, s_b], axis=1), k2, v2)


def _even_latent(h, rope, w_in, w_out, log_g, lam, lam_init, st_ret, ck, cv):
    B, T, _ = h.shape
    rq, rk, rv, rg, dq, dk, dv = _split(h @ w_in, EVEN_SPLITS)
    o_r, _, _ = _bidir_retention(_heads(rq, H_RET), _heads(rk, H_RET) * DK_RET ** -0.5,
                                 _heads(rv, H_RET), log_g, st_ret[:, 0], st_ret[:, 1])
    ret = _merge_heads(_head_norm(o_r)).astype(h.dtype) * jax.nn.silu(rg)
    q2 = _apply_axial_rope(_diff_heads(dq), rope)
    k_all = jnp.concatenate([ck, _apply_axial_rope(_diff_heads(dk), rope)], axis=3)
    v_all = jnp.concatenate([cv, _heads(dv, H_DIFF)], axis=2)
    qb = jnp.moveaxis(q2.reshape(B, H_DIFF, 2, T // Q_BLOCK, Q_BLOCK, DH_DIFF), 3, 0)
    ob = lax.map(lambda qq: _diff_core(qq, k_all, v_all, lam), qb)
    o_d = jnp.moveaxis(ob, 0, 2).reshape(B, H_DIFF, T, DV_DIFF)
    diff = _merge_heads(_head_norm(o_d)).astype(h.dtype) * (1.0 - lam_init)
    return jnp.concatenate([ret, diff], axis=-1) @ w_out, ()


def _odd_ctx(h, w_in, w_out, sink, w2, b2):
    B, L, _ = h.shape
    sq, sk, sv, gq, gk, gv, gr, glr = _split(h @ w_in, ODD_SPLITS)
    q = _heads(sq, H_SWA).reshape(B, KV_SWA, SWA_GROUP, L, DH_SWA)
    k, v = _heads(sk, KV_SWA), _heads(sv, KV_SWA)
    swa = _merge_heads(_sink_attention_ctx(q, k, v, sink))
    zero = jnp.zeros((B, H_GLA, DK_GLA, DV_GLA), jnp.float32)
    o_g, s_f, s_b = _bidir_gla(_heads(gq, H_GLA) * DK_GLA ** -0.5, _heads(gk, H_GLA), _heads(gv, H_GLA),
                               _gla_log_gates(glr, w2, b2), zero, zero)
    gla = _merge_heads(_head_norm(o_g)).astype(h.dtype) * jax.nn.silu(gr)
    out = jnp.concatenate([swa, gla], axis=-1) @ w_out
    return out, (k, v, jnp.stack([s_f, s_b], axis=1))


def _odd_latent(h, rope, w_in, w_out, sink, w2, b2, ck, cv, st_gla):
    B, T, _ = h.shape
    sq, sk, sv, gq, gk, gv, gr, glr = _split(h @ w_in, ODD_SPLITS)
    q = _apply_axial_rope(_heads(sq, H_SWA), rope).reshape(B, KV_SWA, SWA_GROUP, T, DH_SWA)
    k = _apply_axial_rope(_heads(sk, KV_SWA), rope)
    v = _heads(sv, KV_SWA)
    swa = _merge_heads(_banded_sink_attention(q, k, v, ck, cv, sink))
    o_g, _, _ = _bidir_gla(_heads(gq, H_GLA) * DK_GLA ** -0.5, _heads(gk, H_GLA), _heads(gv, H_GLA),
                           _gla_log_gates(glr, w2, b2), st_gla[:, 0], st_gla[:, 1])
    gla = _merge_heads(_head_norm(o_g)).astype(h.dtype) * jax.nn.silu(gr)
    return jnp.concatenate([swa, gla], axis=-1) @ w_out, ()


def _block(x, mod, mixer, g, b, w1, w2):
    sh1, sc1, gt1, sh2, sc2, gt2 = jnp.split(mod, 6, axis=-1)
    mix, aux = mixer(x * (1 + sc1) + sh1)
    x = _layer_norm(ALPHA * x + gt1 * mix, g[0], b[0])
    hf = x * (1 + sc2) + sh2
    ff = jnp.square(jax.nn.relu(hf @ w1)) @ w2
    x = _layer_norm(ALPHA * x + gt2 * ff, g[1], b[1])
    return x, aux


def setup_inputs(seed: int = 0) -> dict:
    key = jax.random.key(seed)
    ks = jax.random.split(key, 26)
    f32 = jnp.float32
    nrm = lambda k, shape, s: jax.random.normal(k, shape, f32) * s
    D, L = D_MODEL, PAST_LEN
    ret_base = jnp.asarray(np.log(-np.log1p(-(2.0 ** (-5.0 - np.arange(H_RET)))))).astype(f32)
    return {
        'x_prompt': nrm(ks[0], (BATCH, SEQ, D), 1.0),
        'x_sample': nrm(ks[1], (DEC_BATCH, DEC_SEQ, D), 1.0),
        'c': nrm(ks[2], (DEC_BATCH, D), 1.0),
        'state_ret': nrm(ks[3], (DEC_BATCH, N_EVEN, 2, H_RET, DK_RET, DV_RET), 0.5),
        'cache_diff_k': nrm(ks[4], (DEC_BATCH, N_EVEN, H_DIFF, 2, L, DH_DIFF), 1.0),
        'cache_diff_v': nrm(ks[5], (DEC_BATCH, N_EVEN, H_DIFF, L, DV_DIFF), 1.0),
        'cache_swa_k': nrm(ks[6], (DEC_BATCH, N_ODD, KV_SWA, L, DH_SWA), 1.0),
        'cache_swa_v': nrm(ks[7], (DEC_BATCH, N_ODD, KV_SWA, L, DH_SWA), 1.0),
        'state_gla': nrm(ks[8], (DEC_BATCH, N_ODD, 2, H_GLA, DK_GLA, DV_GLA), 0.5),
        'c_ctx': nrm(ks[9], (D,), 1.0),
        'w_mod': nrm(ks[10], (DEPTH, D, 6 * D), 0.5 * D ** -0.5),
        'b_mod': nrm(ks[11], (DEPTH, 6 * D), 0.02),
        'ln_g': 1.0 + nrm(ks[12], (DEPTH, 2, D), 0.02),
        'ln_b': nrm(ks[13], (DEPTH, 2, D), 0.02),
        'w_in_even': nrm(ks[14], (N_EVEN, D, EVEN_IN), D ** -0.5),
        'w_out_even': nrm(ks[15], (N_EVEN, EVEN_OUT, D), BETA * EVEN_OUT ** -0.5),
        'ret_decay': ret_base[None, None, :] + nrm(ks[16], (N_EVEN, 2, H_RET), 0.05),
        'diff_lam': nrm(ks[17], (N_EVEN, 4, DH_DIFF), 0.1),
        'w_in_odd': nrm(ks[18], (N_ODD, D, ODD_IN), D ** -0.5),
        'w_out_odd': nrm(ks[19], (N_ODD, ODD_OUT, D), BETA * ODD_OUT ** -0.5),
        'swa_sink': nrm(ks[20], (N_ODD, H_SWA), 0.5),
        'gla_w2': nrm(ks[21], (N_ODD, 2, GLA_RANK, H_GLA * DK_GLA), GLA_RANK ** -0.5),
        'gla_b': nrm(ks[22], (N_ODD, 2, H_GLA * DK_GLA), 0.1),
        'w_ff1': nrm(ks[23], (DEPTH, D, D_FF), D ** -0.5),
        'w_ff2': nrm(ks[24], (DEPTH, D_FF, D), BETA * D_FF ** -0.5),
    }


def reference(x_prompt, x_sample, c, state_ret, cache_diff_k, cache_diff_v, cache_swa_k, cache_swa_v,
              state_gla, c_ctx, w_mod, b_mod, ln_g, ln_b, w_in_even, w_out_even, ret_decay, diff_lam,
              w_in_odd, w_out_odd, swa_sink, gla_w2, gla_b, w_ff1, w_ff2):
    T = x_sample.shape[1]
    rows = T // GRID_W
    rope_diff = _axial_rope(rows, DH_DIFF)
    rope_swa = _axial_rope(rows, DH_SWA)
    mod_ctx = jnp.einsum('d,lde->le', jax.nn.silu(c_ctx), w_mod) + b_mod
    mod_lat = jnp.einsum('bd,lde->lbe', jax.nn.silu(c), w_mod) + b_mod[:, None, :]
    xp, xs = x_prompt, x_sample
    new_ret, new_dk, new_dv, new_sk, new_sv, new_gla = [], [], [], [], [], []
    for l in range(DEPTH):
        mc = mod_ctx[l][None, None, :]
        ml = mod_lat[l][:, None, :]
        if l % 2 == 0:
            e = l // 2
            log_g = -jnp.exp(ret_decay[e].astype(jnp.float32))
            lam_init = 0.8 - 0.6 * math.exp(-0.3 * l)
            lam = _diff_lambda(diff_lam[e], lam_init)
            ctx_mix = functools.partial(_even_ctx, w_in=w_in_even[e], w_out=w_out_even[e], log_g=log_g,
                                        lam=lam, lam_init=lam_init)
            xp, (s_r, k_c, v_c) = _block(xp, mc, ctx_mix, ln_g[l], ln_b[l], w_ff1[l], w_ff2[l])
            new_ret.append(s_r)
            new_dk.append(k_c)
            new_dv.append(v_c)
            lat_mix = functools.partial(_even_latent, rope=rope_diff, w_in=w_in_even[e], w_out=w_out_even[e],
                                        log_g=log_g, lam=lam, lam_init=lam_init, st_ret=state_ret[:, e],
                                        ck=cache_diff_k[:, e], cv=cache_diff_v[:, e])
            xs, _ = _block(xs, ml, lat_mix, ln_g[l], ln_b[l], w_ff1[l], w_ff2[l])
        else:
            o = l // 2
            ctx_mix = functools.partial(_odd_ctx, w_in=w_in_odd[o], w_out=w_out_odd[o], sink=swa_sink[o],
                                        w2=gla_w2[o], b2=gla_b[o])
            xp, (k_c, v_c, s_g) = _block(xp, mc, ctx_mix, ln_g[l], ln_b[l], w_ff1[l], w_ff2[l])
            new_sk.append(k_c)
            new_sv.append(v_c)
            new_gla.append(s_g)
            lat_mix = functools.partial(_odd_latent, rope=rope_swa, w_in=w_in_odd[o], w_out=w_out_odd[o],
                                        sink=swa_sink[o], w2=gla_w2[o], b2=gla_b[o], ck=cache_swa_k[:, o],
                                        cv=cache_swa_v[:, o], st_gla=state_gla[:, o])
            xs, _ = _block(xs, ml, lat_mix, ln_g[l], ln_b[l], w_ff1[l], w_ff2[l])
    state_ret_new = jnp.stack(new_ret, axis=1)
    cache_diff_k_new = jnp.stack(new_dk, axis=1)
    cache_diff_v_new = jnp.stack(new_dv, axis=1)
    cache_swa_k_new = jnp.stack(new_sk, axis=1)
    cache_swa_v_new = jnp.stack(new_sv, axis=1)
    state_gla_new = jnp.stack(new_gla, axis=1)
    return (xp, xs, state_ret_new, cache_diff_k_new, cache_diff_v_new, cache_swa_k_new, cache_swa_v_new, state_gla_new)
```

```python
import functools
import math

import numpy as np
import jax
import jax.numpy as jnp
from jax import lax
from jax.experimental import pallas as pl
from jax.experimental.pallas import tpu as pltpu

F32 = jnp.float32
BF16 = jnp.bfloat16

D_MODEL = 1024
BATCH = 16
SEQ = 256
DEPTH = 4
DEC_BATCH = 4
DEC_SEQ = 2048
PAST_LEN = 256
GRID_W = 64
H_RET, DK_RET, DV_RET = 4, 64, 128
H_DIFF, DH_DIFF, DV_DIFF = 4, 64, 128
H_SWA, KV_SWA, DH_SWA = 8, 2, 64
SWA_GROUP = H_SWA // KV_SWA
WINDOW = 128
H_GLA, DK_GLA, DV_GLA = 4, 64, 128
GLA_RANK = 16
GLA_TAU = 16.0
D_FF = 4 * D_MODEL
ROPE_BASE = 10000.0
N_EVEN = (DEPTH + 1) // 2
N_ODD = DEPTH // 2
ALPHA = (2 * DEPTH) ** 0.25
EVEN_IN = 3072
ODD_IN = 2336
ODD_IN_PAD = 2432
EPS = 1e-5

E_RQ, E_RK, E_RV, E_RG, E_DQ, E_DK, E_DV = 0, 256, 512, 1024, 1536, 2048, 2560
O_SQ, O_GV, O_GR, O_GQ, O_GK, O_SK, O_SV, O_GLR = 0, 512, 1024, 1536, 1792, 2048, 2176, 2304


def _reorder_odd(w):
    sq, sk, sv, gq, gk, gv, gr, glr = jnp.split(w, [512, 640, 768, 1024, 1280, 1792, 2304], axis=-1)
    pad = jnp.zeros(w.shape[:-1] + (ODD_IN_PAD - ODD_IN,), w.dtype)
    return jnp.concatenate([sq, gv, gr, gq, gk, sk, sv, glr, pad], axis=-1)

N_CTX = BATCH * SEQ
N_LAT = DEC_BATCH * DEC_SEQ
N_ALL = N_CTX + N_LAT
ROW_TILE = 256
N_COND = 8

VMEM_LIMIT = 56 * 1024 * 1024


def _cparams(n_axes):
    return pltpu.CompilerParams(dimension_semantics=("arbitrary",) * n_axes,
                                vmem_limit_bytes=VMEM_LIMIT)


def _dot(a, b):
    return jnp.dot(a, b, preferred_element_type=F32)


def _dot_nt(a, b):
    return lax.dot_general(a, b, (((1,), (1,)), ((), ())), preferred_element_type=F32)


def _silu(x):
    return x * (1.0 / (1.0 + jnp.exp(-x)))


def _norm_rows(x):
    mu = jnp.mean(x, axis=-1, keepdims=True)
    xc = x - mu
    var = jnp.mean(xc * xc, axis=-1, keepdims=True)
    return xc * lax.rsqrt(var + EPS)


def _mod_row(i):
    n_ctx_tiles = N_CTX // ROW_TILE
    tiles_per_req = DEC_SEQ // ROW_TILE
    return jnp.where(i < n_ctx_tiles, 0, 1 + (i - n_ctx_tiles) // tiles_per_req)


def _mod_body(c_ref, w_ref, b_ref, o_ref):
    c = _silu(c_ref[...]).astype(BF16)
    o_ref[...] = _dot(c, w_ref[...].astype(BF16)) + b_ref[...]


def _modulation(cond, w_mod, b_mod):
    tn = 1536
    return pl.pallas_call(
        _mod_body,
        grid=(DEPTH, 6 * D_MODEL // tn),
        in_specs=[pl.BlockSpec((N_COND, D_MODEL), lambda l, j: (0, 0)),
                  pl.BlockSpec((None, D_MODEL, tn), lambda l, j: (l, 0, j)),
                  pl.BlockSpec((None, 1, tn), lambda l, j: (l, 0, j))],
        out_specs=pl.BlockSpec((None, N_COND, tn), lambda l, j: (l, 0, j)),
        out_shape=jax.ShapeDtypeStruct((DEPTH, N_COND, 6 * D_MODEL), F32),
        compiler_params=_cparams(2),
        name="modulation",
    )(cond, w_mod, b_mod.reshape(DEPTH, 1, 6 * D_MODEL))


def _inproj_body(x_ref, mod_ref, w_ref, o_ref):
    sh = mod_ref[:, 0:D_MODEL]
    sc = mod_ref[:, D_MODEL:2 * D_MODEL]
    h = (x_ref[...] * (1.0 + sc) + sh).astype(BF16)
    o_ref[...] = _dot(h, w_ref[...])


def _in_projection(x, mod_l, w):
    n_in = w.shape[1]
    return pl.pallas_call(
        _inproj_body,
        grid=(N_ALL // ROW_TILE,),
        in_specs=[pl.BlockSpec((ROW_TILE, D_MODEL), lambda i: (i, 0)),
                  pl.BlockSpec((None, 1, 6 * D_MODEL), lambda i: (_mod_row(i), 0, 0)),
                  pl.BlockSpec((D_MODEL, n_in), lambda i: (0, 0))],
        out_specs=pl.BlockSpec((ROW_TILE, n_in), lambda i: (i, 0)),
        out_shape=jax.ShapeDtypeStruct((N_ALL, n_in), F32),
        compiler_params=_cparams(1),
        name="in_projection",
    )(x, mod_l, w)


def _layer_norm(x, g, b):
    return _norm_rows(x) * g + b


def _post_body(x_ref, ma_ref, mb_ref, mod_ref, wo_ref, g_ref, b_ref, w1_ref, w2_ref, o_ref):
    half = D_MODEL // 2
    gt1 = mod_ref[:, 2 * D_MODEL:3 * D_MODEL]
    sh2 = mod_ref[:, 3 * D_MODEL:4 * D_MODEL]
    sc2 = mod_ref[:, 4 * D_MODEL:5 * D_MODEL]
    gt2 = mod_ref[:, 5 * D_MODEL:6 * D_MODEL]
    mix = _dot(ma_ref[...], wo_ref[0:half, :]) + _dot(mb_ref[...], wo_ref[half:D_MODEL, :])
    x1 = _layer_norm(ALPHA * x_ref[...] + gt1 * mix, g_ref[0:1, :], b_ref[0:1, :])
    hf = (x1 * (1.0 + sc2) + sh2).astype(BF16)
    ff = jnp.zeros((ROW_TILE, D_MODEL), F32)
    chunk = 1024
    for j in range(D_FF // chunk):
        h1 = _dot(hf, w1_ref[:, j * chunk:(j + 1) * chunk])
        h1 = jnp.square(jnp.maximum(h1, 0.0)).astype(BF16)
        ff = ff + _dot(h1, w2_ref[j * chunk:(j + 1) * chunk, :])
    o_ref[...] = _layer_norm(ALPHA * x1 + gt2 * ff, g_ref[1:2, :], b_ref[1:2, :])


def _post_mixer(x, mix_a, mix_b, mod_l, w_out, ln_g, ln_b, w1, w2):
    half = D_MODEL // 2
    const = lambda i: (0, 0)
    return pl.pallas_call(
        _post_body,
        grid=(N_ALL // ROW_TILE,),
        in_specs=[pl.BlockSpec((ROW_TILE, D_MODEL), lambda i: (i, 0)),
                  pl.BlockSpec((ROW_TILE, half), lambda i: (i, 0)),
                  pl.BlockSpec((ROW_TILE, half), lambda i: (i, 0)),
                  pl.BlockSpec((None, 1, 6 * D_MODEL), lambda i: (_mod_row(i), 0, 0)),
                  pl.BlockSpec((D_MODEL, D_MODEL), const),
                  pl.BlockSpec((2, D_MODEL), const),
                  pl.BlockSpec((2, D_MODEL), const),
                  pl.BlockSpec((D_MODEL, D_FF), const),
                  pl.BlockSpec((D_FF, D_MODEL), const)],
        out_specs=pl.BlockSpec((ROW_TILE, D_MODEL), lambda i: (i, 0)),
        out_shape=jax.ShapeDtypeStruct((N_ALL, D_MODEL), F32),
        compiler_params=_cparams(1),
        name="post_mixer",
    )(x, mix_a, mix_b, mod_l, w_out, ln_g, ln_b, w1, w2)


def _rope_tables(rows, dim):
    row = jnp.repeat(jnp.arange(rows), GRID_W).astype(F32)
    col = jnp.tile(jnp.arange(GRID_W), rows).astype(F32)
    half = dim // 2
    freqs = ROPE_BASE ** (-jnp.arange(0, half, 2, dtype=F32) / half)
    ar = row[:, None] * freqs
    ac = col[:, None] * freqs
    cr, sr, cc, sc = jnp.cos(ar), jnp.sin(ar), jnp.cos(ac), jnp.sin(ac)
    cos = jnp.concatenate([cr, cr, cc, cc], axis=-1)
    sin = jnp.concatenate([-sr, sr, -sc, sc], axis=-1)
    reps = 128 // dim
    return jnp.tile(cos, (1, reps)), jnp.tile(sin, (1, reps))


def _rope128(x, cos, sin):
    lane = lax.broadcasted_iota(jnp.int32, x.shape, 1)
    first = (lane & 31) < 16
    swapped = jnp.where(first, pltpu.roll(x, 112, 1), pltpu.roll(x, 16, 1))
    return x * cos + swapped * sin


RET_CHUNK = 256


def _ret_body(q_ref, k_ref, v_ref, g_ref, rdk_ref, rdh_ref, s0_ref, o_ref, sfin_ref, *, seq):
    C = RET_CHUNK
    n_chunks = seq // C
    scale = DK_RET ** -0.5
    ii = lax.broadcasted_iota(jnp.int32, (C, C), 0)
    jj = lax.broadcasted_iota(jnp.int32, (C, C), 1)
    dist = (ii - jj).astype(F32)
    adist = jnp.abs(dist)
    pos = lax.broadcasted_iota(jnp.int32, (C, H_RET * DK_RET), 0).astype(F32)
    lgf = -jnp.exp(rdk_ref[0])
    lgb = -jnp.exp(rdk_ref[1])
    k_wf = jnp.exp(lgf * (C - 1.0 - pos)) * scale
    k_wb = jnp.exp(lgb * pos) * scale
    q_wf = jnp.exp(lgf * (pos + 1.0))
    q_wb = jnp.exp(lgb * (C - pos))

    uf, ub = [], []
    for n in range(n_chunks):
        kn = k_ref[n * C:(n + 1) * C, :]
        kft = (kn * k_wf).T.astype(BF16)
        kbt = (kn * k_wb).T.astype(BF16)
        ufn, ubn = [], []
        for h in range(H_RET):
            vh = v_ref[n * C:(n + 1) * C, h * DV_RET:(h + 1) * DV_RET].astype(BF16)
            ufn.append(_dot(kft[h * DK_RET:(h + 1) * DK_RET, :], vh))
            ubn.append(_dot(kbt[h * DK_RET:(h + 1) * DK_RET, :], vh))
        uf.append(ufn)
        ub.append(ubn)

    for h in range(H_RET):
        lgf_h = -jnp.exp(rdh_ref[0, h])
        lgb_h = -jnp.exp(rdh_ref[1, h])
        cf = jnp.exp(lgf_h * float(C))[:, 0:DV_RET]
        cb = jnp.exp(lgb_h * float(C))[:, 0:DV_RET]
        dmat = jnp.exp(jnp.where(dist > 0, lgf_h, lgb_h) * adist) + jnp.where(dist == 0, 1.0, 0.0)
        sf = [s0_ref[0, h]]
        for n in range(n_chunks):
            sf.append(cf * sf[n] + uf[n][h])
        sb = [None] * n_chunks
        sb[n_chunks - 1] = s0_ref[1, h]
        for n in range(n_chunks - 1, 0, -1):
            sb[n - 1] = cb * sb[n] + ub[n][h]
        sfin_ref[0, h] = sf[n_chunks]
        sfin_ref[1, h] = cb * sb[0] + ub[0][h]
        lanes = slice(h * DK_RET, (h + 1) * DK_RET)
        for n in range(n_chunks):
            rows = slice(n * C, (n + 1) * C)
            qh = q_ref[rows, lanes]
            kh = (k_ref[rows, lanes] * scale).astype(BF16)
            vh = v_ref[rows, h * DV_RET:(h + 1) * DV_RET].astype(BF16)
            s = _dot_nt(qh.astype(BF16), kh) * dmat
            o = _dot(s.astype(BF16), vh)
            o = o + _dot((qh * q_wf[:, lanes]).astype(BF16), sf[n].astype(BF16))
            o = o + _dot((qh * q_wb[:, lanes]).astype(BF16), sb[n].astype(BF16))
            gate = _silu(g_ref[rows, h * DV_RET:(h + 1) * DV_RET])
            o_ref[rows, h * DV_RET:(h + 1) * DV_RET] = (_norm_rows(o) * gate).astype(BF16)


def _retention(proj, ret_decay_e, s0, *, n_seq, seq, row0):
    rb = row0 // seq
    rdk = jnp.repeat(ret_decay_e, DK_RET, axis=-1).reshape(2, 1, H_RET * DK_RET)
    rdh = jnp.broadcast_to(ret_decay_e[:, :, None, None], (2, H_RET, 1, RET_CHUNK))
    hk, hv = H_RET * DK_RET, H_RET * DV_RET
    return pl.pallas_call(
        functools.partial(_ret_body, seq=seq),
        grid=(n_seq,),
        in_specs=[pl.BlockSpec((seq, hk), lambda b: (rb + b, E_RQ // hk)),
                  pl.BlockSpec((seq, hk), lambda b: (rb + b, E_RK // hk)),
                  pl.BlockSpec((seq, hv), lambda b: (rb + b, E_RV // hv)),
                  pl.BlockSpec((seq, hv), lambda b: (rb + b, E_RG // hv)),
                  pl.BlockSpec((2, 1, hk), lambda b: (0, 0, 0)),
                  pl.BlockSpec((2, H_RET, 1, RET_CHUNK), lambda b: (0, 0, 0, 0)),
                  pl.BlockSpec((None, 2, H_RET, DK_RET, DV_RET), lambda b: (b, 0, 0, 0, 0))],
        out_specs=[pl.BlockSpec((seq, hv), lambda b: (b, 0)),
                   pl.BlockSpec((None, 2, H_RET, DK_RET, DV_RET), lambda b: (b, 0, 0, 0, 0))],
        out_shape=[jax.ShapeDtypeStruct((n_seq * seq, hv), BF16),
                   jax.ShapeDtypeStruct((n_seq, 2, H_RET, DK_RET, DV_RET), F32)],
        compiler_params=_cparams(1),
        name="retention",
    )(proj, proj, proj, proj, rdk, rdh, s0)


DIFF_TQ = 256


def _diff_lambda(p, lam_init):
    a = jnp.sum(p[0:1, :] * p[1:2, :], axis=-1, keepdims=True)
    b = jnp.sum(p[2:3, :] * p[3:4, :], axis=-1, keepdims=True)
    return jnp.exp(a) - jnp.exp(b) + lam_init


def _softmax_parts(parts):
    m = parts[0].max(axis=-1, keepdims=True)
    for p in parts[1:]:
        m = jnp.maximum(m, p.max(axis=-1, keepdims=True))
    es = [jnp.exp(p - m) for p in parts]
    den = es[0].sum(axis=-1, keepdims=True)
    for e in es[1:]:
        den = den + e.sum(axis=-1, keepdims=True)
    return es, 1.0 / den


def _diff_body(*refs, lam_init, latent):
    if latent:
        (lam_ref, q_ref, k_ref, v_ref, ck_ref, cv_ref, cq_ref, sq_ref, ckk_ref, skk_ref,
         o_ref, kr_ref) = refs
        qi = pl.program_id(2)

        @pl.when(qi == 0)
        def _():
            kr_ref[...] = _rope128(k_ref[...], ckk_ref[...], skk_ref[...])

        q = _rope128(q_ref[...], cq_ref[...], sq_ref[...])
        kk = kr_ref[...]
    else:
        lam_ref, q_ref, k_ref, v_ref, o_ref = refs
        q = q_ref[...]
        kk = k_ref[...]
    lam = _diff_lambda(lam_ref[...], lam_init)
    q = q * (DH_DIFF ** -0.5)
    parts = [[], []]
    for m in range(2):
        qm = q[:, m * DH_DIFF:(m + 1) * DH_DIFF].astype(BF16)
        if latent:
            parts[m].append(_dot_nt(qm, ck_ref[m].astype(BF16)))
        parts[m].append(_dot_nt(qm, kk[:, m * DH_DIFF:(m + 1) * DH_DIFF].astype(BF16)))
    e1, inv1 = _softmax_parts(parts[0])
    e2, inv2 = _softmax_parts(parts[1])
    inv2 = inv2 * lam
    vals = ([cv_ref[...]] if latent else []) + [v_ref[...]]
    o = None
    for a, b, vv in zip(e1, e2, vals):
        w = (a * inv1 - b * inv2).astype(BF16)
        t = _dot(w, vv.astype(BF16))
        o = t if o is None else o + t
    o_ref[...] = (_norm_rows(o) * (1.0 - lam_init)).astype(BF16)


def _diff_attention(proj, lam_p, lam_init, *, n_seq, seq, row0, cache=None, rope=None):
    latent = cache is not None
    tq = DIFF_TQ
    nq = seq // tq
    rb_q = row0 // tq
    rb_k = row0 // seq
    q_col, k_col, v_col = E_DQ // 128, E_DK // 128, E_DV // 128
    in_specs = [pl.BlockSpec((4, DH_DIFF), lambda b, h, i: (0, 0)),
                pl.BlockSpec((tq, 128), lambda b, h, i: (rb_q + b * nq + i, q_col + h)),
                pl.BlockSpec((seq, 128), lambda b, h, i: (rb_k + b, k_col + h)),
                pl.BlockSpec((seq, 128), lambda b, h, i: (rb_k + b, v_col + h))]
    args = [lam_p, proj, proj, proj]
    scratch = []
    if latent:
        ck, cv = cache
        cos, sin = rope
        in_specs += [pl.BlockSpec((None, None, 2, PAST_LEN, DH_DIFF), lambda b, h, i: (b, h, 0, 0, 0)),
                     pl.BlockSpec((None, None, PAST_LEN, DV_DIFF), lambda b, h, i: (b, h, 0, 0)),
                     pl.BlockSpec((tq, 128), lambda b, h, i: (i, 0)),
                     pl.BlockSpec((tq, 128), lambda b, h, i: (i, 0)),
                     pl.BlockSpec((seq, 128), lambda b, h, i: (0, 0)),
                     pl.BlockSpec((seq, 128), lambda b, h, i: (0, 0))]
        args += [ck, cv, cos, sin, cos, sin]
        scratch = [pltpu.VMEM((seq, 128), F32)]
    return pl.pallas_call(
        functools.partial(_diff_body, lam_init=lam_init, latent=latent),
        grid=(n_seq, H_DIFF, nq),
        in_specs=in_specs,
        out_specs=pl.BlockSpec((tq, 128), lambda b, h, i: (b * nq + i, h)),
        out_shape=jax.ShapeDtypeStruct((n_seq * seq, H_DIFF * DV_DIFF), BF16),
        scratch_shapes=scratch,
        compiler_params=_cparams(3),
        name="diff_attention",
    )(*args)


SWA_TQ = 256
SWA_WIN = SWA_TQ + 2 * WINDOW
NEG = -1e30


def _swa_body(*refs, latent, seq):
    if latent:
        (q_ref, k_ref, v_ref, sink_ref, ck_ref, cv_ref, cq_ref, sq_ref, ckk_ref, skk_ref,
         o_ref, kr_ref) = refs
        qi = pl.program_id(1)

        @pl.when(qi == 0)
        def _():
            kr_ref[...] = _rope128(k_ref[...], ckk_ref[...], skk_ref[...])

        cq, sq = cq_ref[...], sq_ref[...]
        q = jnp.concatenate([_rope128(q_ref[:, g * 128:(g + 1) * 128], cq, sq)
                             for g in range(H_SWA * DH_SWA // 128)], axis=1)
        ws = pl.multiple_of(jnp.clip(qi * SWA_TQ - WINDOW, 0, seq - SWA_WIN), 128)
        kk = kr_ref[pl.ds(ws, SWA_WIN), :]
        vv = v_ref[pl.ds(ws, SWA_WIN), :]
        qpos = qi * SWA_TQ + lax.broadcasted_iota(jnp.int32, (SWA_TQ, SWA_WIN), 0)
        kpos = ws + lax.broadcasted_iota(jnp.int32, (SWA_TQ, SWA_WIN), 1)
        allowed = jnp.abs(qpos - kpos) <= WINDOW
    else:
        q_ref, k_ref, v_ref, sink_ref, o_ref = refs
        q = q_ref[...]
        kk = k_ref[...]
        vv = v_ref[...]
    q = q * (DH_SWA ** -0.5)
    outs = []
    for kv in range(KV_SWA):
        lanes = slice(kv * DH_SWA, (kv + 1) * DH_SWA)
        kh = kk[:, lanes].astype(BF16)
        vh = vv[:, lanes].astype(BF16)
        for g in range(SWA_GROUP):
            h = kv * SWA_GROUP + g
            qh = q[:, h * DH_SWA:(h + 1) * DH_SWA].astype(BF16)
            sink = sink_ref[h][:, 0:1]
            s_loc = _dot_nt(qh, kh)
            if latent:
                s_loc = jnp.where(allowed, s_loc, NEG)
                s_ctx = _dot_nt(qh, ck_ref[kv].astype(BF16))
                m = jnp.maximum(s_loc.max(axis=-1, keepdims=True), s_ctx.max(axis=-1, keepdims=True))
            else:
                m = s_loc.max(axis=-1, keepdims=True)
            m = jnp.maximum(m, sink)
            e_loc = jnp.exp(s_loc - m)
            den = e_loc.sum(axis=-1, keepdims=True) + jnp.exp(sink - m)
            if latent:
                e_ctx = jnp.exp(s_ctx - m)
                den = den + e_ctx.sum(axis=-1, keepdims=True)
            inv = 1.0 / den
            o = _dot((e_loc * inv).astype(BF16), vh)
            if latent:
                o = o + _dot((e_ctx * inv).astype(BF16), cv_ref[kv].astype(BF16))
            outs.append(o)
    o_ref[...] = jnp.concatenate(outs, axis=1).astype(BF16)


def _swa_attention(proj, sink, *, n_seq, seq, row0, cache=None, rope=None):
    latent = cache is not None
    tq = SWA_TQ
    nq = seq // tq
    rb_q = row0 // tq
    rb_k = row0 // seq
    hq = H_SWA * DH_SWA
    sink_b = jnp.broadcast_to(sink[:, None, None], (H_SWA, 1, 128))
    in_specs = [pl.BlockSpec((tq, hq), lambda b, i: (rb_q + b * nq + i, 0)),
                pl.BlockSpec((seq, 128), lambda b, i: (rb_k + b, O_SK // 128)),
                pl.BlockSpec((seq, 128), lambda b, i: (rb_k + b, O_SV // 128)),
                pl.BlockSpec((H_SWA, 1, 128), lambda b, i: (0, 0, 0))]
    args = [proj, proj, proj, sink_b]
    scratch = []
    if latent:
        ck, cv = cache
        cos, sin = rope
        in_specs += [pl.BlockSpec((None, KV_SWA, PAST_LEN, DH_SWA), lambda b, i: (b, 0, 0, 0)),
                     pl.BlockSpec((None, KV_SWA, PAST_LEN, DH_SWA), lambda b, i: (b, 0, 0, 0)),
                     pl.BlockSpec((tq, 128), lambda b, i: (i, 0)),
                     pl.BlockSpec((tq, 128), lambda b, i: (i, 0)),
                     pl.BlockSpec((seq, 128), lambda b, i: (0, 0)),
                     pl.BlockSpec((seq, 128), lambda b, i: (0, 0))]
        args += [ck, cv, cos, sin, cos, sin]
        scratch = [pltpu.VMEM((seq, 128), F32)]
    return pl.pallas_call(
        functools.partial(_swa_body, latent=latent, seq=seq),
        grid=(n_seq, nq),
        in_specs=in_specs,
        out_specs=pl.BlockSpec((tq, hq), lambda b, i: (b * nq + i, 0)),
        out_shape=jax.ShapeDtypeStruct((n_seq * seq, hq), BF16),
        scratch_shapes=scratch,
        compiler_params=_cparams(2),
        name="swa_attention",
    )(*args)


GLA_BLOCK = 128
GLA_LEVELS = 7


def _split3(x):
    hi = x.astype(BF16)
    r1 = x - hi.astype(F32)
    mid = r1.astype(BF16)
    lo = (r1 - mid.astype(F32)).astype(BF16)
    return hi, mid, lo


def _gla_body(q_ref, k_ref, v_ref, r_ref, w2_ref, b2_ref, s0_ref, o_ref, sfin_ref, st_ref, *, n_blocks):
    d = pl.program_id(0)
    n = pl.program_id(2)
    T = GLA_BLOCK
    hk = H_GLA * DK_GLA

    @pl.when(n == 0)
    def _():
        st_ref[...] = s0_ref[...]

    q = q_ref[...] * (DK_GLA ** -0.5)
    k = k_ref[...]
    v = v_ref[...]
    rs = _split3(r_ref[...])
    ws = _split3(w2_ref[...])
    z = b2_ref[...]
    for a_i in range(3):
        for b_i in range(3 - a_i):
            z = z + _dot(rs[a_i], ws[b_i])
    la = (jnp.minimum(z, 0.0) - jnp.log(1.0 + jnp.exp(-jnp.abs(z)))) * (1.0 / GLA_TAU)

    t_idx = lax.broadcasted_iota(jnp.int32, (T, hk), 0)
    xr = lax.broadcasted_iota(jnp.int32, (T, T), 0) ^ lax.broadcasted_iota(jnp.int32, (T, T), 1)
    heads = [slice(h * DK_GLA, (h + 1) * DK_GLA) for h in range(H_GLA)]
    qb, kb = q.astype(BF16), k.astype(BF16)
    att = [_dot_nt(qb[:, hs], kb[:, hs]) for hs in heads]
    x = la
    tot = la
    for lvl in range(GLA_LEVELS):
        sz = 1 << lvl
        upper = (t_idx >> lvl) & 1
        is_q = (upper ^ d) == 1
        partner = jnp.where(upper == 1, pltpu.roll(tot, sz, 0), pltpu.roll(tot, T - sz, 0))
        e = jnp.exp(jnp.where(is_q, x, tot - x))
        qs = jnp.where(is_q, q * e, 0.0).astype(BF16)
        ks = jnp.where(is_q, 0.0, k * e).astype(BF16)
        for h, hs in enumerate(heads):
            att[h] = jnp.where(xr >= sz, _dot_nt(qs[:, hs], ks[:, hs]), att[h])
        x = x + jnp.where(is_q, partner, 0.0)
        tot = tot + partner

    qe = (q * jnp.exp(x)).astype(BF16)
    kw = (k * jnp.exp(tot - x)).astype(BF16)
    vt = v.T
    vb = v.astype(BF16)
    dec = jnp.exp(tot[0:1, :])
    for h, hs in enumerate(heads):
        vs = slice(h * DV_GLA, (h + 1) * DV_GLA)
        st = st_ref[h]
        o = _dot(att[h].astype(BF16), vb[:, vs]) + _dot_nt(qe[:, hs], st.astype(BF16))
        o_ref[:, vs] = o
        st_ref[h] = st * dec[:, hs] + _dot(vt[vs, :].astype(BF16), kw[:, hs])

    @pl.when(n == n_blocks - 1)
    def _():
        sfin_ref[...] = st_ref[...]


def _gla_finish_body(of_ref, ob_ref, g_ref, o_ref):
    o = of_ref[...] + ob_ref[...]
    for h in range(H_GLA):
        vs = slice(h * DV_GLA, (h + 1) * DV_GLA)
        o_ref[:, vs] = (_norm_rows(o[:, vs]) * _silu(g_ref[:, vs])).astype(BF16)


def _gla(proj, w2, b2, s0t, *, n_seq, seq, row0):
    nb = seq // GLA_BLOCK
    rb = row0 // GLA_BLOCK
    hk, hv = H_GLA * DK_GLA, H_GLA * DV_GLA
    w2p = jnp.zeros((2, 128, hk), F32)
    w2p = w2p.at[0, 0:GLA_RANK].set(w2[0]).at[1, GLA_RANK:2 * GLA_RANK].set(w2[1])

    def blk(d, b, n):
        return rb + b * nb + n + d * (nb - 1 - 2 * n)

    def oblk(d, b, n):
        return b * nb + n + d * (nb - 1 - 2 * n)

    o2, sfin = pl.pallas_call(
        functools.partial(_gla_body, n_blocks=nb),
        grid=(2, n_seq, nb),
        in_specs=[pl.BlockSpec((GLA_BLOCK, hk), lambda d, b, n: (blk(d, b, n), O_GQ // hk)),
                  pl.BlockSpec((GLA_BLOCK, hk), lambda d, b, n: (blk(d, b, n), O_GK // hk)),
                  pl.BlockSpec((GLA_BLOCK, hv), lambda d, b, n: (blk(d, b, n), O_GV // hv)),
                  pl.BlockSpec((GLA_BLOCK, 128), lambda d, b, n: (blk(d, b, n), O_GLR // 128)),
                  pl.BlockSpec((None, 128, hk), lambda d, b, n: (d, 0, 0)),
                  pl.BlockSpec((None, 1, hk), lambda d, b, n: (d, 0, 0)),
                  pl.BlockSpec((None, None, H_GLA, DV_GLA, DK_GLA), lambda d, b, n: (d, b, 0, 0, 0))],
        out_specs=[pl.BlockSpec((None, GLA_BLOCK, hv), lambda d, b, n: (d, oblk(d, b, n), 0)),
                   pl.BlockSpec((None, None, H_GLA, DV_GLA, DK_GLA), lambda d, b, n: (d, b, 0, 0, 0))],
        out_shape=[jax.ShapeDtypeStruct((2, n_seq * seq, hv), F32),
                   jax.ShapeDtypeStruct((2, n_seq, H_GLA, DV_GLA, DK_GLA), F32)],
        scratch_shapes=[pltpu.VMEM((H_GLA, DV_GLA, DK_GLA), F32)],
        compiler_params=_cparams(3),
        name="gla",
    )(proj, proj, proj, proj, w2p, b2.reshape(2, 1, hk), s0t)

    rt = ROW_TILE
    rbt = row0 // rt
    out = pl.pallas_call(
        _gla_finish_body,
        grid=(n_seq * seq // rt,),
        in_specs=[pl.BlockSpec((None, rt, hv), lambda i: (0, i, 0)),
                  pl.BlockSpec((None, rt, hv), lambda i: (1, i, 0)),
                  pl.BlockSpec((rt, hv), lambda i: (rbt + i, O_GR // hv))],
        out_specs=pl.BlockSpec((rt, hv), lambda i: (i, 0)),
        out_shape=jax.ShapeDtypeStruct((n_seq * seq, hv), BF16),
        compiler_params=_cparams(1),
        name="gla_finish",
    )(o2, o2, proj)
    return out, sfin


def kernel(x_prompt, x_sample, c, state_ret, cache_diff_k, cache_diff_v, cache_swa_k, cache_swa_v, state_gla,
           c_ctx, w_mod, b_mod, ln_g, ln_b, w_in_even, w_out_even, ret_decay, diff_lam, w_in_odd, w_out_odd,
           swa_sink, gla_w2, gla_b, w_ff1, w_ff2):
    x = jnp.concatenate([x_prompt.reshape(N_CTX, D_MODEL), x_sample.reshape(N_LAT, D_MODEL)], axis=0)
    cond = jnp.concatenate([c_ctx[None, :], c, jnp.zeros((N_COND - 1 - DEC_BATCH, D_MODEL), F32)], axis=0)
    mod = _modulation(cond, w_mod, b_mod).reshape(DEPTH, N_COND, 1, 6 * D_MODEL)
    rope = _rope_tables(DEC_SEQ // GRID_W, DH_DIFF)

    w_in_even_b = w_in_even.astype(BF16)
    w_in_odd_b = _reorder_odd(w_in_odd).astype(BF16)
    w_out_even_b = w_out_even.astype(BF16)
    w_out_odd_b = w_out_odd.astype(BF16)
    w_ff1_b = w_ff1.astype(BF16)
    w_ff2_b = w_ff2.astype(BF16)

    ctx = dict(n_seq=BATCH, seq=SEQ, row0=0)
    lat = dict(n_seq=DEC_BATCH, seq=DEC_SEQ, row0=N_CTX)
    new_ret, new_dk, new_dv, new_sk, new_sv, new_gla = [], [], [], [], [], []
    for l in range(DEPTH):
        if l % 2 == 0:
            e = l // 2
            lam_init = 0.8 - 0.6 * math.exp(-0.3 * l)
            proj = _in_projection(x, mod[l], w_in_even_b[e])
            zero = jnp.zeros((BATCH, 2, H_RET, DK_RET, DV_RET), F32)
            a_ctx, s_ret = _retention(proj, ret_decay[e], zero, **ctx)
            a_lat, _ = _retention(proj, ret_decay[e], state_ret[:, e], **lat)
            b_ctx = _diff_attention(proj, diff_lam[e], lam_init, **ctx)
            b_lat = _diff_attention(proj, diff_lam[e], lam_init, cache=(cache_diff_k[:, e], cache_diff_v[:, e]),
                                    rope=rope, **lat)
            pc = proj[:N_CTX]
            new_ret.append(s_ret)
            new_dk.append(pc[:, E_DK:E_DK + 512].reshape(BATCH, SEQ, H_DIFF, 2, DH_DIFF).transpose(0, 2, 3, 1, 4))
            new_dv.append(pc[:, E_DV:E_DV + 512].reshape(BATCH, SEQ, H_DIFF, DV_DIFF).transpose(0, 2, 1, 3))
            w_out = w_out_even_b[e]
        else:
            o = l // 2
            proj = _in_projection(x, mod[l], w_in_odd_b[o])
            a_ctx = _swa_attention(proj, swa_sink[o], **ctx)
            a_lat = _swa_attention(proj, swa_sink[o], cache=(cache_swa_k[:, o], cache_swa_v[:, o]), rope=rope, **lat)
            zero = jnp.zeros((2, BATCH, H_GLA, DV_GLA, DK_GLA), F32)
            b_ctx, s_gla = _gla(proj, gla_w2[o], gla_b[o], zero, **ctx)
            s0t = state_gla[:, o].transpose(1, 0, 2, 4, 3)
            b_lat, _ = _gla(proj, gla_w2[o], gla_b[o], s0t, **lat)
            pc = proj[:N_CTX]
            new_sk.append(pc[:, O_SK:O_SK + 128].reshape(BATCH, SEQ, KV_SWA, DH_SWA).transpose(0, 2, 1, 3))
            new_sv.append(pc[:, O_SV:O_SV + 128].reshape(BATCH, SEQ, KV_SWA, DH_SWA).transpose(0, 2, 1, 3))
            new_gla.append(s_gla.transpose(1, 0, 2, 4, 3))
            w_out = w_out_odd_b[o]
        mix_a = jnp.concatenate([a_ctx, a_lat], axis=0)
        mix_b = jnp.concatenate([b_ctx, b_lat], axis=0)
        x = _post_mixer(x, mix_a, mix_b, mod[l], w_out, ln_g[l], ln_b[l], w_ff1_b[l], w_ff2_b[l])
    return (x[:N_CTX].reshape(BATCH, SEQ, D_MODEL), x[N_CTX:].reshape(DEC_BATCH, DEC_SEQ, D_MODEL),
            jnp.stack(new_ret, axis=1), jnp.stack(new_dk, axis=1), jnp.stack(new_dv, axis=1),
            jnp.stack(new_sk, axis=1), jnp.stack(new_sv, axis=1), jnp.stack(new_gla, axis=1))
```

```python
import functools
import math

import numpy as np
import jax
import jax.numpy as jnp
from jax import lax
from jax.experimental import pallas as pl
from jax.experimental.pallas import tpu as pltpu

F32 = jnp.float32
BF16 = jnp.bfloat16

D_MODEL = 1024
BATCH = 16
SEQ = 256
DEPTH = 4
DEC_BATCH = 4
DEC_SEQ = 2048
PAST_LEN = 256
GRID_W = 64
H_RET, DK_RET, DV_RET = 4, 64, 128
H_DIFF, DH_DIFF, DV_DIFF = 4, 64, 128
H_SWA, KV_SWA, DH_SWA = 8, 2, 64
SWA_GROUP = H_SWA // KV_SWA
WINDOW = 128
H_GLA, DK_GLA, DV_GLA = 4, 64, 128
GLA_RANK = 16
GLA_TAU = 16.0
D_FF = 4 * D_MODEL
ROPE_BASE = 10000.0
N_EVEN = (DEPTH + 1) // 2
N_ODD = DEPTH // 2
ALPHA = (2 * DEPTH) ** 0.25
EVEN_IN = 3072
ODD_IN = 2336
ODD_IN_PAD = 2432
EPS = 1e-5

E_RQ, E_RK, E_RV, E_RG, E_DQ, E_DK, E_DV = 0, 256, 512, 1024, 1536, 2048, 2560
O_SQ, O_GV, O_GR, O_GQ, O_GK, O_SK, O_SV, O_GLR = 0, 512, 1024, 1536, 1792, 2048, 2176, 2304


def _reorder_odd(w):
    sq, sk, sv, gq, gk, gv, gr, glr = jnp.split(w, [512, 640, 768, 1024, 1280, 1792, 2304], axis=-1)
    pad = jnp.zeros(w.shape[:-1] + (ODD_IN_PAD - ODD_IN,), w.dtype)
    return jnp.concatenate([sq, gv, gr, gq, gk, sk, sv, glr, pad], axis=-1)

N_CTX = BATCH * SEQ
N_LAT = DEC_BATCH * DEC_SEQ
N_ALL = N_CTX + N_LAT
ROW_TILE = 256
N_COND = 8

VMEM_LIMIT = 56 * 1024 * 1024


def _cparams(n_axes):
    return pltpu.CompilerParams(dimension_semantics=("arbitrary",) * n_axes,
                                vmem_limit_bytes=VMEM_LIMIT)


def _dot(a, b):
    return jnp.dot(a, b, preferred_element_type=F32)


def _dot_nt(a, b):
    return lax.dot_general(a, b, (((1,), (1,)), ((), ())), preferred_element_type=F32)


def _silu(x):
    return x * (1.0 / (1.0 + jnp.exp(-x)))


def _norm_rows(x):
    mu = jnp.mean(x, axis=-1, keepdims=True)
    xc = x - mu
    var = jnp.mean(xc * xc, axis=-1, keepdims=True)
    return xc * lax.rsqrt(var + EPS)


def _mod_row(i, latent):
    return 1 + i // (DEC_SEQ // ROW_TILE) if latent else 0


def _mod_body(c_ref, w_ref, b_ref, o_ref):
    c = _silu(c_ref[...]).astype(BF16)
    o_ref[...] = _dot(c, w_ref[...].astype(BF16)) + b_ref[...]


def _modulation(cond, w_mod, b_mod):
    tn = 1536
    return pl.pallas_call(
        _mod_body,
        grid=(DEPTH, 6 * D_MODEL // tn),
        in_specs=[pl.BlockSpec((N_COND, D_MODEL), lambda l, j: (0, 0)),
                  pl.BlockSpec((None, D_MODEL, tn), lambda l, j: (l, 0, j)),
                  pl.BlockSpec((None, 1, tn), lambda l, j: (l, 0, j))],
        out_specs=pl.BlockSpec((None, N_COND, tn), lambda l, j: (l, 0, j)),
        out_shape=jax.ShapeDtypeStruct((DEPTH, N_COND, 6 * D_MODEL), F32),
        compiler_params=_cparams(2),
        name="modulation",
    )(cond, w_mod, b_mod.reshape(DEPTH, 1, 6 * D_MODEL))


def _inproj_body(x_ref, mod_ref, w_ref, o_ref):
    sh = mod_ref[:, 0:D_MODEL]
    sc = mod_ref[:, D_MODEL:2 * D_MODEL]
    h = (x_ref[...] * (1.0 + sc) + sh).astype(BF16)
    o_ref[...] = _dot(h, w_ref[...])


def _in_projection(x, mod_l, w, latent):
    n_in = w.shape[1]
    n_rows = x.shape[0]
    return pl.pallas_call(
        _inproj_body,
        grid=(n_rows // ROW_TILE,),
        in_specs=[pl.BlockSpec((ROW_TILE, D_MODEL), lambda i: (i, 0)),
                  pl.BlockSpec((None, 1, 6 * D_MODEL), lambda i: (_mod_row(i, latent), 0, 0)),
                  pl.BlockSpec((D_MODEL, n_in), lambda i: (0, 0))],
        out_specs=pl.BlockSpec((ROW_TILE, n_in), lambda i: (i, 0)),
        out_shape=jax.ShapeDtypeStruct((n_rows, n_in), F32),
        compiler_params=_cparams(1),
        name="in_projection",
    )(x, mod_l, w)


def _layer_norm(x, g, b):
    return _norm_rows(x) * g + b


def _post_body(x_ref, ma_ref, mb_ref, mod_ref, wo_ref, g_ref, b_ref, w1_ref, w2_ref, o_ref):
    half = D_MODEL // 2
    gt1 = mod_ref[:, 2 * D_MODEL:3 * D_MODEL]
    sh2 = mod_ref[:, 3 * D_MODEL:4 * D_MODEL]
    sc2 = mod_ref[:, 4 * D_MODEL:5 * D_MODEL]
    gt2 = mod_ref[:, 5 * D_MODEL:6 * D_MODEL]
    mix = _dot(ma_ref[...], wo_ref[0:half, :]) + _dot(mb_ref[...], wo_ref[half:D_MODEL, :])
    x1 = _layer_norm(ALPHA * x_ref[...] + gt1 * mix, g_ref[0:1, :], b_ref[0:1, :])
    hf = (x1 * (1.0 + sc2) + sh2).astype(BF16)
    ff = jnp.zeros((ROW_TILE, D_MODEL), F32)
    chunk = 1024
    for j in range(D_FF // chunk):
        h1 = _dot(hf, w1_ref[:, j * chunk:(j + 1) * chunk])
        h1 = jnp.square(jnp.maximum(h1, 0.0)).astype(BF16)
        ff = ff + _dot(h1, w2_ref[j * chunk:(j + 1) * chunk, :])
    o_ref[...] = _layer_norm(ALPHA * x1 + gt2 * ff, g_ref[1:2, :], b_ref[1:2, :])


def _post_mixer(x, mix_a, mix_b, mod_l, w_out, ln_g, ln_b, w1, w2, latent):
    half = D_MODEL // 2
    n_rows = x.shape[0]
    const = lambda i: (0, 0)
    return pl.pallas_call(
        _post_body,
        grid=(n_rows // ROW_TILE,),
        in_specs=[pl.BlockSpec((ROW_TILE, D_MODEL), lambda i: (i, 0)),
                  pl.BlockSpec((ROW_TILE, half), lambda i: (i, 0)),
                  pl.BlockSpec((ROW_TILE, half), lambda i: (i, 0)),
                  pl.BlockSpec((None, 1, 6 * D_MODEL), lambda i: (_mod_row(i, latent), 0, 0)),
                  pl.BlockSpec((D_MODEL, D_MODEL), const),
                  pl.BlockSpec((2, D_MODEL), const),
                  pl.BlockSpec((2, D_MODEL), const),
                  pl.BlockSpec((D_MODEL, D_FF), const),
                  pl.BlockSpec((D_FF, D_MODEL), const)],
        out_specs=pl.BlockSpec((ROW_TILE, D_MODEL), lambda i: (i, 0)),
        out_shape=jax.ShapeDtypeStruct((n_rows, D_MODEL), F32),
        compiler_params=_cparams(1),
        name="post_mixer",
    )(x, mix_a, mix_b, mod_l, w_out, ln_g, ln_b, w1, w2)


def _rope_tables(rows, dim):
    row = jnp.repeat(jnp.arange(rows), GRID_W).astype(F32)
    col = jnp.tile(jnp.arange(GRID_W), rows).astype(F32)
    half = dim // 2
    freqs = ROPE_BASE ** (-jnp.arange(0, half, 2, dtype=F32) / half)
    ar = row[:, None] * freqs
    ac = col[:, None] * freqs
    cr, sr, cc, sc = jnp.cos(ar), jnp.sin(ar), jnp.cos(ac), jnp.sin(ac)
    cos = jnp.concatenate([cr, cr, cc, cc], axis=-1)
    sin = jnp.concatenate([-sr, sr, -sc, sc], axis=-1)
    reps = 128 // dim
    return jnp.tile(cos, (1, reps)), jnp.tile(sin, (1, reps))


def _rope128(x, cos, sin):
    lane = lax.broadcasted_iota(jnp.int32, x.shape, 1)
    first = (lane & 31) < 16
    swapped = jnp.where(first, pltpu.roll(x, 112, 1), pltpu.roll(x, 16, 1))
    return x * cos + swapped * sin


RET_CHUNK = 256


def _ret_body(q_ref, k_ref, v_ref, g_ref, rdk_ref, rdh_ref, s0_ref, o_ref, sfin_ref, *, seq):
    C = RET_CHUNK
    n_chunks = seq // C
    scale = DK_RET ** -0.5
    ii = lax.broadcasted_iota(jnp.int32, (C, C), 0)
    jj = lax.broadcasted_iota(jnp.int32, (C, C), 1)
    dist = (ii - jj).astype(F32)
    adist = jnp.abs(dist)
    pos = lax.broadcasted_iota(jnp.int32, (C, H_RET * DK_RET), 0).astype(F32)
    lgf = -jnp.exp(rdk_ref[0])
    lgb = -jnp.exp(rdk_ref[1])
    k_wf = jnp.exp(lgf * (C - 1.0 - pos)) * scale
    k_wb = jnp.exp(lgb * pos) * scale
    q_wf = jnp.exp(lgf * (pos + 1.0))
    q_wb = jnp.exp(lgb * (C - pos))

    uf, ub = [], []
    for n in range(n_chunks):
        kn = k_ref[n * C:(n + 1) * C, :]
        kft = (kn * k_wf).T.astype(BF16)
        kbt = (kn * k_wb).T.astype(BF16)
        ufn, ubn = [], []
        for h in range(H_RET):
            vh = v_ref[n * C:(n + 1) * C, h * DV_RET:(h + 1) * DV_RET].astype(BF16)
            ufn.append(_dot(kft[h * DK_RET:(h + 1) * DK_RET, :], vh))
            ubn.append(_dot(kbt[h * DK_RET:(h + 1) * DK_RET, :], vh))
        uf.append(ufn)
        ub.append(ubn)

    for h in range(H_RET):
        lgf_h = -jnp.exp(rdh_ref[0, h])
        lgb_h = -jnp.exp(rdh_ref[1, h])
        cf = jnp.exp(lgf_h * float(C))[:, 0:DV_RET]
        cb = jnp.exp(lgb_h * float(C))[:, 0:DV_RET]
        dmat = jnp.exp(jnp.where(dist > 0, lgf_h, lgb_h) * adist) + jnp.where(dist == 0, 1.0, 0.0)
        sf = [s0_ref[0, h]]
        for n in range(n_chunks):
            sf.append(cf * sf[n] + uf[n][h])
        sb = [None] * n_chunks
        sb[n_chunks - 1] = s0_ref[1, h]
        for n in range(n_chunks - 1, 0, -1):
            sb[n - 1] = cb * sb[n] + ub[n][h]
        sfin_ref[0, h] = sf[n_chunks]
        sfin_ref[1, h] = cb * sb[0] + ub[0][h]
        lanes = slice(h * DK_RET, (h + 1) * DK_RET)
        for n in range(n_chunks):
            rows = slice(n * C, (n + 1) * C)
            qh = q_ref[rows, lanes]
            kh = (k_ref[rows, lanes] * scale).astype(BF16)
            vh = v_ref[rows, h * DV_RET:(h + 1) * DV_RET].astype(BF16)
            s = _dot_nt(qh.astype(BF16), kh) * dmat
            o = _dot(s.astype(BF16), vh)
            o = o + _dot((qh * q_wf[:, lanes]).astype(BF16), sf[n].astype(BF16))
            o = o + _dot((qh * q_wb[:, lanes]).astype(BF16), sb[n].astype(BF16))
            gate = _silu(g_ref[rows, h * DV_RET:(h + 1) * DV_RET])
            o_ref[rows, h * DV_RET:(h + 1) * DV_RET] = (_norm_rows(o) * gate).astype(BF16)


def _retention(proj, ret_decay_e, s0, *, n_seq, seq, row0):
    rb = row0 // seq
    rdk = jnp.repeat(ret_decay_e, DK_RET, axis=-1).reshape(2, 1, H_RET * DK_RET)
    rdh = jnp.broadcast_to(ret_decay_e[:, :, None, None], (2, H_RET, 1, RET_CHUNK))
    hk, hv = H_RET * DK_RET, H_RET * DV_RET
    return pl.pallas_call(
        functools.partial(_ret_body, seq=seq),
        grid=(n_seq,),
        in_specs=[pl.BlockSpec((seq, hk), lambda b: (rb + b, E_RQ // hk)),
                  pl.BlockSpec((seq, hk), lambda b: (rb + b, E_RK // hk)),
                  pl.BlockSpec((seq, hv), lambda b: (rb + b, E_RV // hv)),
                  pl.BlockSpec((seq, hv), lambda b: (rb + b, E_RG // hv)),
                  pl.BlockSpec((2, 1, hk), lambda b: (0, 0, 0)),
                  pl.BlockSpec((2, H_RET, 1, RET_CHUNK), lambda b: (0, 0, 0, 0)),
                  pl.BlockSpec((None, 2, H_RET, DK_RET, DV_RET), lambda b: (b, 0, 0, 0, 0))],
        out_specs=[pl.BlockSpec((seq, hv), lambda b: (b, 0)),
                   pl.BlockSpec((None, 2, H_RET, DK_RET, DV_RET), lambda b: (b, 0, 0, 0, 0))],
        out_shape=[jax.ShapeDtypeStruct((n_seq * seq, hv), BF16),
                   jax.ShapeDtypeStruct((n_seq, 2, H_RET, DK_RET, DV_RET), F32)],
        compiler_params=_cparams(1),
        name="retention",
    )(proj, proj, proj, proj, rdk, rdh, s0)


DIFF_TQ = 256


def _diff_lambda(p, lam_init):
    a = jnp.sum(p[0:1, :] * p[1:2, :], axis=-1, keepdims=True)
    b = jnp.sum(p[2:3, :] * p[3:4, :], axis=-1, keepdims=True)
    return jnp.exp(a) - jnp.exp(b) + lam_init


def _softmax_parts(parts):
    m = parts[0].max(axis=-1, keepdims=True)
    for p in parts[1:]:
        m = jnp.maximum(m, p.max(axis=-1, keepdims=True))
    es = [jnp.exp(p - m) for p in parts]
    den = es[0].sum(axis=-1, keepdims=True)
    for e in es[1:]:
        den = den + e.sum(axis=-1, keepdims=True)
    return es, 1.0 / den


def _diff_body(*refs, lam_init, latent):
    if latent:
        (lam_ref, q_ref, k_ref, v_ref, ck_ref, cv_ref, cq_ref, sq_ref, ckk_ref, skk_ref,
         o_ref, kr_ref) = refs
        qi = pl.program_id(2)

        @pl.when(qi == 0)
        def _():
            kr_ref[...] = _rope128(k_ref[...], ckk_ref[...], skk_ref[...])

        q = _rope128(q_ref[...], cq_ref[...], sq_ref[...])
        kk = kr_ref[...]
    else:
        lam_ref, q_ref, k_ref, v_ref, o_ref, kc_ref, vc_ref = refs
        q = q_ref[...]
        kk = k_ref[...]
        kc_ref[0] = kk[:, 0:DH_DIFF]
        kc_ref[1] = kk[:, DH_DIFF:2 * DH_DIFF]
        vc_ref[...] = v_ref[...]
    lam = _diff_lambda(lam_ref[...], lam_init)
    q = q * (DH_DIFF ** -0.5)
    parts = [[], []]
    for m in range(2):
        qm = q[:, m * DH_DIFF:(m + 1) * DH_DIFF].astype(BF16)
        if latent:
            parts[m].append(_dot_nt(qm, ck_ref[m].astype(BF16)))
        parts[m].append(_dot_nt(qm, kk[:, m * DH_DIFF:(m + 1) * DH_DIFF].astype(BF16)))
    e1, inv1 = _softmax_parts(parts[0])
    e2, inv2 = _softmax_parts(parts[1])
    inv2 = inv2 * lam
    vals = ([cv_ref[...]] if latent else []) + [v_ref[...]]
    o = None
    for a, b, vv in zip(e1, e2, vals):
        w = (a * inv1 - b * inv2).astype(BF16)
        t = _dot(w, vv.astype(BF16))
        o = t if o is None else o + t
    o_ref[...] = (_norm_rows(o) * (1.0 - lam_init)).astype(BF16)


def _diff_attention(proj, lam_p, lam_init, *, n_seq, seq, row0, cache=None, rope=None):
    latent = cache is not None
    tq = DIFF_TQ
    nq = seq // tq
    rb_q = row0 // tq
    rb_k = row0 // seq
    q_col, k_col, v_col = E_DQ // 128, E_DK // 128, E_DV // 128
    in_specs = [pl.BlockSpec((4, DH_DIFF), lambda b, h, i: (0, 0)),
                pl.BlockSpec((tq, 128), lambda b, h, i: (rb_q + b * nq + i, q_col + h)),
                pl.BlockSpec((seq, 128), lambda b, h, i: (rb_k + b, k_col + h)),
                pl.BlockSpec((seq, 128), lambda b, h, i: (rb_k + b, v_col + h))]
    args = [lam_p, proj, proj, proj]
    scratch = []
    if latent:
        ck, cv = cache
        cos, sin = rope
        in_specs += [pl.BlockSpec((None, None, 2, PAST_LEN, DH_DIFF), lambda b, h, i: (b, h, 0, 0, 0)),
                     pl.BlockSpec((None, None, PAST_LEN, DV_DIFF), lambda b, h, i: (b, h, 0, 0)),
                     pl.BlockSpec((tq, 128), lambda b, h, i: (i, 0)),
                     pl.BlockSpec((tq, 128), lambda b, h, i: (i, 0)),
                     pl.BlockSpec((seq, 128), lambda b, h, i: (0, 0)),
                     pl.BlockSpec((seq, 128), lambda b, h, i: (0, 0))]
        args += [ck, cv, cos, sin, cos, sin]
        scratch = [pltpu.VMEM((seq, 128), F32)]
    out_specs = [pl.BlockSpec((tq, 128), lambda b, h, i: (b * nq + i, h))]
    out_shape = [jax.ShapeDtypeStruct((n_seq * seq, H_DIFF * DV_DIFF), BF16)]
    if not latent:
        out_specs += [pl.BlockSpec((None, None, 2, seq, DH_DIFF), lambda b, h, i: (b, h, 0, 0, 0)),
                      pl.BlockSpec((None, None, seq, DV_DIFF), lambda b, h, i: (b, h, 0, 0))]
        out_shape += [jax.ShapeDtypeStruct((n_seq, H_DIFF, 2, seq, DH_DIFF), F32),
                      jax.ShapeDtypeStruct((n_seq, H_DIFF, seq, DV_DIFF), F32)]
    return pl.pallas_call(
        functools.partial(_diff_body, lam_init=lam_init, latent=latent),
        grid=(n_seq, H_DIFF, nq),
        in_specs=in_specs,
        out_specs=out_specs,
        out_shape=out_shape,
        scratch_shapes=scratch,
        compiler_params=_cparams(3),
        name="diff_attention",
    )(*args)


SWA_TQ = 256
SWA_WIN = SWA_TQ + 2 * WINDOW
NEG = -1e30


def _swa_body(*refs, latent, seq):
    if latent:
        (q_ref, k_ref, v_ref, sink_ref, ck_ref, cv_ref, cq_ref, sq_ref, ckk_ref, skk_ref,
         o_ref, kr_ref) = refs
        qi = pl.program_id(1)

        @pl.when(qi == 0)
        def _():
            kr_ref[...] = _rope128(k_ref[...], ckk_ref[...], skk_ref[...])

        cq, sq = cq_ref[...], sq_ref[...]
        q = jnp.concatenate([_rope128(q_ref[:, g * 128:(g + 1) * 128], cq, sq)
                             for g in range(H_SWA * DH_SWA // 128)], axis=1)
        ws = pl.multiple_of(jnp.clip(qi * SWA_TQ - WINDOW, 0, seq - SWA_WIN), 128)
        kk = kr_ref[pl.ds(ws, SWA_WIN), :]
        vv = v_ref[pl.ds(ws, SWA_WIN), :]
        qpos = qi * SWA_TQ + lax.broadcasted_iota(jnp.int32, (SWA_TQ, SWA_WIN), 0)
        kpos = ws + lax.broadcasted_iota(jnp.int32, (SWA_TQ, SWA_WIN), 1)
        allowed = jnp.abs(qpos - kpos) <= WINDOW
    else:
        q_ref, k_ref, v_ref, sink_ref, o_ref, kc_ref, vc_ref = refs
        q = q_ref[...]
        kk = k_ref[...]
        vv = v_ref[...]
        for kv in range(KV_SWA):
            kc_ref[kv] = kk[:, kv * DH_SWA:(kv + 1) * DH_SWA]
            vc_ref[kv] = vv[:, kv * DH_SWA:(kv + 1) * DH_SWA]
    q = q * (DH_SWA ** -0.5)
    outs = []
    for kv in range(KV_SWA):
        lanes = slice(kv * DH_SWA, (kv + 1) * DH_SWA)
        kh = kk[:, lanes].astype(BF16)
        vh = vv[:, lanes].astype(BF16)
        for g in range(SWA_GROUP):
            h = kv * SWA_GROUP + g
            qh = q[:, h * DH_SWA:(h + 1) * DH_SWA].astype(BF16)
            sink = sink_ref[h][:, 0:1]
            s_loc = _dot_nt(qh, kh)
            if latent:
                s_loc = jnp.where(allowed, s_loc, NEG)
                s_ctx = _dot_nt(qh, ck_ref[kv].astype(BF16))
                m = jnp.maximum(s_loc.max(axis=-1, keepdims=True), s_ctx.max(axis=-1, keepdims=True))
            else:
                m = s_loc.max(axis=-1, keepdims=True)
            m = jnp.maximum(m, sink)
            e_loc = jnp.exp(s_loc - m)
            den = e_loc.sum(axis=-1, keepdims=True) + jnp.exp(sink - m)
            if latent:
                e_ctx = jnp.exp(s_ctx - m)
                den = den + e_ctx.sum(axis=-1, keepdims=True)
            inv = 1.0 / den
            o = _dot((e_loc * inv).astype(BF16), vh)
            if latent:
                o = o + _dot((e_ctx * inv).astype(BF16), cv_ref[kv].astype(BF16))
            outs.append(o)
    o_ref[...] = jnp.concatenate(outs, axis=1).astype(BF16)


def _swa_attention(proj, sink, *, n_seq, seq, row0, cache=None, rope=None):
    latent = cache is not None
    tq = SWA_TQ
    nq = seq // tq
    rb_q = row0 // tq
    rb_k = row0 // seq
    hq = H_SWA * DH_SWA
    sink_b = jnp.broadcast_to(sink[:, None, None], (H_SWA, 1, 128))
    in_specs = [pl.BlockSpec((tq, hq), lambda b, i: (rb_q + b * nq + i, 0)),
                pl.BlockSpec((seq, 128), lambda b, i: (rb_k + b, O_SK // 128)),
                pl.BlockSpec((seq, 128), lambda b, i: (rb_k + b, O_SV // 128)),
                pl.BlockSpec((H_SWA, 1, 128), lambda b, i: (0, 0, 0))]
    args = [proj, proj, proj, sink_b]
    scratch = []
    if latent:
        ck, cv = cache
        cos, sin = rope
        in_specs += [pl.BlockSpec((None, KV_SWA, PAST_LEN, DH_SWA), lambda b, i: (b, 0, 0, 0)),
                     pl.BlockSpec((None, KV_SWA, PAST_LEN, DH_SWA), lambda b, i: (b, 0, 0, 0)),
                     pl.BlockSpec((tq, 128), lambda b, i: (i, 0)),
                     pl.BlockSpec((tq, 128), lambda b, i: (i, 0)),
                     pl.BlockSpec((seq, 128), lambda b, i: (0, 0)),
                     pl.BlockSpec((seq, 128), lambda b, i: (0, 0))]
        args += [ck, cv, cos, sin, cos, sin]
        scratch = [pltpu.VMEM((seq, 128), F32)]
    out_specs = [pl.BlockSpec((tq, hq), lambda b, i: (b * nq + i, 0))]
    out_shape = [jax.ShapeDtypeStruct((n_seq * seq, hq), BF16)]
    if not latent:
        cache_spec = pl.BlockSpec((None, KV_SWA, seq, DH_SWA), lambda b, i: (b, 0, 0, 0))
        cache_shape = jax.ShapeDtypeStruct((n_seq, KV_SWA, seq, DH_SWA), F32)
        out_specs += [cache_spec, cache_spec]
        out_shape += [cache_shape, cache_shape]
    return pl.pallas_call(
        functools.partial(_swa_body, latent=latent, seq=seq),
        grid=(n_seq, nq),
        in_specs=in_specs,
        out_specs=out_specs,
        out_shape=out_shape,
        scratch_shapes=scratch,
        compiler_params=_cparams(2),
        name="swa_attention",
    )(*args)


GLA_BLOCK = 128
GLA_LEVELS = 7


def _split3(x):
    hi = x.astype(BF16)
    r1 = x - hi.astype(F32)
    mid = r1.astype(BF16)
    lo = (r1 - mid.astype(F32)).astype(BF16)
    return hi, mid, lo


def _gla_body(q_ref, k_ref, v_ref, r_ref, w2_ref, b2_ref, s0_ref, o_ref, sfin_ref, st_ref, *, n_blocks):
    d = pl.program_id(0)
    n = pl.program_id(2)
    T = GLA_BLOCK
    hk = H_GLA * DK_GLA

    @pl.when(n == 0)
    def _():
        st_ref[...] = s0_ref[...]

    q = q_ref[...] * (DK_GLA ** -0.5)
    k = k_ref[...]
    v = v_ref[...]
    rs = _split3(r_ref[...])
    ws = _split3(w2_ref[...])
    z = b2_ref[...]
    for a_i in range(3):
        for b_i in range(3 - a_i):
            z = z + _dot(rs[a_i], ws[b_i])
    la = (jnp.minimum(z, 0.0) - jnp.log(1.0 + jnp.exp(-jnp.abs(z)))) * (1.0 / GLA_TAU)

    t_idx = lax.broadcasted_iota(jnp.int32, (T, hk), 0)
    xr = lax.broadcasted_iota(jnp.int32, (T, T), 0) ^ lax.broadcasted_iota(jnp.int32, (T, T), 1)
    heads = [slice(h * DK_GLA, (h + 1) * DK_GLA) for h in range(H_GLA)]
    qb, kb = q.astype(BF16), k.astype(BF16)
    att = [_dot_nt(qb[:, hs], kb[:, hs]) for hs in heads]
    x = la
    tot = la
    for lvl in range(GLA_LEVELS):
        sz = 1 << lvl
        upper = (t_idx >> lvl) & 1
        is_q = (upper ^ d) == 1
        partner = jnp.where(upper == 1, pltpu.roll(tot, sz, 0), pltpu.roll(tot, T - sz, 0))
        e = jnp.exp(jnp.where(is_q, x, tot - x))
        qs = jnp.where(is_q, q * e, 0.0).astype(BF16)
        ks = jnp.where(is_q, 0.0, k * e).astype(BF16)
        for h, hs in enumerate(heads):
            att[h] = jnp.where(xr >= sz, _dot_nt(qs[:, hs], ks[:, hs]), att[h])
        x = x + jnp.where(is_q, partner, 0.0)
        tot = tot + partner

    qe = (q * jnp.exp(x)).astype(BF16)
    kw = (k * jnp.exp(tot - x)).astype(BF16)
    vt = v.T
    vb = v.astype(BF16)
    dec = jnp.exp(tot[0:1, :])
    for h, hs in enumerate(heads):
        vs = slice(h * DV_GLA, (h + 1) * DV_GLA)
        st = st_ref[h]
        o = _dot(att[h].astype(BF16), vb[:, vs]) + _dot_nt(qe[:, hs], st.astype(BF16))
        o_ref[:, vs] = o
        st_ref[h] = st * dec[:, hs] + _dot(vt[vs, :].astype(BF16), kw[:, hs])

    @pl.when(n == n_blocks - 1)
    def _():
        sfin_ref[...] = st_ref[...]


def _gla_finish_body(of_ref, ob_ref, g_ref, o_ref):
    o = of_ref[...] + ob_ref[...]
    for h in range(H_GLA):
        vs = slice(h * DV_GLA, (h + 1) * DV_GLA)
        o_ref[:, vs] = (_norm_rows(o[:, vs]) * _silu(g_ref[:, vs])).astype(BF16)


def _gla(proj, w2, b2, s0t, *, n_seq, seq, row0):
    nb = seq // GLA_BLOCK
    rb = row0 // GLA_BLOCK
    hk, hv = H_GLA * DK_GLA, H_GLA * DV_GLA
    w2p = jnp.zeros((2, 128, hk), F32)
    w2p = w2p.at[0, 0:GLA_RANK].set(w2[0]).at[1, GLA_RANK:2 * GLA_RANK].set(w2[1])

    def blk(d, b, n):
        return rb + b * nb + n + d * (nb - 1 - 2 * n)

    def oblk(d, b, n):
        return b * nb + n + d * (nb - 1 - 2 * n)

    o2, sfin = pl.pallas_call(
        functools.partial(_gla_body, n_blocks=nb),
        grid=(2, n_seq, nb),
        in_specs=[pl.BlockSpec((GLA_BLOCK, hk), lambda d, b, n: (blk(d, b, n), O_GQ // hk)),
                  pl.BlockSpec((GLA_BLOCK, hk), lambda d, b, n: (blk(d, b, n), O_GK // hk)),
                  pl.BlockSpec((GLA_BLOCK, hv), lambda d, b, n: (blk(d, b, n), O_GV // hv)),
                  pl.BlockSpec((GLA_BLOCK, 128), lambda d, b, n: (blk(d, b, n), O_GLR // 128)),
                  pl.BlockSpec((None, 128, hk), lambda d, b, n: (d, 0, 0)),
                  pl.BlockSpec((None, 1, hk), lambda d, b, n: (d, 0, 0)),
                  pl.BlockSpec((None, None, H_GLA, DV_GLA, DK_GLA), lambda d, b, n: (d, b, 0, 0, 0))],
        out_specs=[pl.BlockSpec((None, GLA_BLOCK, hv), lambda d, b, n: (d, oblk(d, b, n), 0)),
                   pl.BlockSpec((None, None, H_GLA, DV_GLA, DK_GLA), lambda d, b, n: (d, b, 0, 0, 0))],
        out_shape=[jax.ShapeDtypeStruct((2, n_seq * seq, hv), F32),
                   jax.ShapeDtypeStruct((2, n_seq, H_GLA, DV_GLA, DK_GLA), F32)],
        scratch_shapes=[pltpu.VMEM((H_GLA, DV_GLA, DK_GLA), F32)],
        compiler_params=_cparams(3),
        name="gla",
    )(proj, proj, proj, proj, w2p, b2.reshape(2, 1, hk), s0t)

    rt = ROW_TILE
    rbt = row0 // rt
    out = pl.pallas_call(
        _gla_finish_body,
        grid=(n_seq * seq // rt,),
        in_specs=[pl.BlockSpec((None, rt, hv), lambda i: (0, i, 0)),
                  pl.BlockSpec((None, rt, hv), lambda i: (1, i, 0)),
                  pl.BlockSpec((rt, hv), lambda i: (rbt + i, O_GR // hv))],
        out_specs=pl.BlockSpec((rt, hv), lambda i: (i, 0)),
        out_shape=jax.ShapeDtypeStruct((n_seq * seq, hv), BF16),
        compiler_params=_cparams(1),
        name="gla_finish",
    )(o2, o2, proj)
    return out, sfin


def kernel(x_prompt, x_sample, c, state_ret, cache_diff_k, cache_diff_v, cache_swa_k, cache_swa_v, state_gla,
           c_ctx, w_mod, b_mod, ln_g, ln_b, w_in_even, w_out_even, ret_decay, diff_lam, w_in_odd, w_out_odd,
           swa_sink, gla_w2, gla_b, w_ff1, w_ff2):
    xc = x_prompt.reshape(N_CTX, D_MODEL)
    xl = x_sample.reshape(N_LAT, D_MODEL)
    cond =jnp.concatenate([c_ctx[None, :], c, jnp.zeros((N_COND - 1 - DEC_BATCH, D_MODEL), F32)], axis=0)
    mod = _modulation(cond, w_mod, b_mod).reshape(DEPTH, N_COND, 1, 6 * D_MODEL)
    rope = _rope_tables(DEC_SEQ // GRID_W, DH_DIFF)

    w_in_even_b = w_in_even.astype(BF16)
    w_in_odd_b = _reorder_odd(w_in_odd).astype(BF16)
    w_out_even_b = w_out_even.astype(BF16)
    w_out_odd_b = w_out_odd.astype(BF16)
    w_ff1_b = w_ff1.astype(BF16)
    w_ff2_b = w_ff2.astype(BF16)

    ctx = dict(n_seq=BATCH, seq=SEQ, row0=0)
    lat = dict(n_seq=DEC_BATCH, seq=DEC_SEQ, row0=0)
    new_ret, new_dk, new_dv, new_sk, new_sv, new_gla = [], [], [], [], [], []
    for l in range(DEPTH):
        if l % 2 == 0:
            e = l // 2
            lam_init = 0.8 - 0.6 * math.exp(-0.3 * l)
            pc = _in_projection(xc, mod[l], w_in_even_b[e], False)
            pl_ = _in_projection(xl, mod[l], w_in_even_b[e], True)
            zero = jnp.zeros((BATCH, 2, H_RET, DK_RET, DV_RET), F32)
            a_ctx, s_ret = _retention(pc, ret_decay[e], zero, **ctx)
            a_lat, _ = _retention(pl_, ret_decay[e], state_ret[:, e], **lat)
            b_ctx, k_new, v_new = _diff_attention(pc, diff_lam[e], lam_init, **ctx)
            b_lat, = _diff_attention(pl_, diff_lam[e], lam_init, cache=(cache_diff_k[:, e], cache_diff_v[:, e]),
                                     rope=rope, **lat)
            new_ret.append(s_ret)
            new_dk.append(k_new)
            new_dv.append(v_new)
            w_out = w_out_even_b[e]
        else:
            o = l // 2
            pc = _in_projection(xc, mod[l], w_in_odd_b[o], False)
            pl_ = _in_projection(xl, mod[l], w_in_odd_b[o], True)
            a_ctx, k_new, v_new = _swa_attention(pc, swa_sink[o], **ctx)
            a_lat, = _swa_attention(pl_, swa_sink[o], cache=(cache_swa_k[:, o], cache_swa_v[:, o]), rope=rope, **lat)
            zero = jnp.zeros((2, BATCH, H_GLA, DV_GLA, DK_GLA), F32)
            b_ctx, s_gla = _gla(pc, gla_w2[o], gla_b[o], zero, **ctx)
            s0t = state_gla[:, o].transpose(1, 0, 2, 4, 3)
            b_lat, _ = _gla(pl_, gla_w2[o], gla_b[o], s0t, **lat)
            new_sk.append(k_new)
            new_sv.append(v_new)
            new_gla.append(s_gla.transpose(1, 0, 2, 4, 3))
            w_out = w_out_odd_b[o]
        xc = _post_mixer(xc, a_ctx, b_ctx, mod[l], w_out, ln_g[l], ln_b[l], w_ff1_b[l], w_ff2_b[l], False)
        xl = _post_mixer(xl, a_lat, b_lat, mod[l], w_out, ln_g[l], ln_b[l], w_ff1_b[l], w_ff2_b[l], True)
    return (xc.reshape(BATCH, SEQ, D_MODEL), xl.reshape(DEC_BATCH, DEC_SEQ, D_MODEL),
            jnp.stack(new_ret, axis=1), jnp.stack(new_dk, axis=1), jnp.stack(new_dv, axis=1),
            jnp.stack(new_sk, axis=1), jnp.stack(new_sv, axis=1), jnp.stack(new_gla, axis=1))
```

```python
import functools
import math

import numpy as np
import jax
import jax.numpy as jnp
from jax import lax
from jax.experimental import pallas as pl
from jax.experimental.pallas import tpu as pltpu

F32 = jnp.float32
BF16 = jnp.bfloat16

D_MODEL = 1024
BATCH = 16
SEQ = 256
DEPTH = 4
DEC_BATCH = 4
DEC_SEQ = 2048
PAST_LEN = 256
GRID_W = 64
H_RET, DK_RET, DV_RET = 4, 64, 128
H_DIFF, DH_DIFF, DV_DIFF = 4, 64, 128
H_SWA, KV_SWA, DH_SWA = 8, 2, 64
SWA_GROUP = H_SWA // KV_SWA
WINDOW = 128
H_GLA, DK_GLA, DV_GLA = 4, 64, 128
GLA_RANK = 16
GLA_TAU = 16.0
D_FF = 4 * D_MODEL
ROPE_BASE = 10000.0
N_EVEN = (DEPTH + 1) // 2
N_ODD = DEPTH // 2
ALPHA = (2 * DEPTH) ** 0.25
EVEN_IN = 3072
ODD_IN = 2336
ODD_IN_PAD = 2432
EPS = 1e-5
LOG2E = 1.4426950408889634

E_RQ, E_RK, E_RV, E_RG, E_DQ, E_DK, E_DV = 0, 256, 512, 1024, 1536, 2048, 2560
O_SQ, O_GV, O_GR, O_GQ, O_GK, O_SK, O_SV, O_GLR = 0, 512, 1024, 1536, 1792, 2048, 2176, 2304


def _reorder_odd(w):
    sq, sk, sv, gq, gk, gv, gr, glr = jnp.split(w, [512, 640, 768, 1024, 1280, 1792, 2304], axis=-1)
    pad = jnp.zeros(w.shape[:-1] + (ODD_IN_PAD - ODD_IN,), w.dtype)
    return jnp.concatenate([sq, gv, gr, gq, gk, sk, sv, glr, pad], axis=-1)


N_CTX = BATCH * SEQ
N_LAT = DEC_BATCH * DEC_SEQ
ROW_TILE = 256
DENSE_TILE = 512
N_COND = 8

VMEM_LIMIT = 56 * 1024 * 1024


def _cparams(n_axes):
    return pltpu.CompilerParams(dimension_semantics=("arbitrary",) * n_axes,
                                vmem_limit_bytes=VMEM_LIMIT)


def _dot(a, b):
    return jnp.dot(a, b, preferred_element_type=F32)


def _dot_nt(a, b):
    return lax.dot_general(a, b, (((1,), (1,)), ((), ())), preferred_element_type=F32)


def _silu(x):
    return x * (1.0 / (1.0 + jnp.exp(-x)))


def _norm_rows(x):
    mu = jnp.mean(x, axis=-1, keepdims=True)
    xc = x - mu
    var = jnp.mean(xc * xc, axis=-1, keepdims=True)
    return xc * lax.rsqrt(var + EPS)


def _mod_row(i, latent):
    return 1 + i // (DEC_SEQ // DENSE_TILE) if latent else 0


def _resident(shape, index_map):
    return pl.BlockSpec(shape, index_map, pipeline_mode=pl.Buffered(1))


def _mod_body(c_ref, w_ref, b_ref, o_ref):
    c = _silu(c_ref[...]).astype(BF16)
    o_ref[...] = _dot(c, w_ref[...].astype(BF16)) + b_ref[...]


def _modulation(cond, w_mod, b_mod):
    tn = 1536
    return pl.pallas_call(
        _mod_body,
        grid=(DEPTH, 6 * D_MODEL // tn),
        in_specs=[pl.BlockSpec((N_COND, D_MODEL), lambda l, j: (0, 0)),
                  pl.BlockSpec((None, D_MODEL, tn), lambda l, j: (l, 0, j)),
                  pl.BlockSpec((None, 1, tn), lambda l, j: (l, 0, j))],
        out_specs=pl.BlockSpec((None, N_COND, tn), lambda l, j: (l, 0, j)),
        out_shape=jax.ShapeDtypeStruct((DEPTH, N_COND, 6 * D_MODEL), F32),
        compiler_params=_cparams(2),
        name="modulation",
    )(cond, w_mod, b_mod.reshape(DEPTH, 1, 6 * D_MODEL))


def _inproj_body(x_ref, mod_ref, w_ref, o_ref):
    sh = mod_ref[:, 0:D_MODEL]
    sc = mod_ref[:, D_MODEL:2 * D_MODEL]
    h = (x_ref[...] * (1.0 + sc) + sh).astype(BF16)
    o_ref[...] = _dot(h, w_ref[...])


def _in_projection(x, mod_l, w_stack, idx, latent):
    n_in = w_stack.shape[2]
    n_rows = x.shape[0]
    tm = DENSE_TILE
    return pl.pallas_call(
        _inproj_body,
        grid=(n_rows // tm,),
        in_specs=[pl.BlockSpec((tm, D_MODEL), lambda i: (i, 0)),
                  pl.BlockSpec((None, 1, 6 * D_MODEL), lambda i: (_mod_row(i, latent), 0, 0)),
                  _resident((None, D_MODEL, n_in), lambda i: (idx, 0, 0))],
        out_specs=pl.BlockSpec((tm, n_in), lambda i: (i, 0)),
        out_shape=jax.ShapeDtypeStruct((n_rows, n_in), F32),
        compiler_params=_cparams(1),
        name="in_projection",
    )(x, mod_l, w_stack)


def _layer_norm(x, g, b):
    return _norm_rows(x) * g + b


def _post_body(x_ref, ma_ref, mb_ref, mod_ref, wo_ref, g_ref, b_ref, w1_ref, w2_ref, o_ref):
    half = D_MODEL // 2
    gt1 = mod_ref[:, 2 * D_MODEL:3 * D_MODEL]
    sh2 = mod_ref[:, 3 * D_MODEL:4 * D_MODEL]
    sc2 = mod_ref[:, 4 * D_MODEL:5 * D_MODEL]
    gt2 = mod_ref[:, 5 * D_MODEL:6 * D_MODEL]
    mix = _dot(ma_ref[...], wo_ref[0:half, :]) + _dot(mb_ref[...], wo_ref[half:D_MODEL, :])
    x1 = _layer_norm(ALPHA * x_ref[...] + gt1 * mix, g_ref[0:1, :], b_ref[0:1, :])
    hf = (x1 * (1.0 + sc2) + sh2).astype(BF16)
    ff = jnp.zeros((DENSE_TILE, D_MODEL), F32)
    chunk = 1024
    for j in range(D_FF // chunk):
        h1 = _dot(hf, w1_ref[:, j * chunk:(j + 1) * chunk])
        h1 = jnp.square(jnp.maximum(h1, 0.0)).astype(BF16)
        ff = ff + _dot(h1, w2_ref[j * chunk:(j + 1) * chunk, :])
    o_ref[...] = _layer_norm(ALPHA * x1 + gt2 * ff, g_ref[1:2, :], b_ref[1:2, :])


def _post_mixer(x, mix_a, mix_b, mod_l, w_out_stack, idx, ln_g, ln_b, w1_stack, w2_stack, layer, latent):
    half = D_MODEL // 2
    n_rows = x.shape[0]
    tm = DENSE_TILE
    return pl.pallas_call(
        _post_body,
        grid=(n_rows // tm,),
        in_specs=[pl.BlockSpec((tm, D_MODEL), lambda i: (i, 0)),
                  pl.BlockSpec((tm, half), lambda i: (i, 0)),
                  pl.BlockSpec((tm, half), lambda i: (i, 0)),
                  pl.BlockSpec((None, 1, 6 * D_MODEL), lambda i: (_mod_row(i, latent), 0, 0)),
                  _resident((None, D_MODEL, D_MODEL), lambda i: (idx, 0, 0)),
                  _resident((None, 2, D_MODEL), lambda i: (layer, 0, 0)),
                  _resident((None, 2, D_MODEL), lambda i: (layer, 0, 0)),
                  _resident((None, D_MODEL, D_FF), lambda i: (layer, 0, 0)),
                  _resident((None, D_FF, D_MODEL), lambda i: (layer, 0, 0))],
        out_specs=pl.BlockSpec((tm, D_MODEL), lambda i: (i, 0)),
        out_shape=jax.ShapeDtypeStruct((n_rows, D_MODEL), F32),
        compiler_params=_cparams(1),
        name="post_mixer",
    )(x, mix_a, mix_b, mod_l, w_out_stack, ln_g, ln_b, w1_stack, w2_stack)


def _rope_tables(rows, dim):
    row = jnp.repeat(jnp.arange(rows), GRID_W).astype(F32)
    col = jnp.tile(jnp.arange(GRID_W), rows).astype(F32)
    half = dim // 2
    freqs = ROPE_BASE ** (-jnp.arange(0, half, 2, dtype=F32) / half)
    ar = row[:, None] * freqs
    ac = col[:, None] * freqs
    cr, sr, cc, sc = jnp.cos(ar), jnp.sin(ar), jnp.cos(ac), jnp.sin(ac)
    cos = jnp.concatenate([cr, cr, cc, cc], axis=-1)
    sin = jnp.concatenate([-sr, sr, -sc, sc], axis=-1)
    reps = 128 // dim
    return jnp.tile(cos, (1, reps)), jnp.tile(sin, (1, reps))


def _rope128(x, cos, sin):
    lane = lax.broadcasted_iota(jnp.int32, x.shape, 1)
    first = (lane & 31) < 16
    swapped = jnp.where(first, pltpu.roll(x, 112, 1), pltpu.roll(x, 16, 1))
    return x * cos + swapped * sin


RET_CHUNK = 256


def _ret_body(q_ref, k_ref, v_ref, g_ref, rdk_ref, rdh_ref, s0_ref, o_ref, sfin_ref, *, seq):
    C = RET_CHUNK
    n_chunks = seq // C
    scale = DK_RET ** -0.5
    ii = lax.broadcasted_iota(jnp.int32, (C, C), 0)
    jj = lax.broadcasted_iota(jnp.int32, (C, C), 1)
    dist = (ii - jj).astype(F32)
    adist = jnp.abs(dist)
    pos = lax.broadcasted_iota(jnp.int32, (C, H_RET * DK_RET), 0).astype(F32)
    lgf = -jnp.exp(rdk_ref[0])
    lgb = -jnp.exp(rdk_ref[1])
    k_wf = jnp.exp(lgf * (C - 1.0 - pos)) * scale
    k_wb = jnp.exp(lgb * pos) * scale
    q_wf = jnp.exp(lgf * (pos + 1.0))
    q_wb = jnp.exp(lgb * (C - pos))

    uf, ub = [], []
    for n in range(n_chunks):
        kn = k_ref[n * C:(n + 1) * C, :]
        kft = (kn * k_wf).T.astype(BF16)
        kbt = (kn * k_wb).T.astype(BF16)
        ufn, ubn = [], []
        for h in range(H_RET):
            vh = v_ref[n * C:(n + 1) * C, h * DV_RET:(h + 1) * DV_RET].astype(BF16)
            ufn.append(_dot(kft[h * DK_RET:(h + 1) * DK_RET, :], vh))
            ubn.append(_dot(kbt[h * DK_RET:(h + 1) * DK_RET, :], vh))
        uf.append(ufn)
        ub.append(ubn)

    for h in range(H_RET):
        lgf_h = -jnp.exp(rdh_ref[0, h])
        lgb_h = -jnp.exp(rdh_ref[1, h])
        cf = jnp.exp(lgf_h * float(C))[:, 0:DV_RET]
        cb = jnp.exp(lgb_h * float(C))[:, 0:DV_RET]
        dmat = jnp.exp(jnp.where(dist > 0, lgf_h, lgb_h) * adist) + jnp.where(dist == 0, 1.0, 0.0)
        sf = [s0_ref[0, h]]
        for n in range(n_chunks):
            sf.append(cf * sf[n] + uf[n][h])
        sb = [None] * n_chunks
        sb[n_chunks - 1] = s0_ref[1, h]
        for n in range(n_chunks - 1, 0, -1):
            sb[n - 1] = cb * sb[n] + ub[n][h]
        sfin_ref[0, h] = sf[n_chunks]
        sfin_ref[1, h] = cb * sb[0] + ub[0][h]
        lanes = slice(h * DK_RET, (h + 1) * DK_RET)
        for n in range(n_chunks):
            rows = slice(n * C, (n + 1) * C)
            qh = q_ref[rows, lanes]
            kh = (k_ref[rows, lanes] * scale).astype(BF16)
            vh = v_ref[rows, h * DV_RET:(h + 1) * DV_RET].astype(BF16)
            s = _dot_nt(qh.astype(BF16), kh) * dmat
            o = _dot(s.astype(BF16), vh)
            o = o + _dot((qh * q_wf[:, lanes]).astype(BF16), sf[n].astype(BF16))
            o = o + _dot((qh * q_wb[:, lanes]).astype(BF16), sb[n].astype(BF16))
            gate = _silu(g_ref[rows, h * DV_RET:(h + 1) * DV_RET])
            o_ref[rows, h * DV_RET:(h + 1) * DV_RET] = (_norm_rows(o) * gate).astype(BF16)


def _retention(proj, ret_decay_e, s0, *, n_seq, seq):
    rdk = jnp.repeat(ret_decay_e, DK_RET, axis=-1).reshape(2, 1, H_RET * DK_RET)
    rdh = jnp.broadcast_to(ret_decay_e[:, :, None, None], (2, H_RET, 1, RET_CHUNK))
    hk, hv = H_RET * DK_RET, H_RET * DV_RET
    return pl.pallas_call(
        functools.partial(_ret_body, seq=seq),
        grid=(n_seq,),
        in_specs=[pl.BlockSpec((seq, hk), lambda b: (b, E_RQ // hk)),
                  pl.BlockSpec((seq, hk), lambda b: (b, E_RK // hk)),
                  pl.BlockSpec((seq, hv), lambda b: (b, E_RV // hv)),
                  pl.BlockSpec((seq, hv), lambda b: (b, E_RG // hv)),
                  pl.BlockSpec((2, 1, hk), lambda b: (0, 0, 0)),
                  pl.BlockSpec((2, H_RET, 1, RET_CHUNK), lambda b: (0, 0, 0, 0)),
                  pl.BlockSpec((None, 2, H_RET, DK_RET, DV_RET), lambda b: (b, 0, 0, 0, 0))],
        out_specs=[pl.BlockSpec((seq, hv), lambda b: (b, 0)),
                   pl.BlockSpec((None, 2, H_RET, DK_RET, DV_RET), lambda b: (b, 0, 0, 0, 0))],
        out_shape=[jax.ShapeDtypeStruct((n_seq * seq, hv), BF16),
                   jax.ShapeDtypeStruct((n_seq, 2, H_RET, DK_RET, DV_RET), F32)],
        compiler_params=_cparams(1),
        name="retention",
    )(proj, proj, proj, proj, rdk, rdh, s0)


def _col_softmax(parts, extra=None):
    m = parts[0].max(axis=0, keepdims=True)
    for p in parts[1:]:
        m = jnp.maximum(m, p.max(axis=0, keepdims=True))
    if extra is not None:
        m = jnp.maximum(m, extra)
    es = [jnp.exp2(p - m) for p in parts]
    den = es[0].sum(axis=0, keepdims=True)
    for e in es[1:]:
        den = den + e.sum(axis=0, keepdims=True)
    if extra is not None:
        den = den + jnp.exp2(extra - m)
    return es, 1.0 / den


def _row_softmax_pv(qb, keys, vals):
    parts = [_dot_nt(qb, k) for k in keys]
    m = parts[0].max(axis=-1, keepdims=True)
    for p in parts[1:]:
        m = jnp.maximum(m, p.max(axis=-1, keepdims=True))
    acc = den = None
    for p, v in zip(parts, vals):
        e = jnp.exp2(p - m)
        d = e.sum(axis=-1, keepdims=True)
        t = _dot(e.astype(BF16), v)
        den = d if den is None else den + d
        acc = t if acc is None else acc + t
    return acc, 1.0 / den


DIFF_TQ = 256


def _diff_lambda(p, lam_init):
    a = jnp.sum(p[0:1, :] * p[1:2, :], axis=-1, keepdims=True)
    b = jnp.sum(p[2:3, :] * p[3:4, :], axis=-1, keepdims=True)
    return jnp.exp(a) - jnp.exp(b) + lam_init


def _split_maps(k):
    lane = lax.broadcasted_iota(jnp.int32, k.shape, 1)
    return (jnp.where(lane < DH_DIFF, k, 0.0).astype(BF16), jnp.where(lane >= DH_DIFF, k, 0.0).astype(BF16))


def _diff_body(*refs, lam_init, latent):
    if latent:
        (lam_ref, q_ref, k_ref, v_ref, ck_ref, cv_ref, cq_ref, sq_ref, ckk_ref, skk_ref,
         o_ref, km_ref, vb_ref, ckm_ref, cvb_ref) = refs
        qi = pl.program_id(2)

        @pl.when(qi == 0)
        def _():
            k1, k2 = _split_maps(_rope128(k_ref[...], ckk_ref[...], skk_ref[...]))
            km_ref[0] = k1
            km_ref[1] = k2
            vb_ref[...] = v_ref[...].astype(BF16)
            c1, c2 = _split_maps(jnp.concatenate([ck_ref[0], ck_ref[1]], axis=1))
            ckm_ref[0] = c1
            ckm_ref[1] = c2
            cvb_ref[...] = cv_ref[...].astype(BF16)

        q = _rope128(q_ref[...], cq_ref[...], sq_ref[...])
        keys = lambda m: [ckm_ref[m], km_ref[m]]
        vals = [cvb_ref[...], vb_ref[...]]
    else:
        lam_ref, q_ref, k_ref, v_ref, o_ref, kc_ref, vc_ref = refs
        q = q_ref[...]
        kk = k_ref[...]
        vv = v_ref[...]
        kc_ref[0] = kk[:, 0:DH_DIFF]
        kc_ref[1] = kk[:, DH_DIFF:2 * DH_DIFF]
        vc_ref[...] = vv
        kms = _split_maps(kk)
        keys = lambda m: [kms[m]]
        vals = [vv.astype(BF16)]
    lam = _diff_lambda(lam_ref[...], lam_init)
    qb = (q * (DH_DIFF ** -0.5 * LOG2E)).astype(BF16)
    acc1, inv1 = _row_softmax_pv(qb, keys(0), vals)
    acc2, inv2 = _row_softmax_pv(qb, keys(1), vals)
    o = acc1 * inv1 - acc2 * (inv2 * lam)
    o_ref[...] = (_norm_rows(o) * (1.0 - lam_init)).astype(BF16)


def _diff_attention(proj, lam_p, lam_init, *, n_seq, seq, cache=None, rope=None):
    latent = cache is not None
    tq = DIFF_TQ
    nq = seq // tq
    q_col, k_col, v_col = E_DQ // 128, E_DK // 128, E_DV // 128
    in_specs = [pl.BlockSpec((4, DH_DIFF), lambda b, h, i: (0, 0)),
                pl.BlockSpec((tq, 128), lambda b, h, i: (b * nq + i, q_col + h)),
                pl.BlockSpec((seq, 128), lambda b, h, i: (b, k_col + h)),
                pl.BlockSpec((seq, 128), lambda b, h, i: (b, v_col + h))]
    args = [lam_p, proj, proj, proj]
    scratch = []
    out_specs = [pl.BlockSpec((tq, 128), lambda b, h, i: (b * nq + i, h))]
    out_shape = [jax.ShapeDtypeStruct((n_seq * seq, H_DIFF * DV_DIFF), BF16)]
    if latent:
        ck, cv = cache
        cos, sin = rope
        in_specs += [pl.BlockSpec((None, None, 2, PAST_LEN, DH_DIFF), lambda b, h, i: (b, h, 0, 0, 0)),
                     pl.BlockSpec((None, None, PAST_LEN, DV_DIFF), lambda b, h, i: (b, h, 0, 0)),
                     pl.BlockSpec((tq, 128), lambda b, h, i: (i, 0)),
                     pl.BlockSpec((tq, 128), lambda b, h, i: (i, 0)),
                     pl.BlockSpec((seq, 128), lambda b, h, i: (0, 0)),
                     pl.BlockSpec((seq, 128), lambda b, h, i: (0, 0))]
        args += [ck, cv, cos, sin, cos, sin]
        scratch = [pltpu.VMEM((2, seq, 128), BF16), pltpu.VMEM((seq, DV_DIFF), BF16),
                   pltpu.VMEM((2, PAST_LEN, 128), BF16), pltpu.VMEM((PAST_LEN, DV_DIFF), BF16)]
    else:
        out_specs += [pl.BlockSpec((None, None, 2, seq, DH_DIFF), lambda b, h, i: (b, h, 0, 0, 0)),
                      pl.BlockSpec((None, None, seq, DV_DIFF), lambda b, h, i: (b, h, 0, 0))]
        out_shape += [jax.ShapeDtypeStruct((n_seq, H_DIFF, 2, seq, DH_DIFF), F32),
                      jax.ShapeDtypeStruct((n_seq, H_DIFF, seq, DV_DIFF), F32)]
    return pl.pallas_call(
        functools.partial(_diff_body, lam_init=lam_init, latent=latent),
        grid=(n_seq, H_DIFF, nq),
        in_specs=in_specs,
        out_specs=out_specs,
        out_shape=out_shape,
        scratch_shapes=scratch,
        compiler_params=_cparams(3),
        name="diff_attention",
    )(*args)


SWA_TQ = 256
SWA_WIN = SWA_TQ + 2 * WINDOW
NEG = -1e30


def _swa_body(*refs, latent, seq):
    tq = SWA_TQ
    if latent:
        (q_ref, k_ref, v_ref, sink_ref, ck_ref, cv_ref, cq_ref, sq_ref, ckk_ref, skk_ref,
         o_ref, kr_ref, cvt_ref) = refs
        qi = pl.program_id(1)

        @pl.when(qi == 0)
        def _():
            kr_ref[...] = _rope128(k_ref[...], ckk_ref[...], skk_ref[...]).astype(BF16)
            cvt_ref[...] = jnp.concatenate([cv_ref[0], cv_ref[1]], axis=1).T.astype(BF16)

        cq, sq = cq_ref[...], sq_ref[...]
        q = jnp.concatenate([_rope128(q_ref[:, g * 128:(g + 1) * 128], cq, sq)
                             for g in range(H_SWA * DH_SWA // 128)], axis=1)
        ws = pl.multiple_of(jnp.clip(qi * tq - WINDOW, 0, seq - SWA_WIN), 128)
        kw = kr_ref[pl.ds(ws, SWA_WIN), :]
        vwt = v_ref[pl.ds(ws, SWA_WIN), :].T.astype(BF16)
        kpos = ws + lax.broadcasted_iota(jnp.int32, (SWA_WIN, tq), 0)
        qpos = qi * tq + lax.broadcasted_iota(jnp.int32, (SWA_WIN, tq), 1)
        bias1 = jnp.where(jnp.abs(qpos - kpos) <= WINDOW, 0.0, NEG)
        bias = jnp.concatenate([bias1] * SWA_GROUP, axis=1)
    else:
        q_ref, k_ref, v_ref, sink_ref, o_ref, kc_ref, vc_ref = refs
        q = q_ref[...]
        kk = k_ref[...]
        vv = v_ref[...]
        for kv in range(KV_SWA):
            kc_ref[kv] = kk[:, kv * DH_SWA:(kv + 1) * DH_SWA]
            vc_ref[kv] = vv[:, kv * DH_SWA:(kv + 1) * DH_SWA]
        kw = kk.astype(BF16)
        vwt = vv.T.astype(BF16)
    q = q * (DH_SWA ** -0.5 * LOG2E)
    pieces = []
    for kv in range(KV_SWA):
        lanes = slice(kv * DH_SWA, (kv + 1) * DH_SWA)
        q4 = jnp.concatenate([q[:, (kv * SWA_GROUP + g) * DH_SWA:(kv * SWA_GROUP + g + 1) * DH_SWA]
                              for g in range(SWA_GROUP)], axis=0).astype(BF16)
        sink = sink_ref[kv] * LOG2E
        s_loc = _dot_nt(kw[:, lanes], q4)
        if latent:
            parts = [_dot_nt(ck_ref[kv].astype(BF16), q4), s_loc + bias]
            vals_t = [cvt_ref[lanes, :], vwt[lanes, :]]
        else:
            parts = [s_loc]
            vals_t = [vwt[lanes, :]]
        es, inv = _col_softmax(parts, extra=sink)
        acc = None
        for e, vt in zip(es, vals_t):
            t = _dot(vt, e.astype(BF16))
            acc = t if acc is None else acc + t
        acc = acc * inv
        pieces += [acc[:, g * tq:(g + 1) * tq] for g in range(SWA_GROUP)]
    o_ref[...] = jnp.concatenate(pieces, axis=0).T.astype(BF16)


def _swa_attention(proj, sink, *, n_seq, seq, cache=None, rope=None):
    latent = cache is not None
    tq = SWA_TQ
    nq = seq // tq
    hq = H_SWA * DH_SWA
    sink_b = jnp.repeat(sink.reshape(KV_SWA, SWA_GROUP), tq, axis=1).reshape(KV_SWA, 1, SWA_GROUP * tq)
    in_specs = [pl.BlockSpec((tq, hq), lambda b, i: (b * nq + i, 0)),
                pl.BlockSpec((seq, 128), lambda b, i: (b, O_SK // 128)),
                pl.BlockSpec((seq, 128), lambda b, i: (b, O_SV // 128)),
                pl.BlockSpec((KV_SWA, 1, SWA_GROUP * tq), lambda b, i: (0, 0, 0))]
    args = [proj, proj, proj, sink_b]
    scratch = []
    out_specs = [pl.BlockSpec((tq, hq), lambda b, i: (b * nq + i, 0))]
    out_shape = [jax.ShapeDtypeStruct((n_seq * seq, hq), BF16)]
    if latent:
        ck, cv = cache
        cos, sin = rope
        in_specs += [pl.BlockSpec((None, KV_SWA, PAST_LEN, DH_SWA), lambda b, i: (b, 0, 0, 0)),
                     pl.BlockSpec((None, KV_SWA, PAST_LEN, DH_SWA), lambda b, i: (b, 0, 0, 0)),
                     pl.BlockSpec((tq, 128), lambda b, i: (i, 0)),
                     pl.BlockSpec((tq, 128), lambda b, i: (i, 0)),
                     pl.BlockSpec((seq, 128), lambda b, i: (0, 0)),
                     pl.BlockSpec((seq, 128), lambda b, i: (0, 0))]
        args += [ck, cv, cos, sin, cos, sin]
        scratch = [pltpu.VMEM((seq, 128), BF16), pltpu.VMEM((KV_SWA * DH_SWA, PAST_LEN), BF16)]
    else:
        cache_spec = pl.BlockSpec((None, KV_SWA, seq, DH_SWA), lambda b, i: (b, 0, 0, 0))
        cache_shape = jax.ShapeDtypeStruct((n_seq, KV_SWA, seq, DH_SWA), F32)
        out_specs += [cache_spec, cache_spec]
        out_shape += [cache_shape, cache_shape]
    return pl.pallas_call(
        functools.partial(_swa_body, latent=latent, seq=seq),
        grid=(n_seq, nq),
        in_specs=in_specs,
        out_specs=out_specs,
        out_shape=out_shape,
        scratch_shapes=scratch,
        compiler_params=_cparams(2),
        name="swa_attention",
    )(*args)


GLA_BLOCK = 128
GLA_LEVELS = 7
GLA_SAFE = 60.0
GLA_GROUP = 4
GLA_GATE_ROWS = 1024


def _split3(x):
    hi = x.astype(BF16)
    r1 = x - hi.astype(F32)
    mid = r1.astype(BF16)
    lo = (r1 - mid.astype(F32)).astype(BF16)
    return hi, mid, lo


def _gla_gate_body(r_ref, w2_ref, b2_ref, x_ref, tot_ref):
    T = GLA_BLOCK
    ti = lax.broadcasted_iota(jnp.int32, (T, T), 0)
    si = lax.broadcasted_iota(jnp.int32, (T, T), 1)
    r_hi, r_mid, _ = _split3(r_ref[...])
    for d in range(2):
        w_hi, w_mid, _ = _split3(w2_ref[d])
        z = b2_ref[d] + _dot(r_hi, w_hi) + _dot(r_hi, w_mid) + _dot(r_mid, w_hi)
        la = (jnp.minimum(z, 0.0) - jnp.log(1.0 + jnp.exp(-jnp.abs(z)))) * (1.0 / GLA_TAU)
        tri = jnp.where((si >= ti) if d else (si <= ti), 1.0, 0.0).astype(BF16)
        l_hi, l_mid, l_lo = _split3(la)
        for j in range(GLA_GATE_ROWS // T):
            rows = slice(j * T, (j + 1) * T)
            x = _dot(tri, l_hi[rows]) + _dot(tri, l_mid[rows]) + _dot(tri, l_lo[rows])
            x_ref[d, rows, :] = x
            tot_ref[d, j:j + 1, :] = x[0:1, :] if d else x[T - 1:T, :]


def _gla_gates(proj, w2, b2):
    n_rows = proj.shape[0]
    hk = H_GLA * DK_GLA
    tr = GLA_GATE_ROWS
    w2p = jnp.zeros((2, 128, hk), F32)
    w2p = w2p.at[0, 0:GLA_RANK].set(w2[0]).at[1, GLA_RANK:2 * GLA_RANK].set(w2[1])
    return pl.pallas_call(
        _gla_gate_body,
        grid=(n_rows // tr,),
        in_specs=[pl.BlockSpec((tr, 128), lambda i: (i, O_GLR // 128)),
                  pl.BlockSpec((2, 128, hk), lambda i: (0, 0, 0)),
                  pl.BlockSpec((2, 1, hk), lambda i: (0, 0, 0))],
        out_specs=[pl.BlockSpec((2, tr, hk), lambda i: (0, i, 0)),
                   pl.BlockSpec((2, tr // GLA_BLOCK, hk), lambda i: (0, i, 0))],
        out_shape=[jax.ShapeDtypeStruct((2, n_rows, hk), F32),
                   jax.ShapeDtypeStruct((2, n_rows // GLA_BLOCK, hk), F32)],
        compiler_params=_cparams(1),
        name="gla_gates",
    )(proj, w2p, b2.reshape(2, 1, hk))


def _gla_body(*refs, rev, n_blocks, finish):
    if finish:
        (flag_ref, q_ref, k_ref, v_ref, x_ref, s0_ref, of_ref, g_ref, o_ref, sfin_ref, st_ref, att_ref) = refs
    else:
        flag_ref, q_ref, k_ref, v_ref, x_ref, s0_ref, o_ref, sfin_ref, st_ref, att_ref = refs
    n = pl.program_id(1)
    blk = (n_blocks - 1 - n) if rev else n
    safe = flag_ref[pl.program_id(0) * n_blocks + blk] != 0
    T = GLA_BLOCK
    G = GLA_GROUP
    hk = H_GLA * DK_GLA
    heads = [slice(h * DK_GLA, (h + 1) * DK_GLA) for h in range(H_GLA)]
    scale = DK_GLA ** -0.5
    ti = lax.broadcasted_iota(jnp.int32, (T, T), 0)
    si = lax.broadcasted_iota(jnp.int32, (T, T), 1)
    causal = (si >= ti) if rev else (si <= ti)

    @pl.when(n == 0)
    def _():
        st_ref[...] = s0_ref[...]

    @pl.when(safe)
    def _():
        for j in range(G):
            x = x_ref[j]
            qs = (q_ref[j] * scale * jnp.exp(x)).astype(BF16)
            ks = (k_ref[j] * jnp.exp(-x)).astype(BF16)
            for h, hs in enumerate(heads):
                att_ref[j, h] = jnp.where(causal, _dot_nt(qs[:, hs], ks[:, hs]), 0.0)

    @pl.when(jnp.logical_not(safe))
    def _():
        t_idx = lax.broadcasted_iota(jnp.int32, (T, hk), 0)
        xr = ti ^ si
        for j in range(G):
            q = q_ref[j] * scale
            k = k_ref[j]
            x = x_ref[j]
            if rev:
                la = jnp.where(t_idx == T - 1, x, x - pltpu.roll(x, T - 1, 0))
            else:
                la = jnp.where(t_idx == 0, x, x - pltpu.roll(x, 1, 0))
            qb, kb = q.astype(BF16), k.astype(BF16)
            att = [_dot_nt(qb[:, hs], kb[:, hs]) for hs in heads]
            xg = la
            tg = la
            for lvl in range(GLA_LEVELS):
                sz = 1 << lvl
                upper = ((t_idx >> lvl) & 1) == 1
                is_q = jnp.logical_not(upper) if rev else upper
                partner = jnp.where(upper, pltpu.roll(tg, sz, 0), pltpu.roll(tg, T - sz, 0))
                e = jnp.exp(jnp.where(is_q, xg, tg - xg))
                qs = jnp.where(is_q, q * e, 0.0).astype(BF16)
                ks = jnp.where(is_q, 0.0, k * e).astype(BF16)
                for h, hs in enumerate(heads):
                    att[h] = jnp.where(xr >= sz, _dot_nt(qs[:, hs], ks[:, hs]), att[h])
                xg = xg + jnp.where(is_q, partner, 0.0)
                tg = tg + partner
            for h in range(H_GLA):
                att_ref[j, h] = att[h]

    for j in range(G):
        x = x_ref[j]
        tot = x[0:1, :] if rev else x[T - 1:T, :]
        v = v_ref[j]
        qe = (q_ref[j] * scale * jnp.exp(x)).astype(BF16)
        kw = (k_ref[j] * jnp.exp(tot - x)).astype(BF16)
        vt = v.T
        vb = v.astype(BF16)
        dec = jnp.exp(tot)
        for h, hs in enumerate(heads):
            vs = slice(h * DV_GLA, (h + 1) * DV_GLA)
            st = st_ref[j, h]
            o = _dot(att_ref[j, h].astype(BF16), vb[:, vs]) + _dot_nt(qe[:, hs], st.astype(BF16))
            if finish:
                o = o + of_ref[j, :, vs]
                o_ref[j, :, vs] = (_norm_rows(o) * _silu(g_ref[j, :, vs])).astype(BF16)
            else:
                o_ref[j, :, vs] = o
            st_ref[j, h] = st * dec[:, hs] + _dot(vt[vs, :].astype(BF16), kw[:, hs])

    @pl.when(n == n_blocks - 1)
    def _():
        sfin_ref[...] = st_ref[...]


def _gla(proj, w2, b2, s0t, *, n_seq, seq):
    nb = seq // GLA_BLOCK
    G = GLA_GROUP
    hk, hv = H_GLA * DK_GLA, H_GLA * DV_GLA
    x, tot = _gla_gates(proj, w2, b2)
    x = x.reshape(2, n_seq, seq, hk)
    safe = (tot.min(axis=-1) > -GLA_SAFE).reshape(2, n_seq // G, G, nb).all(axis=2)
    flags = safe.astype(jnp.int32).reshape(2, (n_seq // G) * nb)
    p3 = proj.reshape(n_seq, seq, ODD_IN_PAD)
    state_shape = jax.ShapeDtypeStruct((n_seq, H_GLA, DV_GLA, DK_GLA), F32)
    state_spec = pl.BlockSpec((G, H_GLA, DV_GLA, DK_GLA), lambda g, n, f: (g, 0, 0, 0))
    scratch = [pltpu.VMEM((G, H_GLA, DV_GLA, DK_GLA), F32), pltpu.VMEM((G, H_GLA, GLA_BLOCK, GLA_BLOCK), F32)]

    def run(rev, extra_args, extra_cols, out_dtype):
        blk = (lambda n: nb - 1 - n) if rev else (lambda n: n)
        d = 1 if rev else 0
        tok = lambda width, col: pl.BlockSpec((G, GLA_BLOCK, width), lambda g, n, f: (g, blk(n), col))
        grid_spec = pltpu.PrefetchScalarGridSpec(
            num_scalar_prefetch=1,
            grid=(n_seq // G, nb),
            in_specs=[tok(hk, O_GQ // hk), tok(hk, O_GK // hk), tok(hv, O_GV // hv),
                      pl.BlockSpec((None, G, GLA_BLOCK, hk), lambda g, n, f: (d, g, blk(n), 0)),
                      pl.BlockSpec((None, G, H_GLA, DV_GLA, DK_GLA), lambda g, n, f: (d, g, 0, 0, 0))]
            + [tok(hv, col) for col in extra_cols],
            out_specs=[tok(hv, 0), state_spec],
            scratch_shapes=scratch)
        return pl.pallas_call(
            functools.partial(_gla_body, rev=rev, n_blocks=nb, finish=rev),
            grid_spec=grid_spec,
            out_shape=[jax.ShapeDtypeStruct((n_seq, seq, hv), out_dtype), state_shape],
            compiler_params=_cparams(2),
            name="gla_bwd" if rev else "gla_fwd",
        )(flags[d], p3, p3, p3, x, s0t, *extra_args)

    o_f, s_f = run(False, [], [], F32)
    out, s_b = run(True, [o_f, p3], [0, O_GR // hv], BF16)
    return out.reshape(n_seq * seq, hv), jnp.stack([s_f, s_b], axis=0)


def kernel(x_prompt, x_sample, c, state_ret, cache_diff_k, cache_diff_v, cache_swa_k, cache_swa_v, state_gla,
           c_ctx, w_mod, b_mod, ln_g, ln_b, w_in_even, w_out_even, ret_decay, diff_lam, w_in_odd, w_out_odd,
           swa_sink, gla_w2, gla_b, w_ff1, w_ff2):
    xc = x_prompt.reshape(N_CTX, D_MODEL)
    xl = x_sample.reshape(N_LAT, D_MODEL)
    cond = jnp.concatenate([c_ctx[None, :], c, jnp.zeros((N_COND - 1 - DEC_BATCH, D_MODEL), F32)], axis=0)
    mod = _modulation(cond, w_mod, b_mod).reshape(DEPTH, N_COND, 1, 6 * D_MODEL)
    rope = _rope_tables(DEC_SEQ // GRID_W, DH_DIFF)

    w_in_even_b = w_in_even.astype(BF16)
    w_in_odd_b = _reorder_odd(w_in_odd).astype(BF16)
    w_out_even_b = w_out_even.astype(BF16)
    w_out_odd_b = w_out_odd.astype(BF16)
    w_ff1_b = w_ff1.astype(BF16)
    w_ff2_b = w_ff2.astype(BF16)

    ctx = dict(n_seq=BATCH, seq=SEQ)
    lat = dict(n_seq=DEC_BATCH, seq=DEC_SEQ)
    new_ret, new_dk, new_dv, new_sk, new_sv, new_gla = [], [], [], [], [], []
    for l in range(DEPTH):
        if l % 2 == 0:
            e = l // 2
            lam_init = 0.8 - 0.6 * math.exp(-0.3 * l)
            pc = _in_projection(xc, mod[l], w_in_even_b, e, False)
            pl_ = _in_projection(xl, mod[l], w_in_even_b, e, True)
            zero = jnp.zeros((BATCH, 2, H_RET, DK_RET, DV_RET), F32)
            a_ctx, s_ret = _retention(pc, ret_decay[e], zero, **ctx)
            a_lat, _ = _retention(pl_, ret_decay[e], state_ret[:, e], **lat)
            b_ctx, k_new, v_new = _diff_attention(pc, diff_lam[e], lam_init, **ctx)
            b_lat, = _diff_attention(pl_, diff_lam[e], lam_init, cache=(cache_diff_k[:, e], cache_diff_v[:, e]),
                                     rope=rope, **lat)
            new_ret.append(s_ret)
            new_dk.append(k_new)
            new_dv.append(v_new)
            w_out, idx = w_out_even_b, e
        else:
            o = l // 2
            pc = _in_projection(xc, mod[l], w_in_odd_b, o, False)
            pl_ = _in_projection(xl, mod[l], w_in_odd_b, o, True)
            a_ctx, k_new, v_new = _swa_attention(pc, swa_sink[o], **ctx)
            a_lat, = _swa_attention(pl_, swa_sink[o], cache=(cache_swa_k[:, o], cache_swa_v[:, o]), rope=rope, **lat)
            zero = jnp.zeros((2, BATCH, H_GLA, DV_GLA, DK_GLA), F32)
            b_ctx, s_gla = _gla(pc, gla_w2[o], gla_b[o], zero, **ctx)
            s0t = state_gla[:, o].transpose(1, 0, 2, 4, 3)
            b_lat, _ = _gla(pl_, gla_w2[o], gla_b[o], s0t, **lat)
            new_sk.append(k_new)
            new_sv.append(v_new)
            new_gla.append(s_gla.transpose(1, 0, 2, 4, 3))
            w_out, idx = w_out_odd_b, o
        xc = _post_mixer(xc, a_ctx, b_ctx, mod[l], w_out, idx, ln_g, ln_b, w_ff1_b, w_ff2_b, l, False)
        xl = _post_mixer(xl, a_lat, b_lat, mod[l], w_out, idx, ln_g, ln_b, w_ff1_b, w_ff2_b, l, True)
    return (xc.reshape(BATCH, SEQ, D_MODEL), xl.reshape(DEC_BATCH, DEC_SEQ, D_MODEL),
            jnp.stack(new_ret, axis=1), jnp.stack(new_dk, axis=1), jnp.stack(new_dv, axis=1),
            jnp.stack(new_sk, axis=1), jnp.stack(new_sv, axis=1), jnp.stack(new_gla, axis=1))
```

```python
import functools
import math

import numpy as np
import jax
import jax.numpy as jnp
from jax import lax
from jax.experimental import pallas as pl
from jax.experimental.pallas import tpu as pltpu

F32 = jnp.float32
BF16 = jnp.bfloat16

D_MODEL = 1024
BATCH = 16
SEQ = 256
DEPTH = 4
DEC_BATCH = 4
DEC_SEQ = 2048
PAST_LEN = 256
GRID_W = 64
H_RET, DK_RET, DV_RET = 4, 64, 128
H_DIFF, DH_DIFF, DV_DIFF = 4, 64, 128
H_SWA, KV_SWA, DH_SWA = 8, 2, 64
SWA_GROUP = H_SWA // KV_SWA
WINDOW = 128
H_GLA, DK_GLA, DV_GLA = 4, 64, 128
GLA_RANK = 16
GLA_TAU = 16.0
D_FF = 4 * D_MODEL
ROPE_BASE = 10000.0
N_EVEN = (DEPTH + 1) // 2
N_ODD = DEPTH // 2
ALPHA = (2 * DEPTH) ** 0.25
EVEN_IN = 3072
ODD_IN = 2336
ODD_IN_PAD = 2432
EPS = 1e-5
LOG2E = 1.4426950408889634

E_RQ, E_RK, E_RV, E_RG, E_DQ, E_DK, E_DV = 0, 256, 512, 1024, 1536, 2048, 2560
O_SQ, O_GV, O_GR, O_GQ, O_GK, O_SK, O_SV, O_GLR = 0, 512, 1024, 1536, 1792, 2048, 2176, 2304


def _reorder_odd(w):
    sq, sk, sv, gq, gk, gv, gr, glr = jnp.split(w, [512, 640, 768, 1024, 1280, 1792, 2304], axis=-1)
    pad = jnp.zeros(w.shape[:-1] + (ODD_IN_PAD - ODD_IN,), w.dtype)
    return jnp.concatenate([sq, gv, gr, gq, gk, sk, sv, glr, pad], axis=-1)


N_CTX = BATCH * SEQ
N_LAT = DEC_BATCH * DEC_SEQ
ROW_TILE = 256
DENSE_TILE = 512
N_COND = 8

VMEM_LIMIT = 56 * 1024 * 1024


def _cparams(n_axes):
    return pltpu.CompilerParams(dimension_semantics=("arbitrary",) * n_axes,
                                vmem_limit_bytes=VMEM_LIMIT)


def _dot(a, b):
    return jnp.dot(a, b, preferred_element_type=F32)


def _dot_nt(a, b):
    return lax.dot_general(a, b, (((1,), (1,)), ((), ())), preferred_element_type=F32)


def _silu(x):
    return x * (1.0 / (1.0 + jnp.exp(-x)))


def _norm_rows(x):
    mu = jnp.mean(x, axis=-1, keepdims=True)
    xc = x - mu
    var = jnp.mean(xc * xc, axis=-1, keepdims=True)
    return xc * lax.rsqrt(var + EPS)


def _mod_row(i, latent):
    return 1 + i // (DEC_SEQ // DENSE_TILE) if latent else 0


def _resident(shape, index_map):
    return pl.BlockSpec(shape, index_map, pipeline_mode=pl.Buffered(1))


def _mod_body(c_ref, w_ref, b_ref, o_ref):
    c = _silu(c_ref[...]).astype(BF16)
    o_ref[...] = _dot(c, w_ref[...].astype(BF16)) + b_ref[...]


def _modulation(cond, w_mod, b_mod):
    tn = 1536
    return pl.pallas_call(
        _mod_body,
        grid=(DEPTH, 6 * D_MODEL // tn),
        in_specs=[pl.BlockSpec((N_COND, D_MODEL), lambda l, j: (0, 0)),
                  pl.BlockSpec((None, D_MODEL, tn), lambda l, j: (l, 0, j)),
                  pl.BlockSpec((None, 1, tn), lambda l, j: (l, 0, j))],
        out_specs=pl.BlockSpec((None, N_COND, tn), lambda l, j: (l, 0, j)),
        out_shape=jax.ShapeDtypeStruct((DEPTH, N_COND, 6 * D_MODEL), F32),
        compiler_params=_cparams(2),
        name="modulation",
    )(cond, w_mod, b_mod.reshape(DEPTH, 1, 6 * D_MODEL))


def _inproj_body(x_ref, mod_ref, w_ref, o_ref):
    sh = mod_ref[:, 0:D_MODEL]
    sc = mod_ref[:, D_MODEL:2 * D_MODEL]
    h = (x_ref[...] * (1.0 + sc) + sh).astype(BF16)
    o_ref[...] = _dot(h, w_ref[...])


def _in_projection(x, mod_l, w_stack, idx, latent):
    n_in = w_stack.shape[2]
    n_rows = x.shape[0]
    tm = DENSE_TILE
    return pl.pallas_call(
        _inproj_body,
        grid=(n_rows // tm,),
        in_specs=[pl.BlockSpec((tm, D_MODEL), lambda i: (i, 0)),
                  pl.BlockSpec((None, 1, 6 * D_MODEL), lambda i: (_mod_row(i, latent), 0, 0)),
                  _resident((None, D_MODEL, n_in), lambda i: (idx, 0, 0))],
        out_specs=pl.BlockSpec((tm, n_in), lambda i: (i, 0)),
        out_shape=jax.ShapeDtypeStruct((n_rows, n_in), F32),
        compiler_params=_cparams(1),
        name="in_projection",
    )(x, mod_l, w_stack)


def _layer_norm(x, g, b):
    return _norm_rows(x) * g + b


def _post_body(x_ref, ma_ref, mb_ref, mod_ref, wo_ref, g_ref, b_ref, w1_ref, w2_ref, o_ref):
    half = D_MODEL // 2
    gt1 = mod_ref[:, 2 * D_MODEL:3 * D_MODEL]
    sh2 = mod_ref[:, 3 * D_MODEL:4 * D_MODEL]
    sc2 = mod_ref[:, 4 * D_MODEL:5 * D_MODEL]
    gt2 = mod_ref[:, 5 * D_MODEL:6 * D_MODEL]
    mix = _dot(ma_ref[...], wo_ref[0:half, :]) + _dot(mb_ref[...], wo_ref[half:D_MODEL, :])
    x1 = _layer_norm(ALPHA * x_ref[...] + gt1 * mix, g_ref[0:1, :], b_ref[0:1, :])
    hf = (x1 * (1.0 + sc2) + sh2).astype(BF16)
    ff = jnp.zeros((DENSE_TILE, D_MODEL), F32)
    chunk = 1024
    for j in range(D_FF // chunk):
        h1 = _dot(hf, w1_ref[:, j * chunk:(j + 1) * chunk])
        h1 = jnp.square(jnp.maximum(h1, 0.0)).astype(BF16)
        ff = ff + _dot(h1, w2_ref[j * chunk:(j + 1) * chunk, :])
    o_ref[...] = _layer_norm(ALPHA * x1 + gt2 * ff, g_ref[1:2, :], b_ref[1:2, :])


def _post_mixer(x, mix_a, mix_b, mod_l, w_out_stack, idx, ln_g, ln_b, w1_stack, w2_stack, layer, latent):
    half = D_MODEL // 2
    n_rows = x.shape[0]
    tm = DENSE_TILE
    return pl.pallas_call(
        _post_body,
        grid=(n_rows // tm,),
        in_specs=[pl.BlockSpec((tm, D_MODEL), lambda i: (i, 0)),
                  pl.BlockSpec((tm, half), lambda i: (i, 0)),
                  pl.BlockSpec((tm, half), lambda i: (i, 0)),
                  pl.BlockSpec((None, 1, 6 * D_MODEL), lambda i: (_mod_row(i, latent), 0, 0)),
                  _resident((None, D_MODEL, D_MODEL), lambda i: (idx, 0, 0)),
                  _resident((None, 2, D_MODEL), lambda i: (layer, 0, 0)),
                  _resident((None, 2, D_MODEL), lambda i: (layer, 0, 0)),
                  _resident((None, D_MODEL, D_FF), lambda i: (layer, 0, 0)),
                  _resident((None, D_FF, D_MODEL), lambda i: (layer, 0, 0))],
        out_specs=pl.BlockSpec((tm, D_MODEL), lambda i: (i, 0)),
        out_shape=jax.ShapeDtypeStruct((n_rows, D_MODEL), F32),
        compiler_params=_cparams(1),
        name="post_mixer",
    )(x, mix_a, mix_b, mod_l, w_out_stack, ln_g, ln_b, w1_stack, w2_stack)


def _rope_tables(rows, dim):
    row = jnp.repeat(jnp.arange(rows), GRID_W).astype(F32)
    col = jnp.tile(jnp.arange(GRID_W), rows).astype(F32)
    half = dim // 2
    freqs = ROPE_BASE ** (-jnp.arange(0, half, 2, dtype=F32) / half)
    ar = row[:, None] * freqs
    ac = col[:, None] * freqs
    cr, sr, cc, sc = jnp.cos(ar), jnp.sin(ar), jnp.cos(ac), jnp.sin(ac)
    cos = jnp.concatenate([cr, cr, cc, cc], axis=-1)
    sin = jnp.concatenate([-sr, sr, -sc, sc], axis=-1)
    reps = 128 // dim
    return jnp.tile(cos, (1, reps)), jnp.tile(sin, (1, reps))


def _rope128(x, cos, sin):
    lane = lax.broadcasted_iota(jnp.int32, x.shape, 1)
    first = (lane & 31) < 16
    swapped = jnp.where(first, pltpu.roll(x, 112, 1), pltpu.roll(x, 16, 1))
    return x * cos + swapped * sin


RET_CHUNK = 256


def _ret_body(q_ref, k_ref, v_ref, g_ref, rdk_ref, rdh_ref, s0_ref, o_ref, sfin_ref, *, seq):
    C = RET_CHUNK
    n_chunks = seq // C
    scale = DK_RET ** -0.5
    ii = lax.broadcasted_iota(jnp.int32, (C, C), 0)
    jj = lax.broadcasted_iota(jnp.int32, (C, C), 1)
    dist = (ii - jj).astype(F32)
    adist = jnp.abs(dist)
    pos = lax.broadcasted_iota(jnp.int32, (C, H_RET * DK_RET), 0).astype(F32)
    lgf = -jnp.exp(rdk_ref[0])
    lgb = -jnp.exp(rdk_ref[1])
    k_wf = jnp.exp(lgf * (C - 1.0 - pos)) * scale
    k_wb = jnp.exp(lgb * pos) * scale
    q_wf = jnp.exp(lgf * (pos + 1.0))
    q_wb = jnp.exp(lgb * (C - pos))

    uf, ub = [], []
    for n in range(n_chunks):
        kn = k_ref[n * C:(n + 1) * C, :]
        kft = (kn * k_wf).T.astype(BF16)
        kbt = (kn * k_wb).T.astype(BF16)
        ufn, ubn = [], []
        for h in range(H_RET):
            vh = v_ref[n * C:(n + 1) * C, h * DV_RET:(h + 1) * DV_RET].astype(BF16)
            ufn.append(_dot(kft[h * DK_RET:(h + 1) * DK_RET, :], vh))
            ubn.append(_dot(kbt[h * DK_RET:(h + 1) * DK_RET, :], vh))
        uf.append(ufn)
        ub.append(ubn)

    for h in range(H_RET):
        lgf_h = -jnp.exp(rdh_ref[0, h])
        lgb_h = -jnp.exp(rdh_ref[1, h])
        cf = jnp.exp(lgf_h * float(C))[:, 0:DV_RET]
        cb = jnp.exp(lgb_h * float(C))[:, 0:DV_RET]
        dmat = jnp.exp(jnp.where(dist > 0, lgf_h, lgb_h) * adist) + jnp.where(dist == 0, 1.0, 0.0)
        sf = [s0_ref[0, h]]
        for n in range(n_chunks):
            sf.append(cf * sf[n] + uf[n][h])
        sb = [None] * n_chunks
        sb[n_chunks - 1] = s0_ref[1, h]
        for n in range(n_chunks - 1, 0, -1):
            sb[n - 1] = cb * sb[n] + ub[n][h]
        sfin_ref[0, h] = sf[n_chunks]
        sfin_ref[1, h] = cb * sb[0] + ub[0][h]
        lanes = slice(h * DK_RET, (h + 1) * DK_RET)
        for n in range(n_chunks):
            rows = slice(n * C, (n + 1) * C)
            qh = q_ref[rows, lanes]
            kh = (k_ref[rows, lanes] * scale).astype(BF16)
            vh = v_ref[rows, h * DV_RET:(h + 1) * DV_RET].astype(BF16)
            s = _dot_nt(qh.astype(BF16), kh) * dmat
            o = _dot(s.astype(BF16), vh)
            o = o + _dot((qh * q_wf[:, lanes]).astype(BF16), sf[n].astype(BF16))
            o = o + _dot((qh * q_wb[:, lanes]).astype(BF16), sb[n].astype(BF16))
            gate = _silu(g_ref[rows, h * DV_RET:(h + 1) * DV_RET])
            o_ref[rows, h * DV_RET:(h + 1) * DV_RET] = (_norm_rows(o) * gate).astype(BF16)


def _retention(proj, ret_decay_e, s0, *, n_seq, seq):
    rdk = jnp.repeat(ret_decay_e, DK_RET, axis=-1).reshape(2, 1, H_RET * DK_RET)
    rdh = jnp.broadcast_to(ret_decay_e[:, :, None, None], (2, H_RET, 1, RET_CHUNK))
    hk, hv = H_RET * DK_RET, H_RET * DV_RET
    return pl.pallas_call(
        functools.partial(_ret_body, seq=seq),
        grid=(n_seq,),
        in_specs=[pl.BlockSpec((seq, hk), lambda b: (b, E_RQ // hk)),
                  pl.BlockSpec((seq, hk), lambda b: (b, E_RK // hk)),
                  pl.BlockSpec((seq, hv), lambda b: (b, E_RV // hv)),
                  pl.BlockSpec((seq, hv), lambda b: (b, E_RG // hv)),
                  pl.BlockSpec((2, 1, hk), lambda b: (0, 0, 0)),
                  pl.BlockSpec((2, H_RET, 1, RET_CHUNK), lambda b: (0, 0, 0, 0)),
                  pl.BlockSpec((None, 2, H_RET, DK_RET, DV_RET), lambda b: (b, 0, 0, 0, 0))],
        out_specs=[pl.BlockSpec((seq, hv), lambda b: (b, 0)),
                   pl.BlockSpec((None, 2, H_RET, DK_RET, DV_RET), lambda b: (b, 0, 0, 0, 0))],
        out_shape=[jax.ShapeDtypeStruct((n_seq * seq, hv), BF16),
                   jax.ShapeDtypeStruct((n_seq, 2, H_RET, DK_RET, DV_RET), F32)],
        compiler_params=_cparams(1),
        name="retention",
    )(proj, proj, proj, proj, rdk, rdh, s0)


def _col_softmax(parts, extra=None):
    m = parts[0].max(axis=0, keepdims=True)
    for p in parts[1:]:
        m = jnp.maximum(m, p.max(axis=0, keepdims=True))
    if extra is not None:
        m = jnp.maximum(m, extra)
    es = [jnp.exp2(p - m) for p in parts]
    den = es[0].sum(axis=0, keepdims=True)
    for e in es[1:]:
        den = den + e.sum(axis=0, keepdims=True)
    if extra is not None:
        den = den + jnp.exp2(extra - m)
    return es, 1.0 / den


def _row_softmax_pv(qb, keys, vals):
    parts = [_dot_nt(qb, k) for k in keys]
    m = parts[0].max(axis=-1, keepdims=True)
    for p in parts[1:]:
        m = jnp.maximum(m, p.max(axis=-1, keepdims=True))
    acc = den = None
    for p, v in zip(parts, vals):
        e = jnp.exp2(p - m)
        d = e.sum(axis=-1, keepdims=True)
        t = _dot(e.astype(BF16), v)
        den = d if den is None else den + d
        acc = t if acc is None else acc + t
    return acc, 1.0 / den


DIFF_TQ = 256
DIFF_CHUNK = 256


def _diff_lambda(p, lam_init):
    a = jnp.sum(p[0:1, :] * p[1:2, :], axis=-1, keepdims=True)
    b = jnp.sum(p[2:3, :] * p[3:4, :], axis=-1, keepdims=True)
    return jnp.exp(a) - jnp.exp(b) + lam_init


def _split_maps(k):
    lane = lax.broadcasted_iota(jnp.int32, k.shape, 1)
    return (jnp.where(lane < DH_DIFF, k, 0.0).astype(BF16), jnp.where(lane >= DH_DIFF, k, 0.0).astype(BF16))


def _diff_latent_body(lam_ref, q_ref, k_ref, v_ref, ck_ref, cv_ref, cq_ref, sq_ref, ckk_ref, skk_ref,
                      o_ref, km_ref, vb_ref, ckm_ref, cvb_ref, *, lam_init):
    i = pl.program_id(2)

    @pl.when(i == 0)
    def _():
        k1, k2 = _split_maps(_rope128(k_ref[...], ckk_ref[...], skk_ref[...]))
        km_ref[0] = k1
        km_ref[1] = k2
        vb_ref[...] = v_ref[...].astype(BF16)
        c1, c2 = _split_maps(jnp.concatenate([ck_ref[0], ck_ref[1]], axis=1))
        ckm_ref[0] = c1
        ckm_ref[1] = c2
        cvb_ref[...] = cv_ref[...].astype(BF16)

    lam = _diff_lambda(lam_ref[...], lam_init)
    q = _rope128(q_ref[...], cq_ref[...], sq_ref[...])
    qb = (q * (DH_DIFF ** -0.5 * LOG2E)).astype(BF16)
    ck = DIFF_CHUNK
    n_chunks = (PAST_LEN + km_ref.shape[1]) // ck
    own = lambda c: slice(c * ck - PAST_LEN, (c + 1) * ck - PAST_LEN)
    keys = lambda m, c: ckm_ref[m] if c == 0 else km_ref[m, own(c), :]
    vals = lambda c: cvb_ref[...] if c == 0 else vb_ref[own(c), :]

    def row_max(chunks):
        m = chunks[0]
        for s in chunks[1:]:
            m = jnp.maximum(m, s)
        return m.max(axis=-1, keepdims=True)

    def add(a, b):
        return b if a is None else a + b

    s1 = [_dot_nt(qb, keys(0, c)) for c in range(n_chunks)]
    mx1 = row_max(s1)
    s2, e1, psum1 = [], [], None
    for c in range(n_chunks):
        s2.append(_dot_nt(qb, keys(1, c)))
        e = jnp.exp2(s1[c] - mx1)
        psum1 = add(psum1, e)
        e1.append(e.astype(BF16))
    mx2 = row_max(s2)
    e2, psum2, acc1 = [], None, None
    for c in range(n_chunks):
        acc1 = add(acc1, _dot(e1[c], vals(c)))
        e = jnp.exp2(s2[c] - mx2)
        psum2 = add(psum2, e)
        e2.append(e.astype(BF16))
    acc2 = None
    for c in range(n_chunks):
        acc2 = add(acc2, _dot(e2[c], vals(c)))
    inv1 = 1.0 / psum1.sum(axis=-1, keepdims=True)
    inv2 = 1.0 / psum2.sum(axis=-1, keepdims=True)
    o = acc1 * inv1 - acc2 * (inv2 * lam)
    o_ref[...] = (_norm_rows(o) * (1.0 - lam_init)).astype(BF16)


def _diff_ctx_body(lam_ref, q_ref, k_ref, v_ref, o_ref, kc_ref, vc_ref, *, lam_init):
    kk = k_ref[...]
    vv = v_ref[...]
    kc_ref[0] = kk[:, 0:DH_DIFF]
    kc_ref[1] = kk[:, DH_DIFF:2 * DH_DIFF]
    vc_ref[...] = vv
    kms = _split_maps(kk)
    vals = [vv.astype(BF16)]
    lam = _diff_lambda(lam_ref[...], lam_init)
    qb = (q_ref[...] * (DH_DIFF ** -0.5 * LOG2E)).astype(BF16)
    acc1, inv1 = _row_softmax_pv(qb, [kms[0]], vals)
    acc2, inv2 = _row_softmax_pv(qb, [kms[1]], vals)
    o = acc1 * inv1 - acc2 * (inv2 * lam)
    o_ref[...] = (_norm_rows(o) * (1.0 - lam_init)).astype(BF16)


def _diff_attention(proj, lam_p, lam_init, *, n_seq, seq, cache=None, rope=None):
    latent = cache is not None
    tq = min(DIFF_TQ, seq)
    nq = seq // tq
    q_col, k_col, v_col = E_DQ // 128, E_DK // 128, E_DV // 128
    q_tile = o_tile = lambda i: i
    in_specs = [pl.BlockSpec((4, DH_DIFF), lambda b, h, i: (0, 0)),
                pl.BlockSpec((tq, 128), lambda b, h, i: (b * nq + q_tile(i), q_col + h)),
                pl.BlockSpec((seq, 128), lambda b, h, i: (b, k_col + h)),
                pl.BlockSpec((seq, 128), lambda b, h, i: (b, v_col + h))]
    args = [lam_p, proj, proj, proj]
    scratch = []
    out_specs = [pl.BlockSpec((tq, 128), lambda b, h, i: (b * nq + o_tile(i), h))]
    out_shape = [jax.ShapeDtypeStruct((n_seq * seq, H_DIFF * DV_DIFF), BF16)]
    if latent:
        ck, cv = cache
        cos, sin = rope
        in_specs += [pl.BlockSpec((None, None, 2, PAST_LEN, DH_DIFF), lambda b, h, i: (b, h, 0, 0, 0)),
                     pl.BlockSpec((None, None, PAST_LEN, DV_DIFF), lambda b, h, i: (b, h, 0, 0)),
                     pl.BlockSpec((tq, 128), lambda b, h, i: (q_tile(i), 0)),
                     pl.BlockSpec((tq, 128), lambda b, h, i: (q_tile(i), 0)),
                     pl.BlockSpec((seq, 128), lambda b, h, i: (0, 0)),
                     pl.BlockSpec((seq, 128), lambda b, h, i: (0, 0))]
        args += [ck, cv, cos, sin, cos, sin]
        scratch = [pltpu.VMEM((2, seq, 128), BF16), pltpu.VMEM((seq, DV_DIFF), BF16),
                   pltpu.VMEM((2, PAST_LEN, 128), BF16), pltpu.VMEM((PAST_LEN, DV_DIFF), BF16)]
        body, steps = _diff_latent_body, nq
    else:
        out_specs += [pl.BlockSpec((None, None, 2, seq, DH_DIFF), lambda b, h, i: (b, h, 0, 0, 0)),
                      pl.BlockSpec((None, None, seq, DV_DIFF), lambda b, h, i: (b, h, 0, 0))]
        out_shape += [jax.ShapeDtypeStruct((n_seq, H_DIFF, 2, seq, DH_DIFF), F32),
                      jax.ShapeDtypeStruct((n_seq, H_DIFF, seq, DV_DIFF), F32)]
        body, steps = _diff_ctx_body, nq
    return pl.pallas_call(
        functools.partial(body, lam_init=lam_init),
        grid=(n_seq, H_DIFF, steps),
        in_specs=in_specs,
        out_specs=out_specs,
        out_shape=out_shape,
        scratch_shapes=scratch,
        compiler_params=_cparams(3),
        name="diff_attention",
    )(*args)


SWA_TQ = 256
SWA_WIN = SWA_TQ + 2 * WINDOW
NEG = -1e30


def _swa_body(*refs, latent, seq):
    tq = SWA_TQ
    if latent:
        (q_ref, k_ref, v_ref, sink_ref, ck_ref, cv_ref, cq_ref, sq_ref, ckk_ref, skk_ref,
         o_ref, kr_ref, cvt_ref) = refs
        qi = pl.program_id(1)

        @pl.when(qi == 0)
        def _():
            kr_ref[...] = _rope128(k_ref[...], ckk_ref[...], skk_ref[...]).astype(BF16)
            cvt_ref[...] = jnp.concatenate([cv_ref[0], cv_ref[1]], axis=1).T.astype(BF16)

        cq, sq = cq_ref[...], sq_ref[...]
        q = jnp.concatenate([_rope128(q_ref[:, g * 128:(g + 1) * 128], cq, sq)
                             for g in range(H_SWA * DH_SWA // 128)], axis=1)
        ws = pl.multiple_of(jnp.clip(qi * tq - WINDOW, 0, seq - SWA_WIN), 128)
        kw = kr_ref[pl.ds(ws, SWA_WIN), :]
        vwt = v_ref[pl.ds(ws, SWA_WIN), :].T.astype(BF16)
        kpos = ws + lax.broadcasted_iota(jnp.int32, (SWA_WIN, tq), 0)
        qpos = qi * tq + lax.broadcasted_iota(jnp.int32, (SWA_WIN, tq), 1)
        bias1 = jnp.where(jnp.abs(qpos - kpos) <= WINDOW, 0.0, NEG)
        bias = jnp.concatenate([bias1] * SWA_GROUP, axis=1)
    else:
        q_ref, k_ref, v_ref, sink_ref, o_ref, kc_ref, vc_ref = refs
        q = q_ref[...]
        kk = k_ref[...]
        vv = v_ref[...]
        for kv in range(KV_SWA):
            kc_ref[kv] = kk[:, kv * DH_SWA:(kv + 1) * DH_SWA]
            vc_ref[kv] = vv[:, kv * DH_SWA:(kv + 1) * DH_SWA]
        kw = kk.astype(BF16)
        vwt = vv.T.astype(BF16)
    q = q * (DH_SWA ** -0.5 * LOG2E)
    pieces = []
    for kv in range(KV_SWA):
        lanes = slice(kv * DH_SWA, (kv + 1) * DH_SWA)
        q4 = jnp.concatenate([q[:, (kv * SWA_GROUP + g) * DH_SWA:(kv * SWA_GROUP + g + 1) * DH_SWA]
                              for g in range(SWA_GROUP)], axis=0).astype(BF16)
        sink = sink_ref[kv] * LOG2E
        s_loc = _dot_nt(kw[:, lanes], q4)
        if latent:
            parts = [_dot_nt(ck_ref[kv].astype(BF16), q4), s_loc + bias]
            vals_t = [cvt_ref[lanes, :], vwt[lanes, :]]
        else:
            parts = [s_loc]
            vals_t = [vwt[lanes, :]]
        es, inv = _col_softmax(parts, extra=sink)
        acc = None
        for e, vt in zip(es, vals_t):
            t = _dot(vt, e.astype(BF16))
            acc = t if acc is None else acc + t
        acc = acc * inv
        pieces += [acc[:, g * tq:(g + 1) * tq] for g in range(SWA_GROUP)]
    o_ref[...] = jnp.concatenate(pieces, axis=0).T.astype(BF16)


def _swa_attention(proj, sink, *, n_seq, seq, cache=None, rope=None):
    latent = cache is not None
    tq = SWA_TQ
    nq = seq // tq
    hq = H_SWA * DH_SWA
    sink_b = jnp.repeat(sink.reshape(KV_SWA, SWA_GROUP), tq, axis=1).reshape(KV_SWA, 1, SWA_GROUP * tq)
    in_specs = [pl.BlockSpec((tq, hq), lambda b, i: (b * nq + i, 0)),
                pl.BlockSpec((seq, 128), lambda b, i: (b, O_SK // 128)),
                pl.BlockSpec((seq, 128), lambda b, i: (b, O_SV // 128)),
                pl.BlockSpec((KV_SWA, 1, SWA_GROUP * tq), lambda b, i: (0, 0, 0))]
    args = [proj, proj, proj, sink_b]
    scratch = []
    out_specs = [pl.BlockSpec((tq, hq), lambda b, i: (b * nq + i, 0))]
    out_shape = [jax.ShapeDtypeStruct((n_seq * seq, hq), BF16)]
    if latent:
        ck, cv = cache
        cos, sin = rope
        in_specs += [pl.BlockSpec((None, KV_SWA, PAST_LEN, DH_SWA), lambda b, i: (b, 0, 0, 0)),
                     pl.BlockSpec((None, KV_SWA, PAST_LEN, DH_SWA), lambda b, i: (b, 0, 0, 0)),
                     pl.BlockSpec((tq, 128), lambda b, i: (i, 0)),
                     pl.BlockSpec((tq, 128), lambda b, i: (i, 0)),
                     pl.BlockSpec((seq, 128), lambda b, i: (0, 0)),
                     pl.BlockSpec((seq, 128), lambda b, i: (0, 0))]
        args += [ck, cv, cos, sin, cos, sin]
        scratch = [pltpu.VMEM((seq, 128), BF16), pltpu.VMEM((KV_SWA * DH_SWA, PAST_LEN), BF16)]
    else:
        cache_spec = pl.BlockSpec((None, KV_SWA, seq, DH_SWA), lambda b, i: (b, 0, 0, 0))
        cache_shape = jax.ShapeDtypeStruct((n_seq, KV_SWA, seq, DH_SWA), F32)
        out_specs += [cache_spec, cache_spec]
        out_shape += [cache_shape, cache_shape]
    return pl.pallas_call(
        functools.partial(_swa_body, latent=latent, seq=seq),
        grid=(n_seq, nq),
        in_specs=in_specs,
        out_specs=out_specs,
        out_shape=out_shape,
        scratch_shapes=scratch,
        compiler_params=_cparams(2),
        name="swa_attention",
    )(*args)


GLA_BLOCK = 128
GLA_LEVELS = 7
GLA_SAFE = 60.0
GLA_GROUP = 4
GLA_GATE_ROWS = 1024


def _split3(x):
    hi = x.astype(BF16)
    r1 = x - hi.astype(F32)
    mid = r1.astype(BF16)
    lo = (r1 - mid.astype(F32)).astype(BF16)
    return hi, mid, lo


def _gla_gate_body(r_ref, w2_ref, b2_ref, x_ref, tot_ref):
    T = GLA_BLOCK
    ti = lax.broadcasted_iota(jnp.int32, (T, T), 0)
    si = lax.broadcasted_iota(jnp.int32, (T, T), 1)
    r_hi, r_mid, _ = _split3(r_ref[...])
    for d in range(2):
        w_hi, w_mid, _ = _split3(w2_ref[d])
        z = b2_ref[d] + _dot(r_hi, w_hi) + _dot(r_hi, w_mid) + _dot(r_mid, w_hi)
        la = (jnp.minimum(z, 0.0) - jnp.log(1.0 + jnp.exp(-jnp.abs(z)))) * (1.0 / GLA_TAU)
        tri = jnp.where((si >= ti) if d else (si <= ti), 1.0, 0.0).astype(BF16)
        l_hi, l_mid, l_lo = _split3(la)
        for j in range(GLA_GATE_ROWS // T):
            rows = slice(j * T, (j + 1) * T)
            x = _dot(tri, l_hi[rows]) + _dot(tri, l_mid[rows]) + _dot(tri, l_lo[rows])
            x_ref[d, rows, :] = x
            tot_ref[d, j:j + 1, :] = x[0:1, :] if d else x[T - 1:T, :]


def _gla_gates(proj, w2, b2):
    n_rows = proj.shape[0]
    hk = H_GLA * DK_GLA
    tr = GLA_GATE_ROWS
    w2p = jnp.zeros((2, 128, hk), F32)
    w2p = w2p.at[0, 0:GLA_RANK].set(w2[0]).at[1, GLA_RANK:2 * GLA_RANK].set(w2[1])
    return pl.pallas_call(
        _gla_gate_body,
        grid=(n_rows // tr,),
        in_specs=[pl.BlockSpec((tr, 128), lambda i: (i, O_GLR // 128)),
                  pl.BlockSpec((2, 128, hk), lambda i: (0, 0, 0)),
                  pl.BlockSpec((2, 1, hk), lambda i: (0, 0, 0))],
        out_specs=[pl.BlockSpec((2, tr, hk), lambda i: (0, i, 0)),
                   pl.BlockSpec((2, tr // GLA_BLOCK, hk), lambda i: (0, i, 0))],
        out_shape=[jax.ShapeDtypeStruct((2, n_rows, hk), F32),
                   jax.ShapeDtypeStruct((2, n_rows // GLA_BLOCK, hk), F32)],
        compiler_params=_cparams(1),
        name="gla_gates",
    )(proj, w2p, b2.reshape(2, 1, hk))


def _gla_body(*refs, rev, n_blocks, finish):
    if finish:
        (flag_ref, q_ref, k_ref, v_ref, x_ref, s0_ref, of_ref, g_ref, o_ref, sfin_ref, st_ref, att_ref) = refs
    else:
        flag_ref, q_ref, k_ref, v_ref, x_ref, s0_ref, o_ref, sfin_ref, st_ref, att_ref = refs
    n = pl.program_id(1)
    blk = (n_blocks - 1 - n) if rev else n
    safe = flag_ref[pl.program_id(0) * n_blocks + blk] != 0
    T = GLA_BLOCK
    G = GLA_GROUP
    hk = H_GLA * DK_GLA
    heads = [slice(h * DK_GLA, (h + 1) * DK_GLA) for h in range(H_GLA)]
    scale = DK_GLA ** -0.5
    ti = lax.broadcasted_iota(jnp.int32, (T, T), 0)
    si = lax.broadcasted_iota(jnp.int32, (T, T), 1)
    causal = (si >= ti) if rev else (si <= ti)

    @pl.when(n == 0)
    def _():
        st_ref[...] = s0_ref[...]

    @pl.when(safe)
    def _():
        for j in range(G):
            x = x_ref[j]
            qs = (q_ref[j] * scale * jnp.exp(x)).astype(BF16)
            ks = (k_ref[j] * jnp.exp(-x)).astype(BF16)
            for h, hs in enumerate(heads):
                att_ref[j, h] = jnp.where(causal, _dot_nt(qs[:, hs], ks[:, hs]), 0.0)

    @pl.when(jnp.logical_not(safe))
    def _():
        t_idx = lax.broadcasted_iota(jnp.int32, (T, hk), 0)
        xr = ti ^ si
        for j in range(G):
            q = q_ref[j] * scale
            k = k_ref[j]
            x = x_ref[j]
            if rev:
                la = jnp.where(t_idx == T - 1, x, x - pltpu.roll(x, T - 1, 0))
            else:
                la = jnp.where(t_idx == 0, x, x - pltpu.roll(x, 1, 0))
            qb, kb = q.astype(BF16), k.astype(BF16)
            att = [_dot_nt(qb[:, hs], kb[:, hs]) for hs in heads]
            xg = la
            tg = la
            for lvl in range(GLA_LEVELS):
                sz = 1 << lvl
                upper = ((t_idx >> lvl) & 1) == 1
                is_q = jnp.logical_not(upper) if rev else upper
                partner = jnp.where(upper, pltpu.roll(tg, sz, 0), pltpu.roll(tg, T - sz, 0))
                e = jnp.exp(jnp.where(is_q, xg, tg - xg))
                qs = jnp.where(is_q, q * e, 0.0).astype(BF16)
                ks = jnp.where(is_q, 0.0, k * e).astype(BF16)
                for h, hs in enumerate(heads):
                    att[h] = jnp.where(xr >= sz, _dot_nt(qs[:, hs], ks[:, hs]), att[h])
                xg = xg + jnp.where(is_q, partner, 0.0)
                tg = tg + partner
            for h in range(H_GLA):
                att_ref[j, h] = att[h]

    for j in range(G):
        x = x_ref[j]
        tot = x[0:1, :] if rev else x[T - 1:T, :]
        v = v_ref[j]
        qe = (q_ref[j] * scale * jnp.exp(x)).astype(BF16)
        kw = (k_ref[j] * jnp.exp(tot - x)).astype(BF16)
        vt = v.T
        vb = v.astype(BF16)
        dec = jnp.exp(tot)
        for h, hs in enumerate(heads):
            vs = slice(h * DV_GLA, (h + 1) * DV_GLA)
            st = st_ref[j, h]
            o = _dot(att_ref[j, h].astype(BF16), vb[:, vs]) + _dot_nt(qe[:, hs], st.astype(BF16))
            if finish:
                o = o + of_ref[j, :, vs]
                o_ref[j, :, vs] = (_norm_rows(o) * _silu(g_ref[j, :, vs])).astype(BF16)
            else:
                o_ref[j, :, vs] = o
            st_ref[j, h] = st * dec[:, hs] + _dot(vt[vs, :].astype(BF16), kw[:, hs])

    @pl.when(n == n_blocks - 1)
    def _():
        sfin_ref[...] = st_ref[...]


def _gla(proj, w2, b2, s0t, *, n_seq, seq):
    nb = seq // GLA_BLOCK
    G = GLA_GROUP
    hk, hv = H_GLA * DK_GLA, H_GLA * DV_GLA
    x, tot = _gla_gates(proj, w2, b2)
    x = x.reshape(2, n_seq, seq, hk)
    safe = (tot.min(axis=-1) > -GLA_SAFE).reshape(2, n_seq // G, G, nb).all(axis=2)
    flags = safe.astype(jnp.int32).reshape(2, (n_seq // G) * nb)
    p3 = proj.reshape(n_seq, seq, ODD_IN_PAD)
    state_shape = jax.ShapeDtypeStruct((n_seq, H_GLA, DV_GLA, DK_GLA), F32)
    state_spec = pl.BlockSpec((G, H_GLA, DV_GLA, DK_GLA), lambda g, n, f: (g, 0, 0, 0))
    scratch = [pltpu.VMEM((G, H_GLA, DV_GLA, DK_GLA), F32), pltpu.VMEM((G, H_GLA, GLA_BLOCK, GLA_BLOCK), F32)]

    def run(rev, extra_args, extra_cols, out_dtype):
        blk = (lambda n: nb - 1 - n) if rev else (lambda n: n)
        d = 1 if rev else 0
        tok = lambda width, col: pl.BlockSpec((G, GLA_BLOCK, width), lambda g, n, f: (g, blk(n), col))
        grid_spec = pltpu.PrefetchScalarGridSpec(
            num_scalar_prefetch=1,
            grid=(n_seq // G, nb),
            in_specs=[tok(hk, O_GQ // hk), tok(hk, O_GK // hk), tok(hv, O_GV // hv),
                      pl.BlockSpec((None, G, GLA_BLOCK, hk), lambda g, n, f: (d, g, blk(n), 0)),
                      pl.BlockSpec((None, G, H_GLA, DV_GLA, DK_GLA), lambda g, n, f: (d, g, 0, 0, 0))]
            + [tok(hv, col) for col in extra_cols],
            out_specs=[tok(hv, 0), state_spec],
            scratch_shapes=scratch)
        return pl.pallas_call(
            functools.partial(_gla_body, rev=rev, n_blocks=nb, finish=rev),
            grid_spec=grid_spec,
            out_shape=[jax.ShapeDtypeStruct((n_seq, seq, hv), out_dtype), state_shape],
            compiler_params=_cparams(2),
            name="gla_bwd" if rev else "gla_fwd",
        )(flags[d], p3, p3, p3, x, s0t, *extra_args)

    o_f, s_f = run(False, [], [], F32)
    out, s_b = run(True, [o_f, p3], [0, O_GR // hv], BF16)
    return out.reshape(n_seq * seq, hv), jnp.stack([s_f, s_b], axis=0)


def kernel(x_prompt, x_sample, c, state_ret, cache_diff_k, cache_diff_v, cache_swa_k, cache_swa_v, state_gla,
           c_ctx, w_mod, b_mod, ln_g, ln_b, w_in_even, w_out_even, ret_decay, diff_lam, w_in_odd, w_out_odd,
           swa_sink, gla_w2, gla_b, w_ff1, w_ff2):
    xc = x_prompt.reshape(N_CTX, D_MODEL)
    xl = x_sample.reshape(N_LAT, D_MODEL)
    cond = jnp.concatenate([c_ctx[None, :], c, jnp.zeros((N_COND - 1 - DEC_BATCH, D_MODEL), F32)], axis=0)
    mod = _modulation(cond, w_mod, b_mod).reshape(DEPTH, N_COND, 1, 6 * D_MODEL)
    rope = _rope_tables(DEC_SEQ // GRID_W, DH_DIFF)

    w_in_even_b = w_in_even.astype(BF16)
    w_in_odd_b = _reorder_odd(w_in_odd.astype(BF16))
    w_out_even_b = w_out_even.astype(BF16)
    w_out_odd_b = w_out_odd.astype(BF16)
    w_ff1_b = w_ff1.astype(BF16)
    w_ff2_b = w_ff2.astype(BF16)

    ctx = dict(n_seq=BATCH, seq=SEQ)
    lat = dict(n_seq=DEC_BATCH, seq=DEC_SEQ)
    new_ret, new_dk, new_dv, new_sk, new_sv, new_gla = [], [], [], [], [], []
    for l in range(DEPTH):
        if l % 2 == 0:
            e = l // 2
            lam_init = 0.8 - 0.6 * math.exp(-0.3 * l)
            pc = _in_projection(xc, mod[l], w_in_even_b, e, False)
            pl_ = _in_projection(xl, mod[l], w_in_even_b, e, True)
            zero = jnp.zeros((BATCH, 2, H_RET, DK_RET, DV_RET), F32)
            a_ctx, s_ret = _retention(pc, ret_decay[e], zero, **ctx)
            a_lat, _ = _retention(pl_, ret_decay[e], state_ret[:, e], **lat)
            b_ctx, k_new, v_new = _diff_attention(pc, diff_lam[e], lam_init, **ctx)
            new_dk.append(k_new)
            new_dv.append(v_new)
            b_lat, = _diff_attention(pl_, diff_lam[e], lam_init, cache=(cache_diff_k[:, e], cache_diff_v[:, e]),
                                     rope=rope, **lat)
            new_ret.append(s_ret)
            w_out, idx = w_out_even_b, e
        else:
            o = l // 2
            pc = _in_projection(xc, mod[l], w_in_odd_b, o, False)
            pl_ = _in_projection(xl, mod[l], w_in_odd_b, o, True)
            a_ctx, k_new, v_new = _swa_attention(pc, swa_sink[o], **ctx)
            new_sk.append(k_new)
            new_sv.append(v_new)
            a_lat, = _swa_attention(pl_, swa_sink[o], cache=(cache_swa_k[:, o], cache_swa_v[:, o]), rope=rope, **lat)
            zero = jnp.zeros((2, BATCH, H_GLA, DV_GLA, DK_GLA), F32)
            b_ctx, s_gla = _gla(pc, gla_w2[o], gla_b[o], zero, **ctx)
            s0t = state_gla[:, o].transpose(1, 0, 2, 4, 3)
            b_lat, _ = _gla(pl_, gla_w2[o], gla_b[o], s0t, **lat)
            new_gla.append(s_gla.transpose(1, 0, 2, 4, 3))
            w_out, idx = w_out_odd_b, o
        xc = _post_mixer(xc, a_ctx, b_ctx, mod[l], w_out, idx, ln_g, ln_b, w_ff1_b, w_ff2_b, l, False)
        xl = _post_mixer(xl, a_lat, b_lat, mod[l], w_out, idx, ln_g, ln_b, w_ff1_b, w_ff2_b, l, True)
    return (xc.reshape(BATCH, SEQ, D_MODEL), xl.reshape(DEC_BATCH, DEC_SEQ, D_MODEL),
            jnp.stack(new_ret, axis=1), jnp.stack(new_dk, axis=1), jnp.stack(new_dv, axis=1),
            jnp.stack(new_sk, axis=1), jnp.stack(new_sv, axis=1), jnp.stack(new_gla, axis=1))
```

```python
import functools
import math

import numpy as np
import jax
import jax.numpy as jnp
from jax import lax
from jax.experimental import pallas as pl
from jax.experimental.pallas import tpu as pltpu

F32 = jnp.float32
BF16 = jnp.bfloat16

D_MODEL = 1024
BATCH = 16
SEQ = 256
DEPTH = 4
DEC_BATCH = 4
DEC_SEQ = 2048
PAST_LEN = 256
GRID_W = 64
H_RET, DK_RET, DV_RET = 4, 64, 128
H_DIFF, DH_DIFF, DV_DIFF = 4, 64, 128
H_SWA, KV_SWA, DH_SWA = 8, 2, 64
SWA_GROUP = H_SWA // KV_SWA
WINDOW = 128
H_GLA, DK_GLA, DV_GLA = 4, 64, 128
GLA_RANK = 16
GLA_TAU = 16.0
D_FF = 4 * D_MODEL
ROPE_BASE = 10000.0
N_EVEN = (DEPTH + 1) // 2
N_ODD = DEPTH // 2
ALPHA = (2 * DEPTH) ** 0.25
EVEN_IN = 3072
ODD_IN = 2336
ODD_IN_PAD = 2432
EPS = 1e-5
LOG2E = 1.4426950408889634

E_RQ, E_RK, E_RV, E_RG, E_DQ, E_DK, E_DV = 0, 256, 512, 1024, 1536, 2048, 2560
O_SQ, O_GV, O_GR, O_GQ, O_GK, O_SK, O_SV, O_GLR = 0, 512, 1024, 1536, 1792, 2048, 2176, 2304


def _reorder_odd(w):
    sq, sk, sv, gq, gk, gv, gr, glr = jnp.split(w, [512, 640, 768, 1024, 1280, 1792, 2304], axis=-1)
    pad = jnp.zeros(w.shape[:-1] + (ODD_IN_PAD - ODD_IN,), w.dtype)
    return jnp.concatenate([sq, gv, gr, gq, gk, sk, sv, glr, pad], axis=-1)


N_CTX = BATCH * SEQ
N_LAT = DEC_BATCH * DEC_SEQ
ROW_TILE = 256
DENSE_TILE = 512
POST_TILE = 1024
N_COND = 8

VMEM_LIMIT = 56 * 1024 * 1024


def _cparams(n_axes):
    return pltpu.CompilerParams(dimension_semantics=("arbitrary",) * n_axes,
                                vmem_limit_bytes=VMEM_LIMIT)


def _dot(a, b):
    return jnp.dot(a, b, preferred_element_type=F32)


def _dot_nt(a, b):
    return lax.dot_general(a, b, (((1,), (1,)), ((), ())), preferred_element_type=F32)


def _silu(x):
    return x * (1.0 / (1.0 + jnp.exp(-x)))


def _norm_rows(x):
    mu = jnp.mean(x, axis=-1, keepdims=True)
    xc = x - mu
    var = jnp.mean(xc * xc, axis=-1, keepdims=True)
    return xc * lax.rsqrt(var + EPS)


def _mod_row(i, latent, tile):
    return 1 + i // (DEC_SEQ // tile) if latent else 0


def _resident(shape, index_map):
    return pl.BlockSpec(shape, index_map, pipeline_mode=pl.Buffered(1))


def _mod_body(c_ref, w_ref, b_ref, o_ref):
    c = _silu(c_ref[...]).astype(BF16)
    o_ref[...] = _dot(c, w_ref[...].astype(BF16)) + b_ref[...]


def _modulation(cond, w_mod, b_mod):
    tn = 1536
    return pl.pallas_call(
        _mod_body,
        grid=(DEPTH, 6 * D_MODEL // tn),
        in_specs=[pl.BlockSpec((N_COND, D_MODEL), lambda l, j: (0, 0)),
                  pl.BlockSpec((None, D_MODEL, tn), lambda l, j: (l, 0, j)),
                  pl.BlockSpec((None, 1, tn), lambda l, j: (l, 0, j))],
        out_specs=pl.BlockSpec((None, N_COND, tn), lambda l, j: (l, 0, j)),
        out_shape=jax.ShapeDtypeStruct((DEPTH, N_COND, 6 * D_MODEL), F32),
        compiler_params=_cparams(2),
        name="modulation",
    )(cond, w_mod, b_mod.reshape(DEPTH, 1, 6 * D_MODEL))


def _inproj_body(x_ref, mod_ref, w_ref, o_ref):
    sh = mod_ref[:, 0:D_MODEL]
    sc = mod_ref[:, D_MODEL:2 * D_MODEL]
    h = (x_ref[...] * (1.0 + sc) + sh).astype(BF16)
    o_ref[...] = _dot(h, w_ref[...])


def _in_projection(x, mod_l, w_stack, idx, latent):
    n_in = w_stack.shape[2]
    n_rows = x.shape[0]
    tm = DENSE_TILE
    return pl.pallas_call(
        _inproj_body,
        grid=(n_rows // tm,),
        in_specs=[pl.BlockSpec((tm, D_MODEL), lambda i: (i, 0)),
                  pl.BlockSpec((None, 1, 6 * D_MODEL), lambda i: (_mod_row(i, latent, tm), 0, 0)),
                  _resident((None, D_MODEL, n_in), lambda i: (idx, 0, 0))],
        out_specs=pl.BlockSpec((tm, n_in), lambda i: (i, 0)),
        out_shape=jax.ShapeDtypeStruct((n_rows, n_in), F32),
        compiler_params=_cparams(1),
        name="in_projection",
    )(x, mod_l, w_stack)


def _layer_norm(x, g, b):
    return _norm_rows(x) * g + b


def _post_body(x_ref, ma_ref, mb_ref, mod_ref, wo_ref, g_ref, b_ref, w1_ref, w2_ref, o_ref):
    half = D_MODEL // 2
    gt1 = mod_ref[:, 2 * D_MODEL:3 * D_MODEL]
    sh2 = mod_ref[:, 3 * D_MODEL:4 * D_MODEL]
    sc2 = mod_ref[:, 4 * D_MODEL:5 * D_MODEL]
    gt2 = mod_ref[:, 5 * D_MODEL:6 * D_MODEL]
    chunk = 1024
    n_chunks = D_FF // chunk
    top, bot = slice(0, POST_TILE // 2), slice(POST_TILE // 2, POST_TILE)

    def mix(r):
        return _dot(ma_ref[r, :], wo_ref[0:half, :]) + _dot(mb_ref[r, :], wo_ref[half:D_MODEL, :])

    def norm1(r, m):
        x1 = _layer_norm(ALPHA * x_ref[r, :] + gt1 * m, g_ref[0:1, :], b_ref[0:1, :])
        return x1, (x1 * (1.0 + sc2) + sh2).astype(BF16)

    def ffn(hf, j):
        h1 = _dot(hf, w1_ref[:, j * chunk:(j + 1) * chunk])
        h1 = jnp.square(jnp.maximum(h1, 0.0)).astype(BF16)
        return _dot(h1, w2_ref[j * chunk:(j + 1) * chunk, :])

    def norm2(r, x1, ff):
        o_ref[r, :] = _layer_norm(ALPHA * x1 + gt2 * ff, g_ref[1:2, :], b_ref[1:2, :])

    m_top, m_bot = mix(top), mix(bot)
    x1_top, hf_top = norm1(top, m_top)
    ff_top = ffn(hf_top, 0)
    x1_bot, hf_bot = norm1(bot, m_bot)
    for j in range(1, n_chunks):
        ff_top = ff_top + ffn(hf_top, j)
    ff_bot = ffn(hf_bot, 0)
    norm2(top, x1_top, ff_top)
    for j in range(1, n_chunks):
        ff_bot = ff_bot + ffn(hf_bot, j)
    norm2(bot, x1_bot, ff_bot)


def _post_mixer(x, mix_a, mix_b, mod_l, w_out, ln_g, ln_b, w1, w2, layer, latent):
    half = D_MODEL // 2
    n_rows = x.shape[0]
    tm = POST_TILE
    return pl.pallas_call(
        _post_body,
        grid=(n_rows // tm,),
        in_specs=[pl.BlockSpec((tm, D_MODEL), lambda i: (i, 0)),
                  pl.BlockSpec((tm, half), lambda i: (i, 0)),
                  pl.BlockSpec((tm, half), lambda i: (i, 0)),
                  pl.BlockSpec((None, 1, 6 * D_MODEL), lambda i: (_mod_row(i, latent, tm), 0, 0)),
                  _resident((D_MODEL, D_MODEL), lambda i: (0, 0)),
                  _resident((None, 2, D_MODEL), lambda i: (layer, 0, 0)),
                  _resident((None, 2, D_MODEL), lambda i: (layer, 0, 0)),
                  _resident((D_MODEL, D_FF), lambda i: (0, 0)),
                  _resident((D_FF, D_MODEL), lambda i: (0, 0))],
        out_specs=pl.BlockSpec((tm, D_MODEL), lambda i: (i, 0)),
        out_shape=jax.ShapeDtypeStruct((n_rows, D_MODEL), F32),
        compiler_params=_cparams(1),
        name="post_mixer",
    )(x, mix_a, mix_b, mod_l, w_out, ln_g, ln_b, w1, w2)


def _rope_tables(rows, dim):
    row = jnp.repeat(jnp.arange(rows), GRID_W).astype(F32)
    col = jnp.tile(jnp.arange(GRID_W), rows).astype(F32)
    half = dim // 2
    freqs = ROPE_BASE ** (-jnp.arange(0, half, 2, dtype=F32) / half)
    ar = row[:, None] * freqs
    ac = col[:, None] * freqs
    cr, sr, cc, sc = jnp.cos(ar), jnp.sin(ar), jnp.cos(ac), jnp.sin(ac)
    cos = jnp.concatenate([cr, cr, cc, cc], axis=-1)
    sin = jnp.concatenate([-sr, sr, -sc, sc], axis=-1)
    reps = 128 // dim
    return jnp.tile(cos, (1, reps)), jnp.tile(sin, (1, reps))


def _rope128(x, cos, sin):
    lane = lax.broadcasted_iota(jnp.int32, x.shape, 1)
    first = (lane & 31) < 16
    swapped = jnp.where(first, pltpu.roll(x, 112, 1), pltpu.roll(x, 16, 1))
    return x * cos + swapped * sin


RET_CHUNK = 256


def _ret_body(q_ref, k_ref, v_ref, g_ref, rdk_ref, rdh_ref, s0_ref, o_ref, sfin_ref, *, seq):
    C = RET_CHUNK
    n_chunks = seq // C
    scale = DK_RET ** -0.5
    ii = lax.broadcasted_iota(jnp.int32, (C, C), 0)
    jj = lax.broadcasted_iota(jnp.int32, (C, C), 1)
    dist = (ii - jj).astype(F32)
    adist = jnp.abs(dist)
    pos = lax.broadcasted_iota(jnp.int32, (C, H_RET * DK_RET), 0).astype(F32)
    lgf = -jnp.exp(rdk_ref[0])
    lgb = -jnp.exp(rdk_ref[1])
    k_wf = jnp.exp(lgf * (C - 1.0 - pos)) * scale
    k_wb = jnp.exp(lgb * pos) * scale
    q_wf = jnp.exp(lgf * (pos + 1.0))
    q_wb = jnp.exp(lgb * (C - pos))

    uf, ub = [], []
    for n in range(n_chunks):
        kn = k_ref[n * C:(n + 1) * C, :]
        kft = (kn * k_wf).T.astype(BF16)
        kbt = (kn * k_wb).T.astype(BF16)
        ufn, ubn = [], []
        for h in range(H_RET):
            vh = v_ref[n * C:(n + 1) * C, h * DV_RET:(h + 1) * DV_RET].astype(BF16)
            ufn.append(_dot(kft[h * DK_RET:(h + 1) * DK_RET, :], vh))
            ubn.append(_dot(kbt[h * DK_RET:(h + 1) * DK_RET, :], vh))
        uf.append(ufn)
        ub.append(ubn)

    for h in range(H_RET):
        lgf_h = -jnp.exp(rdh_ref[0, h])
        lgb_h = -jnp.exp(rdh_ref[1, h])
        cf = jnp.exp(lgf_h * float(C))[:, 0:DV_RET]
        cb = jnp.exp(lgb_h * float(C))[:, 0:DV_RET]
        dmat = jnp.exp(jnp.where(dist > 0, lgf_h, lgb_h) * adist) + jnp.where(dist == 0, 1.0, 0.0)
        sf = [s0_ref[0, h]]
        for n in range(n_chunks):
            sf.append(cf * sf[n] + uf[n][h])
        sb = [None] * n_chunks
        sb[n_chunks - 1] = s0_ref[1, h]
        for n in range(n_chunks - 1, 0, -1):
            sb[n - 1] = cb * sb[n] + ub[n][h]
        sfin_ref[0, h] = sf[n_chunks]
        sfin_ref[1, h] = cb * sb[0] + ub[0][h]
        lanes = slice(h * DK_RET, (h + 1) * DK_RET)
        for n in range(n_chunks):
            rows = slice(n * C, (n + 1) * C)
            qh = q_ref[rows, lanes]
            kh = (k_ref[rows, lanes] * scale).astype(BF16)
            vh = v_ref[rows, h * DV_RET:(h + 1) * DV_RET].astype(BF16)
            s = _dot_nt(qh.astype(BF16), kh) * dmat
            o = _dot(s.astype(BF16), vh)
            o = o + _dot((qh * q_wf[:, lanes]).astype(BF16), sf[n].astype(BF16))
            o = o + _dot((qh * q_wb[:, lanes]).astype(BF16), sb[n].astype(BF16))
            gate = _silu(g_ref[rows, h * DV_RET:(h + 1) * DV_RET])
            o_ref[rows, h * DV_RET:(h + 1) * DV_RET] = (_norm_rows(o) * gate).astype(BF16)


def _retention(proj, ret_decay_e, s0, *, n_seq, seq):
    rdk = jnp.repeat(ret_decay_e, DK_RET, axis=-1).reshape(2, 1, H_RET * DK_RET)
    rdh = jnp.broadcast_to(ret_decay_e[:, :, None, None], (2, H_RET, 1, RET_CHUNK))
    hk, hv = H_RET * DK_RET, H_RET * DV_RET
    return pl.pallas_call(
        functools.partial(_ret_body, seq=seq),
        grid=(n_seq,),
        in_specs=[pl.BlockSpec((seq, hk), lambda b: (b, E_RQ // hk)),
                  pl.BlockSpec((seq, hk), lambda b: (b, E_RK // hk)),
                  pl.BlockSpec((seq, hv), lambda b: (b, E_RV // hv)),
                  pl.BlockSpec((seq, hv), lambda b: (b, E_RG // hv)),
                  pl.BlockSpec((2, 1, hk), lambda b: (0, 0, 0)),
                  pl.BlockSpec((2, H_RET, 1, RET_CHUNK), lambda b: (0, 0, 0, 0)),
                  pl.BlockSpec((None, 2, H_RET, DK_RET, DV_RET), lambda b: (b, 0, 0, 0, 0))],
        out_specs=[pl.BlockSpec((seq, hv), lambda b: (b, 0)),
                   pl.BlockSpec((None, 2, H_RET, DK_RET, DV_RET), lambda b: (b, 0, 0, 0, 0))],
        out_shape=[jax.ShapeDtypeStruct((n_seq * seq, hv), BF16),
                   jax.ShapeDtypeStruct((n_seq, 2, H_RET, DK_RET, DV_RET), F32)],
        compiler_params=_cparams(1),
        name="retention",
    )(proj, proj, proj, proj, rdk, rdh, s0)


def _cast_weight_chunks(src_refs, dst_refs):
    for src, dst in zip(src_refs, dst_refs):
        dst[...] = src[...].astype(BF16)


def _weight_cast_plan(weights, n_chunks, chunk_of):
    args, in_specs, out_specs, out_shapes = [], [], [], []
    for w, idx in weights:
        rows, cols = w.shape[1], w.shape[2]
        rb = rows // n_chunks
        assert rb * n_chunks == rows and rb % 16 == 0
        args.append(w)
        in_specs.append(pl.BlockSpec((None, rb, cols), lambda *g, idx=idx: (idx, chunk_of(*g), 0)))
        out_specs.append(pl.BlockSpec((rb, cols), lambda *g: (chunk_of(*g), 0)))
        out_shapes.append(jax.ShapeDtypeStruct((rows, cols), BF16))
    return args, in_specs, out_specs, out_shapes


def _col_softmax(parts, extra=None):
    m = parts[0].max(axis=0, keepdims=True)
    for p in parts[1:]:
        m = jnp.maximum(m, p.max(axis=0, keepdims=True))
    if extra is not None:
        m = jnp.maximum(m, extra)
    es = [jnp.exp2(p - m) for p in parts]
    den = es[0].sum(axis=0, keepdims=True)
    for e in es[1:]:
        den = den + e.sum(axis=0, keepdims=True)
    if extra is not None:
        den = den + jnp.exp2(extra - m)
    return es, 1.0 / den


def _row_softmax_pv(qb, keys, vals):
    parts = [_dot_nt(qb, k) for k in keys]
    m = parts[0].max(axis=-1, keepdims=True)
    for p in parts[1:]:
        m = jnp.maximum(m, p.max(axis=-1, keepdims=True))
    acc = den = None
    for p, v in zip(parts, vals):
        e = jnp.exp2(p - m)
        d = e.sum(axis=-1, keepdims=True)
        t = _dot(e.astype(BF16), v)
        den = d if den is None else den + d
        acc = t if acc is None else acc + t
    return acc, 1.0 / den


DIFF_TQ = 256
DIFF_CHUNK = 256


def _diff_lambda(p, lam_init):
    a = jnp.sum(p[0:1, :] * p[1:2, :], axis=-1, keepdims=True)
    b = jnp.sum(p[2:3, :] * p[3:4, :], axis=-1, keepdims=True)
    return jnp.exp(a) - jnp.exp(b) + lam_init


def _split_maps(k):
    lane = lax.broadcasted_iota(jnp.int32, k.shape, 1)
    return (jnp.where(lane < DH_DIFF, k, 0.0).astype(BF16), jnp.where(lane >= DH_DIFF, k, 0.0).astype(BF16))


def _diff_latent_body(lam_ref, q_ref, k_ref, v_ref, ck_ref, cv_ref, cq_ref, sq_ref, ckk_ref, skk_ref,
                      wo_ref, w1_ref, w2_ref, o_ref, wob_ref, w1b_ref, w2b_ref,
                      km_ref, vb_ref, ckm_ref, cvb_ref, *, lam_init):
    i = pl.program_id(2)
    _cast_weight_chunks((wo_ref, w1_ref, w2_ref), (wob_ref, w1b_ref, w2b_ref))

    @pl.when(i == 0)
    def _():
        k1, k2 = _split_maps(_rope128(k_ref[...], ckk_ref[...], skk_ref[...]))
        km_ref[0] = k1
        km_ref[1] = k2
        vb_ref[...] = v_ref[...].astype(BF16)
        c1, c2 = _split_maps(jnp.concatenate([ck_ref[0], ck_ref[1]], axis=1))
        ckm_ref[0] = c1
        ckm_ref[1] = c2
        cvb_ref[...] = cv_ref[...].astype(BF16)

    lam = _diff_lambda(lam_ref[...], lam_init)
    q = _rope128(q_ref[...], cq_ref[...], sq_ref[...])
    qb = (q * (DH_DIFF ** -0.5 * LOG2E)).astype(BF16)
    ck = DIFF_CHUNK
    n_chunks = (PAST_LEN + km_ref.shape[1]) // ck
    own = lambda c: slice(c * ck - PAST_LEN, (c + 1) * ck - PAST_LEN)
    keys = lambda m, c: ckm_ref[m] if c == 0 else km_ref[m, own(c), :]
    vals = lambda c: cvb_ref[...] if c == 0 else vb_ref[own(c), :]

    def row_max(chunks):
        m = chunks[0]
        for s in chunks[1:]:
            m = jnp.maximum(m, s)
        return m.max(axis=-1, keepdims=True)

    def add(a, b):
        return b if a is None else a + b

    s1 = [_dot_nt(qb, keys(0, c)) for c in range(n_chunks)]
    mx1 = row_max(s1)
    s2, e1, psum1 = [], [], None
    for c in range(n_chunks):
        s2.append(_dot_nt(qb, keys(1, c)))
        e = jnp.exp2(s1[c] - mx1)
        psum1 = add(psum1, e)
        e1.append(e.astype(BF16))
    mx2 = row_max(s2)
    e2, psum2, acc1 = [], None, None
    for c in range(n_chunks):
        acc1 = add(acc1, _dot(e1[c], vals(c)))
        e = jnp.exp2(s2[c] - mx2)
        psum2 = add(psum2, e)
        e2.append(e.astype(BF16))
    acc2 = None
    for c in range(n_chunks):
        acc2 = add(acc2, _dot(e2[c], vals(c)))
    inv1 = 1.0 / psum1.sum(axis=-1, keepdims=True)
    inv2 = 1.0 / psum2.sum(axis=-1, keepdims=True)
    o = acc1 * inv1 - acc2 * (inv2 * lam)
    o_ref[...] = (_norm_rows(o) * (1.0 - lam_init)).astype(BF16)


def _diff_ctx_body(lam_ref, q_ref, k_ref, v_ref, o_ref, kc_ref, vc_ref, *, lam_init):
    kk = k_ref[...]
    vv = v_ref[...]
    kc_ref[0] = kk[:, 0:DH_DIFF]
    kc_ref[1] = kk[:, DH_DIFF:2 * DH_DIFF]
    vc_ref[...] = vv
    kms = _split_maps(kk)
    vals = [vv.astype(BF16)]
    lam = _diff_lambda(lam_ref[...], lam_init)
    qb = (q_ref[...] * (DH_DIFF ** -0.5 * LOG2E)).astype(BF16)
    acc1, inv1 = _row_softmax_pv(qb, [kms[0]], vals)
    acc2, inv2 = _row_softmax_pv(qb, [kms[1]], vals)
    o = acc1 * inv1 - acc2 * (inv2 * lam)
    o_ref[...] = (_norm_rows(o) * (1.0 - lam_init)).astype(BF16)


def _diff_attention(proj, lam_p, lam_init, *, n_seq, seq, cache=None, rope=None, cast=None):
    latent = cache is not None
    tq = min(DIFF_TQ, seq)
    nq = seq // tq
    q_col, k_col, v_col = E_DQ // 128, E_DK // 128, E_DV // 128
    q_tile = o_tile = lambda i: i
    in_specs = [pl.BlockSpec((4, DH_DIFF), lambda b, h, i: (0, 0)),
                pl.BlockSpec((tq, 128), lambda b, h, i: (b * nq + q_tile(i), q_col + h)),
                pl.BlockSpec((seq, 128), lambda b, h, i: (b, k_col + h)),
                pl.BlockSpec((seq, 128), lambda b, h, i: (b, v_col + h))]
    args = [lam_p, proj, proj, proj]
    scratch = []
    out_specs = [pl.BlockSpec((tq, 128), lambda b, h, i: (b * nq + o_tile(i), h))]
    out_shape = [jax.ShapeDtypeStruct((n_seq * seq, H_DIFF * DV_DIFF), BF16)]
    if latent:
        ck, cv = cache
        cos, sin = rope
        in_specs += [pl.BlockSpec((None, None, 2, PAST_LEN, DH_DIFF), lambda b, h, i: (b, h, 0, 0, 0)),
                     pl.BlockSpec((None, None, PAST_LEN, DV_DIFF), lambda b, h, i: (b, h, 0, 0)),
                     pl.BlockSpec((tq, 128), lambda b, h, i: (q_tile(i), 0)),
                     pl.BlockSpec((tq, 128), lambda b, h, i: (q_tile(i), 0)),
                     pl.BlockSpec((seq, 128), lambda b, h, i: (0, 0)),
                     pl.BlockSpec((seq, 128), lambda b, h, i: (0, 0))]
        args += [ck, cv, cos, sin, cos, sin]
        n_steps = n_seq * H_DIFF * nq
        w_args, w_in, w_out, w_shape = _weight_cast_plan(
            cast, n_steps // 2, lambda b, h, i: ((b * H_DIFF + h) * nq + i) // 2)
        args += w_args
        in_specs += w_in
        out_specs += w_out
        out_shape += w_shape
        scratch = [pltpu.VMEM((2, seq, 128), BF16), pltpu.VMEM((seq, DV_DIFF), BF16),
                   pltpu.VMEM((2, PAST_LEN, 128), BF16), pltpu.VMEM((PAST_LEN, DV_DIFF), BF16)]
        body, steps = _diff_latent_body, nq
    else:
        out_specs += [pl.BlockSpec((None, None, 2, seq, DH_DIFF), lambda b, h, i: (b, h, 0, 0, 0)),
                      pl.BlockSpec((None, None, seq, DV_DIFF), lambda b, h, i: (b, h, 0, 0))]
        out_shape += [jax.ShapeDtypeStruct((n_seq, H_DIFF, 2, seq, DH_DIFF), F32),
                      jax.ShapeDtypeStruct((n_seq, H_DIFF, seq, DV_DIFF), F32)]
        body, steps = _diff_ctx_body, nq
    return pl.pallas_call(
        functools.partial(body, lam_init=lam_init),
        grid=(n_seq, H_DIFF, steps),
        in_specs=in_specs,
        out_specs=out_specs,
        out_shape=out_shape,
        scratch_shapes=scratch,
        compiler_params=_cparams(3),
        name="diff_attention",
    )(*args)


SWA_TQ = 256
SWA_WIN = SWA_TQ + 2 * WINDOW
NEG = -1e30


def _swa_body(*refs, latent, seq):
    tq = SWA_TQ
    if latent:
        (q_ref, k_ref, v_ref, sink_ref, ck_ref, cv_ref, cq_ref, sq_ref, ckk_ref, skk_ref,
         wo_ref, w1_ref, w2_ref, o_ref, wob_ref, w1b_ref, w2b_ref, kr_ref, cvt_ref) = refs
        _cast_weight_chunks((wo_ref, w1_ref, w2_ref), (wob_ref, w1b_ref, w2b_ref))
        qi = pl.program_id(1)

        @pl.when(qi == 0)
        def _():
            kr_ref[...] = _rope128(k_ref[...], ckk_ref[...], skk_ref[...]).astype(BF16)
            cvt_ref[...] = jnp.concatenate([cv_ref[0], cv_ref[1]], axis=1).T.astype(BF16)

        cq, sq = cq_ref[...], sq_ref[...]
        q = jnp.concatenate([_rope128(q_ref[:, g * 128:(g + 1) * 128], cq, sq)
                             for g in range(H_SWA * DH_SWA // 128)], axis=1)
        ws = pl.multiple_of(jnp.clip(qi * tq - WINDOW, 0, seq - SWA_WIN), 128)
        kw = kr_ref[pl.ds(ws, SWA_WIN), :]
        vwt = v_ref[pl.ds(ws, SWA_WIN), :].T.astype(BF16)
        kpos = ws + lax.broadcasted_iota(jnp.int32, (SWA_WIN, tq), 0)
        qpos = qi * tq + lax.broadcasted_iota(jnp.int32, (SWA_WIN, tq), 1)
        bias1 = jnp.where(jnp.abs(qpos - kpos) <= WINDOW, 0.0, NEG)
        bias = jnp.concatenate([bias1] * SWA_GROUP, axis=1)
    else:
        q_ref, k_ref, v_ref, sink_ref, o_ref, kc_ref, vc_ref = refs
        q = q_ref[...]
        kk = k_ref[...]
        vv = v_ref[...]
        for kv in range(KV_SWA):
            kc_ref[kv] = kk[:, kv * DH_SWA:(kv + 1) * DH_SWA]
            vc_ref[kv] = vv[:, kv * DH_SWA:(kv + 1) * DH_SWA]
        kw = kk.astype(BF16)
        vwt = vv.T.astype(BF16)
    q = q * (DH_SWA ** -0.5 * LOG2E)
    lanes = [slice(kv * DH_SWA, (kv + 1) * DH_SWA) for kv in range(KV_SWA)]

    def logits(kv):
        q4 = jnp.concatenate([q[:, (kv * SWA_GROUP + g) * DH_SWA:(kv * SWA_GROUP + g + 1) * DH_SWA]
                              for g in range(SWA_GROUP)], axis=0).astype(BF16)
        s_loc = _dot_nt(kw[:, lanes[kv]], q4)
        if latent:
            return [_dot_nt(ck_ref[kv].astype(BF16), q4), s_loc + bias]
        return [s_loc]

    def values(kv, es):
        vals_t = [cvt_ref[lanes[kv], :], vwt[lanes[kv], :]] if latent else [vwt[lanes[kv], :]]
        acc = None
        for e, vt in zip(es, vals_t):
            t = _dot(vt, e.astype(BF16))
            acc = t if acc is None else acc + t
        return acc

    parts = logits(0)
    accs = []
    for kv in range(KV_SWA):
        nxt = logits(kv + 1) if kv + 1 < KV_SWA else None
        es, inv = _col_softmax(parts, extra=sink_ref[kv] * LOG2E)
        accs.append((es, inv))
        parts = nxt
    pieces = []
    for kv, (es, inv) in enumerate(accs):
        acc = values(kv, es) * inv
        pieces += [acc[:, g * tq:(g + 1) * tq] for g in range(SWA_GROUP)]
    o_ref[...] = jnp.concatenate(pieces, axis=0).T.astype(BF16)


def _swa_attention(proj, sink, *, n_seq, seq, cache=None, rope=None, cast=None):
    latent = cache is not None
    tq = SWA_TQ
    nq = seq // tq
    hq = H_SWA * DH_SWA
    sink_b = jnp.repeat(sink.reshape(KV_SWA, SWA_GROUP), tq, axis=1).reshape(KV_SWA, 1, SWA_GROUP * tq)
    in_specs = [pl.BlockSpec((tq, hq), lambda b, i: (b * nq + i, 0)),
                pl.BlockSpec((seq, 128), lambda b, i: (b, O_SK // 128)),
                pl.BlockSpec((seq, 128), lambda b, i: (b, O_SV // 128)),
                pl.BlockSpec((KV_SWA, 1, SWA_GROUP * tq), lambda b, i: (0, 0, 0))]
    args = [proj, proj, proj, sink_b]
    scratch = []
    out_specs = [pl.BlockSpec((tq, hq), lambda b, i: (b * nq + i, 0))]
    out_shape = [jax.ShapeDtypeStruct((n_seq * seq, hq), BF16)]
    if latent:
        ck, cv = cache
        cos, sin = rope
        in_specs += [pl.BlockSpec((None, KV_SWA, PAST_LEN, DH_SWA), lambda b, i: (b, 0, 0, 0)),
                     pl.BlockSpec((None, KV_SWA, PAST_LEN, DH_SWA), lambda b, i: (b, 0, 0, 0)),
                     pl.BlockSpec((tq, 128), lambda b, i: (i, 0)),
                     pl.BlockSpec((tq, 128), lambda b, i: (i, 0)),
                     pl.BlockSpec((seq, 128), lambda b, i: (0, 0)),
                     pl.BlockSpec((seq, 128), lambda b, i: (0, 0))]
        args += [ck, cv, cos, sin, cos, sin]
        w_args, w_in, w_out, w_shape = _weight_cast_plan(cast, n_seq * nq, lambda b, i: b * nq + i)
        args += w_args
        in_specs += w_in
        out_specs += w_out
        out_shape += w_shape
        scratch = [pltpu.VMEM((seq, 128), BF16), pltpu.VMEM((KV_SWA * DH_SWA, PAST_LEN), BF16)]
    else:
        cache_spec = pl.BlockSpec((None, KV_SWA, seq, DH_SWA), lambda b, i: (b, 0, 0, 0))
        cache_shape = jax.ShapeDtypeStruct((n_seq, KV_SWA, seq, DH_SWA), F32)
        out_specs += [cache_spec, cache_spec]
        out_shape += [cache_shape, cache_shape]
    return pl.pallas_call(
        functools.partial(_swa_body, latent=latent, seq=seq),
        grid=(n_seq, nq),
        in_specs=in_specs,
        out_specs=out_specs,
        out_shape=out_shape,
        scratch_shapes=scratch,
        compiler_params=_cparams(2),
        name="swa_attention",
    )(*args)


GLA_BLOCK = 128
GLA_LEVELS = 7
GLA_SAFE = 60.0
GLA_GROUP = 4
GLA_GATE_ROWS = 1024


def _split3(x):
    hi = x.astype(BF16)
    r1 = x - hi.astype(F32)
    mid = r1.astype(BF16)
    lo = (r1 - mid.astype(F32)).astype(BF16)
    return hi, mid, lo


def _gla_gate_body(r_ref, w2_ref, b2_ref, x_ref, tot_ref):
    T = GLA_BLOCK
    ti = lax.broadcasted_iota(jnp.int32, (T, T), 0)
    si = lax.broadcasted_iota(jnp.int32, (T, T), 1)
    r_hi, r_mid, _ = _split3(r_ref[...])
    for d in range(2):
        w_hi, w_mid, _ = _split3(w2_ref[d])
        z = b2_ref[d] + _dot(r_hi, w_hi) + _dot(r_hi, w_mid) + _dot(r_mid, w_hi)
        la = (jnp.minimum(z, 0.0) - jnp.log(1.0 + jnp.exp(-jnp.abs(z)))) * (1.0 / GLA_TAU)
        tri = jnp.where((si >= ti) if d else (si <= ti), 1.0, 0.0).astype(BF16)
        l_hi, l_mid, l_lo = _split3(la)
        for j in range(GLA_GATE_ROWS // T):
            rows = slice(j * T, (j + 1) * T)
            x = _dot(tri, l_hi[rows]) + _dot(tri, l_mid[rows]) + _dot(tri, l_lo[rows])
            x_ref[d, rows, :] = x
            tot_ref[d, j:j + 1, :] = x[0:1, :] if d else x[T - 1:T, :]


def _gla_gates(proj, w2, b2):
    n_rows = proj.shape[0]
    hk = H_GLA * DK_GLA
    tr = GLA_GATE_ROWS
    w2p = jnp.zeros((2, 128, hk), F32)
    w2p = w2p.at[0, 0:GLA_RANK].set(w2[0]).at[1, GLA_RANK:2 * GLA_RANK].set(w2[1])
    return pl.pallas_call(
        _gla_gate_body,
        grid=(n_rows // tr,),
        in_specs=[pl.BlockSpec((tr, 128), lambda i: (i, O_GLR // 128)),
                  pl.BlockSpec((2, 128, hk), lambda i: (0, 0, 0)),
                  pl.BlockSpec((2, 1, hk), lambda i: (0, 0, 0))],
        out_specs=[pl.BlockSpec((2, tr, hk), lambda i: (0, i, 0)),
                   pl.BlockSpec((2, tr // GLA_BLOCK, hk), lambda i: (0, i, 0))],
        out_shape=[jax.ShapeDtypeStruct((2, n_rows, hk), F32),
                   jax.ShapeDtypeStruct((2, n_rows // GLA_BLOCK, hk), F32)],
        compiler_params=_cparams(1),
        name="gla_gates",
    )(proj, w2p, b2.reshape(2, 1, hk))


def _gla_body(*refs, rev, n_blocks, finish):
    if finish:
        (flag_ref, q_ref, k_ref, v_ref, x_ref, s0_ref, of_ref, g_ref, o_ref, sfin_ref, st_ref, att_ref) = refs
    else:
        flag_ref, q_ref, k_ref, v_ref, x_ref, s0_ref, o_ref, sfin_ref, st_ref, att_ref = refs
    n = pl.program_id(1)
    blk = (n_blocks - 1 - n) if rev else n
    safe = flag_ref[pl.program_id(0) * n_blocks + blk] != 0
    T = GLA_BLOCK
    G = GLA_GROUP
    hk = H_GLA * DK_GLA
    heads = [slice(h * DK_GLA, (h + 1) * DK_GLA) for h in range(H_GLA)]
    scale = DK_GLA ** -0.5
    ti = lax.broadcasted_iota(jnp.int32, (T, T), 0)
    si = lax.broadcasted_iota(jnp.int32, (T, T), 1)
    causal = (si >= ti) if rev else (si <= ti)

    @pl.when(n == 0)
    def _():
        st_ref[...] = s0_ref[...]

    @pl.when(safe)
    def _():
        for j in range(G):
            x = x_ref[j]
            qs = (q_ref[j] * scale * jnp.exp(x)).astype(BF16)
            ks = (k_ref[j] * jnp.exp(-x)).astype(BF16)
            for h, hs in enumerate(heads):
                att_ref[j, h] = jnp.where(causal, _dot_nt(qs[:, hs], ks[:, hs]), 0.0)

    @pl.when(jnp.logical_not(safe))
    def _():
        t_idx = lax.broadcasted_iota(jnp.int32, (T, hk), 0)
        xr = ti ^ si
        for j in range(G):
            q = q_ref[j] * scale
            k = k_ref[j]
            x = x_ref[j]
            if rev:
                la = jnp.where(t_idx == T - 1, x, x - pltpu.roll(x, T - 1, 0))
            else:
                la = jnp.where(t_idx == 0, x, x - pltpu.roll(x, 1, 0))
            qb, kb = q.astype(BF16), k.astype(BF16)
            att = [_dot_nt(qb[:, hs], kb[:, hs]) for hs in heads]
            xg = la
            tg = la
            for lvl in range(GLA_LEVELS):
                sz = 1 << lvl
                upper = ((t_idx >> lvl) & 1) == 1
                is_q = jnp.logical_not(upper) if rev else upper
                partner = jnp.where(upper, pltpu.roll(tg, sz, 0), pltpu.roll(tg, T - sz, 0))
                e = jnp.exp(jnp.where(is_q, xg, tg - xg))
                qs = jnp.where(is_q, q * e, 0.0).astype(BF16)
                ks = jnp.where(is_q, 0.0, k * e).astype(BF16)
                for h, hs in enumerate(heads):
                    att[h] = jnp.where(xr >= sz, _dot_nt(qs[:, hs], ks[:, hs]), att[h])
                xg = xg + jnp.where(is_q, partner, 0.0)
                tg = tg + partner
            for h in range(H_GLA):
                att_ref[j, h] = att[h]

    for j in range(G):
        x = x_ref[j]
        tot = x[0:1, :] if rev else x[T - 1:T, :]
        v = v_ref[j]
        qe = (q_ref[j] * scale * jnp.exp(x)).astype(BF16)
        kw = (k_ref[j] * jnp.exp(tot - x)).astype(BF16)
        vt = v.T
        vb = v.astype(BF16)
        dec = jnp.exp(tot)
        for h, hs in enumerate(heads):
            vs = slice(h * DV_GLA, (h + 1) * DV_GLA)
            st = st_ref[j, h]
            o = _dot(att_ref[j, h].astype(BF16), vb[:, vs]) + _dot_nt(qe[:, hs], st.astype(BF16))
            if finish:
                o = o + of_ref[j, :, vs]
                o_ref[j, :, vs] = (_norm_rows(o) * _silu(g_ref[j, :, vs])).astype(BF16)
            else:
                o_ref[j, :, vs] = o
            st_ref[j, h] = st * dec[:, hs] + _dot(vt[vs, :].astype(BF16), kw[:, hs])

    @pl.when(n == n_blocks - 1)
    def _():
        sfin_ref[...] = st_ref[...]


def _gla(proj, w2, b2, s0t, *, n_seq, seq):
    nb = seq // GLA_BLOCK
    G = GLA_GROUP
    hk, hv = H_GLA * DK_GLA, H_GLA * DV_GLA
    x, tot = _gla_gates(proj, w2, b2)
    x = x.reshape(2, n_seq, seq, hk)
    safe = (tot.min(axis=-1) > -GLA_SAFE).reshape(2, n_seq // G, G, nb).all(axis=2)
    flags = safe.astype(jnp.int32).reshape(2, (n_seq // G) * nb)
    p3 = proj.reshape(n_seq, seq, ODD_IN_PAD)
    state_shape = jax.ShapeDtypeStruct((n_seq, H_GLA, DV_GLA, DK_GLA), F32)
    state_spec = pl.BlockSpec((G, H_GLA, DV_GLA, DK_GLA), lambda g, n, f: (g, 0, 0, 0))
    scratch = [pltpu.VMEM((G, H_GLA, DV_GLA, DK_GLA), F32), pltpu.VMEM((G, H_GLA, GLA_BLOCK, GLA_BLOCK), F32)]

    def run(rev, extra_args, extra_cols, out_dtype):
        blk = (lambda n: nb - 1 - n) if rev else (lambda n: n)
        d = 1 if rev else 0
        tok = lambda width, col: pl.BlockSpec((G, GLA_BLOCK, width), lambda g, n, f: (g, blk(n), col))
        grid_spec = pltpu.PrefetchScalarGridSpec(
            num_scalar_prefetch=1,
            grid=(n_seq // G, nb),
            in_specs=[tok(hk, O_GQ // hk), tok(hk, O_GK // hk), tok(hv, O_GV // hv),
                      pl.BlockSpec((None, G, GLA_BLOCK, hk), lambda g, n, f: (d, g, blk(n), 0)),
                      pl.BlockSpec((None, G, H_GLA, DV_GLA, DK_GLA), lambda g, n, f: (d, g, 0, 0, 0))]
            + [tok(hv, col) for col in extra_cols],
            out_specs=[tok(hv, 0), state_spec],
            scratch_shapes=scratch)
        return pl.pallas_call(
            functools.partial(_gla_body, rev=rev, n_blocks=nb, finish=rev),
            grid_spec=grid_spec,
            out_shape=[jax.ShapeDtypeStruct((n_seq, seq, hv), out_dtype), state_shape],
            compiler_params=_cparams(2),
            name="gla_bwd" if rev else "gla_fwd",
        )(flags[d], p3, p3, p3, x, s0t, *extra_args)

    o_f, s_f = run(False, [], [], F32)
    out, s_b = run(True, [o_f, p3], [0, O_GR // hv], BF16)
    return out.reshape(n_seq * seq, hv), jnp.stack([s_f, s_b], axis=0)


def kernel(x_prompt, x_sample, c, state_ret, cache_diff_k, cache_diff_v, cache_swa_k, cache_swa_v, state_gla,
           c_ctx, w_mod, b_mod, ln_g, ln_b, w_in_even, w_out_even, ret_decay, diff_lam, w_in_odd, w_out_odd,
           swa_sink, gla_w2, gla_b, w_ff1, w_ff2):
    xc = x_prompt.reshape(N_CTX, D_MODEL)
    xl = x_sample.reshape(N_LAT, D_MODEL)
    cond = jnp.concatenate([c_ctx[None, :], c, jnp.zeros((N_COND - 1 - DEC_BATCH, D_MODEL), F32)], axis=0)
    mod = _modulation(cond, w_mod, b_mod).reshape(DEPTH, N_COND, 1, 6 * D_MODEL)
    rope = _rope_tables(DEC_SEQ // GRID_W, DH_DIFF)

    w_in_even_b = w_in_even.astype(BF16)
    w_in_odd_b = _reorder_odd(w_in_odd.astype(BF16))

    ctx = dict(n_seq=BATCH, seq=SEQ)
    lat = dict(n_seq=DEC_BATCH, seq=DEC_SEQ)
    new_ret, new_dk, new_dv, new_sk, new_sv, new_gla = [], [], [], [], [], []
    for l in range(DEPTH):
        if l % 2 == 0:
            e = l // 2
            lam_init = 0.8 - 0.6 * math.exp(-0.3 * l)
            pc = _in_projection(xc, mod[l], w_in_even_b, e, False)
            pl_ = _in_projection(xl, mod[l], w_in_even_b, e, True)
            zero = jnp.zeros((BATCH, 2, H_RET, DK_RET, DV_RET), F32)
            a_ctx, s_ret = _retention(pc, ret_decay[e], zero, **ctx)
            a_lat, _ = _retention(pl_, ret_decay[e], state_ret[:, e], **lat)
            b_ctx, k_new, v_new = _diff_attention(pc, diff_lam[e], lam_init, **ctx)
            new_dk.append(k_new)
            new_dv.append(v_new)
            b_lat, *w_post = _diff_attention(pl_, diff_lam[e], lam_init,
                                             cache=(cache_diff_k[:, e], cache_diff_v[:, e]), rope=rope,
                                             cast=[(w_out_even, e), (w_ff1, l), (w_ff2, l)], **lat)
            new_ret.append(s_ret)
        else:
            o = l // 2
            pc = _in_projection(xc, mod[l], w_in_odd_b, o, False)
            pl_ = _in_projection(xl, mod[l], w_in_odd_b, o, True)
            a_ctx, k_new, v_new = _swa_attention(pc, swa_sink[o], **ctx)
            new_sk.append(k_new)
            new_sv.append(v_new)
            a_lat, *w_post = _swa_attention(pl_, swa_sink[o], cache=(cache_swa_k[:, o], cache_swa_v[:, o]),
                                            rope=rope, cast=[(w_out_odd, o), (w_ff1, l), (w_ff2, l)], **lat)
            zero = jnp.zeros((2, BATCH, H_GLA, DV_GLA, DK_GLA), F32)
            b_ctx, s_gla = _gla(pc, gla_w2[o], gla_b[o], zero, **ctx)
            s0t = state_gla[:, o].transpose(1, 0, 2, 4, 3)
            b_lat, _ = _gla(pl_, gla_w2[o], gla_b[o], s0t, **lat)
            new_gla.append(s_gla.transpose(1, 0, 2, 4, 3))
        w_out_b, w1_b, w2_b = w_post
        xc = _post_mixer(xc, a_ctx, b_ctx, mod[l], w_out_b, ln_g, ln_b, w1_b, w2_b, l, False)
        xl = _post_mixer(xl, a_lat, b_lat, mod[l], w_out_b, ln_g, ln_b, w1_b, w2_b, l, True)
    return (xc.reshape(BATCH, SEQ, D_MODEL), xl.reshape(DEC_BATCH, DEC_SEQ, D_MODEL),
            jnp.stack(new_ret, axis=1), jnp.stack(new_dk, axis=1), jnp.stack(new_dv, axis=1),
            jnp.stack(new_sk, axis=1), jnp.stack(new_sv, axis=1), jnp.stack(new_gla, axis=1))
```

```python
import functools
import math

import numpy as np
import jax
import jax.numpy as jnp
from jax import lax
from jax.experimental import pallas as pl
from jax.experimental.pallas import tpu as pltpu

F32 = jnp.float32
BF16 = jnp.bfloat16

D_MODEL = 1024
BATCH = 16
SEQ = 256
DEPTH = 4
DEC_BATCH = 4
DEC_SEQ = 2048
PAST_LEN = 256
GRID_W = 64
H_RET, DK_RET, DV_RET = 4, 64, 128
H_DIFF, DH_DIFF, DV_DIFF = 4, 64, 128
H_SWA, KV_SWA, DH_SWA = 8, 2, 64
SWA_GROUP = H_SWA // KV_SWA
WINDOW = 128
H_GLA, DK_GLA, DV_GLA = 4, 64, 128
GLA_RANK = 16
GLA_TAU = 16.0
D_FF = 4 * D_MODEL
ROPE_BASE = 10000.0
N_EVEN = (DEPTH + 1) // 2
N_ODD = DEPTH // 2
ALPHA = (2 * DEPTH) ** 0.25
EVEN_IN = 3072
ODD_IN = 2336
ODD_IN_PAD = 2432
EPS = 1e-5
LOG2E = 1.4426950408889634

E_RQ, E_RK, E_RV, E_RG, E_DQ, E_DK, E_DV = 0, 256, 512, 1024, 1536, 2048, 2560
O_SQ, O_GV, O_GR, O_GQ, O_GK, O_SK, O_SV, O_GLR = 0, 512, 1024, 1536, 1792, 2048, 2176, 2304


def _reorder_odd(w):
    sq, sk, sv, gq, gk, gv, gr, glr = jnp.split(w, [512, 640, 768, 1024, 1280, 1792, 2304], axis=-1)
    pad = jnp.zeros(w.shape[:-1] + (ODD_IN_PAD - ODD_IN,), w.dtype)
    return jnp.concatenate([sq, gv, gr, gq, gk, sk, sv, glr, pad], axis=-1)


N_CTX = BATCH * SEQ
N_LAT = DEC_BATCH * DEC_SEQ
ROW_TILE = 256
DENSE_TILE = 512
POST_TILE = 1024
N_COND = 8

VMEM_LIMIT = 56 * 1024 * 1024


def _cparams(n_axes):
    return pltpu.CompilerParams(dimension_semantics=("arbitrary",) * n_axes,
                                vmem_limit_bytes=VMEM_LIMIT)


def _dot(a, b):
    return jnp.dot(a, b, preferred_element_type=F32)


def _dot_nt(a, b):
    return lax.dot_general(a, b, (((1,), (1,)), ((), ())), preferred_element_type=F32)


def _silu(x):
    return x * (1.0 / (1.0 + jnp.exp(-x)))


def _norm_rows(x):
    mu = jnp.mean(x, axis=-1, keepdims=True)
    xc = x - mu
    var = jnp.mean(xc * xc, axis=-1, keepdims=True)
    return xc * lax.rsqrt(var + EPS)


def _mod_row(i, latent, tile):
    return 1 + i // (DEC_SEQ // tile) if latent else 0


def _resident(shape, index_map):
    return pl.BlockSpec(shape, index_map, pipeline_mode=pl.Buffered(1))


def _mod_body(c_ref, w_ref, b_ref, o_ref):
    c = _silu(c_ref[...]).astype(BF16)
    o_ref[...] = _dot(c, w_ref[...].astype(BF16)) + b_ref[...]


def _modulation(cond, w_mod, b_mod):
    tn = 1536
    return pl.pallas_call(
        _mod_body,
        grid=(DEPTH, 6 * D_MODEL // tn),
        in_specs=[pl.BlockSpec((N_COND, D_MODEL), lambda l, j: (0, 0)),
                  pl.BlockSpec((None, D_MODEL, tn), lambda l, j: (l, 0, j)),
                  pl.BlockSpec((None, 1, tn), lambda l, j: (l, 0, j))],
        out_specs=pl.BlockSpec((None, N_COND, tn), lambda l, j: (l, 0, j)),
        out_shape=jax.ShapeDtypeStruct((DEPTH, N_COND, 6 * D_MODEL), F32),
        compiler_params=_cparams(2),
        name="modulation",
    )(cond, w_mod, b_mod.reshape(DEPTH, 1, 6 * D_MODEL))


def _inproj_body(x_ref, mod_ref, w_ref, o_ref):
    sh = mod_ref[:, 0:D_MODEL]
    sc = mod_ref[:, D_MODEL:2 * D_MODEL]
    h = (x_ref[...] * (1.0 + sc) + sh).astype(BF16)
    o_ref[...] = _dot(h, w_ref[...])


def _in_projection(x, mod_l, w_stack, idx, latent):
    n_in = w_stack.shape[2]
    n_rows = x.shape[0]
    tm = DENSE_TILE
    return pl.pallas_call(
        _inproj_body,
        grid=(n_rows // tm,),
        in_specs=[pl.BlockSpec((tm, D_MODEL), lambda i: (i, 0)),
                  pl.BlockSpec((None, 1, 6 * D_MODEL), lambda i: (_mod_row(i, latent, tm), 0, 0)),
                  _resident((None, D_MODEL, n_in), lambda i: (idx, 0, 0))],
        out_specs=pl.BlockSpec((tm, n_in), lambda i: (i, 0)),
        out_shape=jax.ShapeDtypeStruct((n_rows, n_in), F32),
        compiler_params=_cparams(1),
        name="in_projection",
    )(x, mod_l, w_stack)


def _layer_norm(x, g, b):
    return _norm_rows(x) * g + b


def _post_body(x_ref, ma_ref, mb_ref, mod_ref, wo_ref, g_ref, b_ref, w1_ref, w2_ref, o_ref):
    half = D_MODEL // 2
    gt1 = mod_ref[:, 2 * D_MODEL:3 * D_MODEL]
    sh2 = mod_ref[:, 3 * D_MODEL:4 * D_MODEL]
    sc2 = mod_ref[:, 4 * D_MODEL:5 * D_MODEL]
    gt2 = mod_ref[:, 5 * D_MODEL:6 * D_MODEL]
    chunk = 1024
    n_chunks = D_FF // chunk
    top, bot = slice(0, POST_TILE // 2), slice(POST_TILE // 2, POST_TILE)

    def mix(r):
        return _dot(ma_ref[r, :], wo_ref[0:half, :]) + _dot(mb_ref[r, :], wo_ref[half:D_MODEL, :])

    def norm1(r, m):
        x1 = _layer_norm(ALPHA * x_ref[r, :] + gt1 * m, g_ref[0:1, :], b_ref[0:1, :])
        return x1, (x1 * (1.0 + sc2) + sh2).astype(BF16)

    def ffn(hf, j):
        h1 = _dot(hf, w1_ref[:, j * chunk:(j + 1) * chunk])
        h1 = jnp.square(jnp.maximum(h1, 0.0)).astype(BF16)
        return _dot(h1, w2_ref[j * chunk:(j + 1) * chunk, :])

    def norm2(r, x1, ff):
        o_ref[r, :] = _layer_norm(ALPHA * x1 + gt2 * ff, g_ref[1:2, :], b_ref[1:2, :])

    m_top, m_bot = mix(top), mix(bot)
    x1_top, hf_top = norm1(top, m_top)
    ff_top = ffn(hf_top, 0)
    x1_bot, hf_bot = norm1(bot, m_bot)
    for j in range(1, n_chunks):
        ff_top = ff_top + ffn(hf_top, j)
    ff_bot = ffn(hf_bot, 0)
    norm2(top, x1_top, ff_top)
    for j in range(1, n_chunks):
        ff_bot = ff_bot + ffn(hf_bot, j)
    norm2(bot, x1_bot, ff_bot)


def _post_mixer(x, mix_a, mix_b, mod_l, w_out, ln_g, ln_b, w1, w2, layer, latent):
    half = D_MODEL // 2
    n_rows = x.shape[0]
    tm = POST_TILE
    return pl.pallas_call(
        _post_body,
        grid=(n_rows // tm,),
        in_specs=[pl.BlockSpec((tm, D_MODEL), lambda i: (i, 0)),
                  pl.BlockSpec((tm, half), lambda i: (i, 0)),
                  pl.BlockSpec((tm, half), lambda i: (i, 0)),
                  pl.BlockSpec((None, 1, 6 * D_MODEL), lambda i: (_mod_row(i, latent, tm), 0, 0)),
                  _resident((D_MODEL, D_MODEL), lambda i: (0, 0)),
                  _resident((None, 2, D_MODEL), lambda i: (layer, 0, 0)),
                  _resident((None, 2, D_MODEL), lambda i: (layer, 0, 0)),
                  _resident((D_MODEL, D_FF), lambda i: (0, 0)),
                  _resident((D_FF, D_MODEL), lambda i: (0, 0))],
        out_specs=pl.BlockSpec((tm, D_MODEL), lambda i: (i, 0)),
        out_shape=jax.ShapeDtypeStruct((n_rows, D_MODEL), F32),
        compiler_params=_cparams(1),
        name="post_mixer",
    )(x, mix_a, mix_b, mod_l, w_out, ln_g, ln_b, w1, w2)


def _rope_tables(rows, dim):
    row = jnp.repeat(jnp.arange(rows), GRID_W).astype(F32)
    col = jnp.tile(jnp.arange(GRID_W), rows).astype(F32)
    half = dim // 2
    freqs = ROPE_BASE ** (-jnp.arange(0, half, 2, dtype=F32) / half)
    ar = row[:, None] * freqs
    ac = col[:, None] * freqs
    cr, sr, cc, sc = jnp.cos(ar), jnp.sin(ar), jnp.cos(ac), jnp.sin(ac)
    cos = jnp.concatenate([cr, cr, cc, cc], axis=-1)
    sin = jnp.concatenate([-sr, sr, -sc, sc], axis=-1)
    reps = 128 // dim
    return jnp.tile(cos, (1, reps)), jnp.tile(sin, (1, reps))


def _rope128(x, cos, sin):
    lane = lax.broadcasted_iota(jnp.int32, x.shape, 1)
    first = (lane & 31) < 16
    swapped = jnp.where(first, pltpu.roll(x, 112, 1), pltpu.roll(x, 16, 1))
    return x * cos + swapped * sin


RET_CHUNK = 256


def _ret_body(q_ref, k_ref, v_ref, g_ref, rdk_ref, rdh_ref, s0_ref, o_ref, sfin_ref, *, seq):
    C = RET_CHUNK
    n_chunks = seq // C
    scale = DK_RET ** -0.5
    ii = lax.broadcasted_iota(jnp.int32, (C, C), 0)
    jj = lax.broadcasted_iota(jnp.int32, (C, C), 1)
    dist = (ii - jj).astype(F32)
    adist = jnp.abs(dist)
    pos = lax.broadcasted_iota(jnp.int32, (C, H_RET * DK_RET), 0).astype(F32)
    lgf = -jnp.exp(rdk_ref[0])
    lgb = -jnp.exp(rdk_ref[1])
    k_wf = jnp.exp(lgf * (C - 1.0 - pos)) * scale
    k_wb = jnp.exp(lgb * pos) * scale
    q_wf = jnp.exp(lgf * (pos + 1.0))
    q_wb = jnp.exp(lgb * (C - pos))

    uf, ub = [], []
    for n in range(n_chunks):
        kn = k_ref[n * C:(n + 1) * C, :]
        kft = (kn * k_wf).T.astype(BF16)
        kbt = (kn * k_wb).T.astype(BF16)
        ufn, ubn = [], []
        for h in range(H_RET):
            vh = v_ref[n * C:(n + 1) * C, h * DV_RET:(h + 1) * DV_RET].astype(BF16)
            ufn.append(_dot(kft[h * DK_RET:(h + 1) * DK_RET, :], vh))
            ubn.append(_dot(kbt[h * DK_RET:(h + 1) * DK_RET, :], vh))
        uf.append(ufn)
        ub.append(ubn)

    for h in range(H_RET):
        lgf_h = -jnp.exp(rdh_ref[0, h])
        lgb_h = -jnp.exp(rdh_ref[1, h])
        cf = jnp.exp(lgf_h * float(C))[:, 0:DV_RET]
        cb = jnp.exp(lgb_h * float(C))[:, 0:DV_RET]
        dmat = jnp.exp(jnp.where(dist > 0, lgf_h, lgb_h) * adist) + jnp.where(dist == 0, 1.0, 0.0)
        sf = [s0_ref[0, h]]
        for n in range(n_chunks):
            sf.append(cf * sf[n] + uf[n][h])
        sb = [None] * n_chunks
        sb[n_chunks - 1] = s0_ref[1, h]
        for n in range(n_chunks - 1, 0, -1):
            sb[n - 1] = cb * sb[n] + ub[n][h]
        sfin_ref[0, h] = sf[n_chunks]
        sfin_ref[1, h] = cb * sb[0] + ub[0][h]
        lanes = slice(h * DK_RET, (h + 1) * DK_RET)
        for n in range(n_chunks):
            rows = slice(n * C, (n + 1) * C)
            qh = q_ref[rows, lanes]
            kh = (k_ref[rows, lanes] * scale).astype(BF16)
            vh = v_ref[rows, h * DV_RET:(h + 1) * DV_RET].astype(BF16)
            s = _dot_nt(qh.astype(BF16), kh) * dmat
            o = _dot(s.astype(BF16), vh)
            o = o + _dot((qh * q_wf[:, lanes]).astype(BF16), sf[n].astype(BF16))
            o = o + _dot((qh * q_wb[:, lanes]).astype(BF16), sb[n].astype(BF16))
            gate = _silu(g_ref[rows, h * DV_RET:(h + 1) * DV_RET])
            o_ref[rows, h * DV_RET:(h + 1) * DV_RET] = (_norm_rows(o) * gate).astype(BF16)


def _retention(proj, ret_decay_e, s0, *, n_seq, seq):
    rdk = jnp.repeat(ret_decay_e, DK_RET, axis=-1).reshape(2, 1, H_RET * DK_RET)
    rdh = jnp.broadcast_to(ret_decay_e[:, :, None, None], (2, H_RET, 1, RET_CHUNK))
    hk, hv = H_RET * DK_RET, H_RET * DV_RET
    return pl.pallas_call(
        functools.partial(_ret_body, seq=seq),
        grid=(n_seq,),
        in_specs=[pl.BlockSpec((seq, hk), lambda b: (b, E_RQ // hk)),
                  pl.BlockSpec((seq, hk), lambda b: (b, E_RK // hk)),
                  pl.BlockSpec((seq, hv), lambda b: (b, E_RV // hv)),
                  pl.BlockSpec((seq, hv), lambda b: (b, E_RG // hv)),
                  pl.BlockSpec((2, 1, hk), lambda b: (0, 0, 0)),
                  pl.BlockSpec((2, H_RET, 1, RET_CHUNK), lambda b: (0, 0, 0, 0)),
                  pl.BlockSpec((None, 2, H_RET, DK_RET, DV_RET), lambda b: (b, 0, 0, 0, 0))],
        out_specs=[pl.BlockSpec((seq, hv), lambda b: (b, 0)),
                   pl.BlockSpec((None, 2, H_RET, DK_RET, DV_RET), lambda b: (b, 0, 0, 0, 0))],
        out_shape=[jax.ShapeDtypeStruct((n_seq * seq, hv), BF16),
                   jax.ShapeDtypeStruct((n_seq, 2, H_RET, DK_RET, DV_RET), F32)],
        compiler_params=_cparams(1),
        name="retention",
    )(proj, proj, proj, proj, rdk, rdh, s0)


def _cast_weight_chunks(src_refs, dst_refs):
    for src, dst in zip(src_refs, dst_refs):
        dst[...] = src[...].astype(BF16)


def _weight_cast_plan(weights, n_chunks, chunk_of):
    args, in_specs, out_specs, out_shapes = [], [], [], []
    for w, idx in weights:
        rows, cols = w.shape[1], w.shape[2]
        rb = rows // n_chunks
        assert rb * n_chunks == rows and rb % 16 == 0
        args.append(w)
        in_specs.append(pl.BlockSpec((None, rb, cols), lambda *g, idx=idx: (idx, chunk_of(*g), 0)))
        out_specs.append(pl.BlockSpec((rb, cols), lambda *g: (chunk_of(*g), 0)))
        out_shapes.append(jax.ShapeDtypeStruct((rows, cols), BF16))
    return args, in_specs, out_specs, out_shapes


def _col_softmax(parts, extra=None):
    m = parts[0].max(axis=0, keepdims=True)
    for p in parts[1:]:
        m = jnp.maximum(m, p.max(axis=0, keepdims=True))
    if extra is not None:
        m = jnp.maximum(m, extra)
    es = [jnp.exp2(p - m) for p in parts]
    den = es[0].sum(axis=0, keepdims=True)
    for e in es[1:]:
        den = den + e.sum(axis=0, keepdims=True)
    if extra is not None:
        den = den + jnp.exp2(extra - m)
    return es, 1.0 / den


DIFF_TQ = 256
DIFF_CHUNK = 256


def _diff_lambda(p, lam_init):
    a = jnp.sum(p[0:1, :] * p[1:2, :], axis=-1, keepdims=True)
    b = jnp.sum(p[2:3, :] * p[3:4, :], axis=-1, keepdims=True)
    return jnp.exp(a) - jnp.exp(b) + lam_init


def _split_maps(k):
    lane = lax.broadcasted_iota(jnp.int32, k.shape, 1)
    return (jnp.where(lane < DH_DIFF, k, 0.0).astype(BF16), jnp.where(lane >= DH_DIFF, k, 0.0).astype(BF16))


def _diff_latent_body(lam_ref, q_ref, k_ref, v_ref, ck_ref, cv_ref, cq_ref, sq_ref, ckk_ref, skk_ref,
                      wo_ref, w1_ref, w2_ref, o_ref, wob_ref, w1b_ref, w2b_ref,
                      km_ref, vb_ref, ckm_ref, cvb_ref, *, lam_init):
    i = pl.program_id(1)
    _cast_weight_chunks((wo_ref, w1_ref, w2_ref), (wob_ref, w1b_ref, w2b_ref))
    heads = [slice(h * 128, (h + 1) * 128) for h in range(H_DIFF)]

    @pl.when(i == 0)
    def _():
        ckk, skk = ckk_ref[...], skk_ref[...]
        for h, hs in enumerate(heads):
            k1, k2 = _split_maps(_rope128(k_ref[:, hs], ckk, skk))
            km_ref[h, 0] = k1
            km_ref[h, 1] = k2
            vb_ref[h] = v_ref[:, hs].astype(BF16)
            c1, c2 = _split_maps(jnp.concatenate([ck_ref[h, 0], ck_ref[h, 1]], axis=1))
            ckm_ref[h, 0] = c1
            ckm_ref[h, 1] = c2
            cvb_ref[h] = cv_ref[h].astype(BF16)

    lam = _diff_lambda(lam_ref[...], lam_init)
    cq, sq = cq_ref[...], sq_ref[...]
    qbs = [(_rope128(q_ref[:, hs], cq, sq) * (DH_DIFF ** -0.5 * LOG2E)).astype(BF16) for hs in heads]
    ck = DIFF_CHUNK
    n_chunks = (PAST_LEN + km_ref.shape[2]) // ck
    own = lambda c: slice(c * ck - PAST_LEN, (c + 1) * ck - PAST_LEN)
    keys = lambda h, m, c: ckm_ref[h, m] if c == 0 else km_ref[h, m, own(c), :]
    vals = lambda h, c: cvb_ref[h] if c == 0 else vb_ref[h, own(c), :]

    def row_max(chunks):
        m = chunks[0]
        for s in chunks[1:]:
            m = jnp.maximum(m, s)
        return m.max(axis=-1, keepdims=True)

    def add(a, b):
        return b if a is None else a + b

    maps = [(h, m) for h in range(H_DIFF) for m in range(2)]
    logits, exps, psum, acc = {}, {}, {}, {}
    for stage in range(len(maps) + 2):
        p_l = maps[stage] if stage < len(maps) else None
        p_e = maps[stage - 1] if 0 <= stage - 1 < len(maps) else None
        p_v = maps[stage - 2] if 0 <= stage - 2 < len(maps) else None
        if p_l is not None:
            logits[p_l] = []
        if p_e is not None:
            mx = row_max(logits[p_e])
            exps[p_e], psum[p_e] = [], None
        if p_v is not None:
            acc[p_v] = None
        for c in range(n_chunks):
            if p_l is not None:
                logits[p_l].append(_dot_nt(qbs[p_l[0]], keys(p_l[0], p_l[1], c)))
            if p_e is not None:
                e = jnp.exp2(logits[p_e][c] - mx)
                psum[p_e] = add(psum[p_e], e)
                exps[p_e].append(e.astype(BF16))
            if p_v is not None:
                acc[p_v] = add(acc[p_v], _dot(exps[p_v][c], vals(p_v[0], c)))
        if p_e is not None:
            del logits[p_e]
        if p_v is not None:
            del exps[p_v]
    for h, hs in enumerate(heads):
        inv1 = 1.0 / psum[h, 0].sum(axis=-1, keepdims=True)
        inv2 = 1.0 / psum[h, 1].sum(axis=-1, keepdims=True)
        o = acc[h, 0] * inv1 - acc[h, 1] * (inv2 * lam)
        o_ref[:, hs] = (_norm_rows(o) * (1.0 - lam_init)).astype(BF16)


def _diff_ctx_body(lam_ref, q_ref, k_ref, v_ref, *rest, lam_init, n_prev):
    o_ref, kc_ref, vc_ref = rest[-3:]
    kc_new, vc_new = kc_ref, vc_ref
    if n_prev:
        kp_ref, vp_ref = rest[0:2]
        for e in range(n_prev):
            kc_ref[e] = kp_ref[e] if n_prev > 1 else kp_ref[...]
            vc_ref[e] = vp_ref[e] if n_prev > 1 else vp_ref[...]
        kc_new, vc_new = kc_ref.at[n_prev], vc_ref.at[n_prev]
    lam = _diff_lambda(lam_ref[...], lam_init)
    heads = [slice(h * 128, (h + 1) * 128) for h in range(H_DIFF)]
    qbs, kms, vbs = [], [], []
    for h, hs in enumerate(heads):
        kk = k_ref[:, hs]
        vv = v_ref[:, hs]
        kc_new[h, 0] = kk[:, 0:DH_DIFF]
        kc_new[h, 1] = kk[:, DH_DIFF:2 * DH_DIFF]
        vc_new[h] = vv
        kms.append(_split_maps(kk))
        vbs.append(vv.astype(BF16))
        qbs.append((q_ref[:, hs] * (DH_DIFF ** -0.5 * LOG2E)).astype(BF16))
    pairs = [(h, m) for h in range(H_DIFF) for m in range(2)]
    logits = {p: _dot_nt(qbs[p[0]], kms[p[0]][p[1]]) for p in pairs}
    exps = {p: jnp.exp2(logits[p] - logits[p].max(axis=-1, keepdims=True)) for p in pairs}
    invs = {p: 1.0 / exps[p].sum(axis=-1, keepdims=True) for p in pairs}
    accs = {p: _dot(exps[p].astype(BF16), vbs[p[0]]) for p in pairs}
    for h, hs in enumerate(heads):
        o = accs[h, 0] * invs[h, 0] - accs[h, 1] * (invs[h, 1] * lam)
        o_ref[:, hs] = (_norm_rows(o) * (1.0 - lam_init)).astype(BF16)


def _diff_attention_ctx(proj, lam_p, lam_init, *, n_seq, seq, prev=None):
    hw = H_DIFF * 128
    n_prev = 0 if prev is None else (1 if prev[0].ndim == 5 else prev[0].shape[1])
    one_k, one_v = (H_DIFF, 2, seq, DH_DIFF), (H_DIFF, seq, DV_DIFF)
    zeros = lambda n: (0,) * n
    in_specs = [pl.BlockSpec((4, DH_DIFF), lambda b: (0, 0)),
                pl.BlockSpec((seq, hw), lambda b: (b, E_DQ // hw)),
                pl.BlockSpec((seq, hw), lambda b: (b, E_DK // hw)),
                pl.BlockSpec((seq, hw), lambda b: (b, E_DV // hw))]
    args = [lam_p, proj, proj, proj]
    if n_prev:
        lead = (None,) if n_prev == 1 else (None, n_prev)
        in_specs += [pl.BlockSpec(lead + one_k, lambda b: (b,) + zeros(len(lead) + 3)),
                     pl.BlockSpec(lead + one_v, lambda b: (b,) + zeros(len(lead) + 2))]
        args += list(prev)
    lead_out = (None, n_prev + 1) if n_prev else (None,)
    out_specs = [pl.BlockSpec((seq, hw), lambda b: (b, 0)),
                 pl.BlockSpec(lead_out + one_k, lambda b: (b,) + zeros(len(lead_out) + 3)),
                 pl.BlockSpec(lead_out + one_v, lambda b: (b,) + zeros(len(lead_out) + 2))]
    out_shape = [jax.ShapeDtypeStruct((n_seq * seq, hw), BF16),
                 jax.ShapeDtypeStruct((n_seq,) + lead_out[1:] + one_k, F32),
                 jax.ShapeDtypeStruct((n_seq,) + lead_out[1:] + one_v, F32)]
    return pl.pallas_call(
        functools.partial(_diff_ctx_body, lam_init=lam_init, n_prev=n_prev),
        grid=(n_seq,),
        in_specs=in_specs,
        out_specs=out_specs,
        out_shape=out_shape,
        compiler_params=_cparams(1),
        name="diff_attention_ctx",
    )(*args)


def _diff_attention_latent(proj, lam_p, lam_init, *, n_seq, seq, cache, rope, cast):
    tq = DIFF_TQ
    nq = seq // tq
    hw = H_DIFF * 128
    ck, cv = cache
    cos, sin = rope
    in_specs = [pl.BlockSpec((4, DH_DIFF), lambda b, i: (0, 0)),
                pl.BlockSpec((tq, hw), lambda b, i: (b * nq + i, E_DQ // hw)),
                pl.BlockSpec((seq, hw), lambda b, i: (b, E_DK // hw)),
                pl.BlockSpec((seq, hw), lambda b, i: (b, E_DV // hw)),
                pl.BlockSpec((None, H_DIFF, 2, PAST_LEN, DH_DIFF), lambda b, i: (b, 0, 0, 0, 0)),
                pl.BlockSpec((None, H_DIFF, PAST_LEN, DV_DIFF), lambda b, i: (b, 0, 0, 0)),
                pl.BlockSpec((tq, 128), lambda b, i: (i, 0)),
                pl.BlockSpec((tq, 128), lambda b, i: (i, 0)),
                pl.BlockSpec((seq, 128), lambda b, i: (0, 0)),
                pl.BlockSpec((seq, 128), lambda b, i: (0, 0))]
    args = [lam_p, proj, proj, proj, ck, cv, cos, sin, cos, sin]
    out_specs = [pl.BlockSpec((tq, hw), lambda b, i: (b * nq + i, 0))]
    out_shape = [jax.ShapeDtypeStruct((n_seq * seq, hw), BF16)]
    w_args, w_in, w_out, w_shape = _weight_cast_plan(cast, n_seq * nq, lambda b, i: b * nq + i)
    scratch = [pltpu.VMEM((H_DIFF, 2, seq, 128), BF16), pltpu.VMEM((H_DIFF, seq, DV_DIFF), BF16),
               pltpu.VMEM((H_DIFF, 2, PAST_LEN, 128), BF16), pltpu.VMEM((H_DIFF, PAST_LEN, DV_DIFF), BF16)]
    return pl.pallas_call(
        functools.partial(_diff_latent_body, lam_init=lam_init),
        grid=(n_seq, nq),
        in_specs=in_specs + w_in,
        out_specs=out_specs + w_out,
        out_shape=out_shape + w_shape,
        scratch_shapes=scratch,
        compiler_params=_cparams(2),
        name="diff_attention",
    )(*args, *w_args)


SWA_TQ = 256
SWA_WIN = SWA_TQ + 2 * WINDOW
NEG = -1e30


def _swa_body(*refs, latent, seq, n_prev=0):
    tq = SWA_TQ
    if latent:
        (q_ref, k_ref, v_ref, sink_ref, ck_ref, cv_ref, cq_ref, sq_ref, ckk_ref, skk_ref,
         wo_ref, w1_ref, w2_ref, o_ref, wob_ref, w1b_ref, w2b_ref, kr_ref, cvt_ref) = refs
        _cast_weight_chunks((wo_ref, w1_ref, w2_ref), (wob_ref, w1b_ref, w2b_ref))
        qi = pl.program_id(1)

        @pl.when(qi == 0)
        def _():
            kr_ref[...] = _rope128(k_ref[...], ckk_ref[...], skk_ref[...]).astype(BF16)
            cvt_ref[...] = jnp.concatenate([cv_ref[0], cv_ref[1]], axis=1).T.astype(BF16)

        cq, sq = cq_ref[...], sq_ref[...]
        q = jnp.concatenate([_rope128(q_ref[:, g * 128:(g + 1) * 128], cq, sq)
                             for g in range(H_SWA * DH_SWA // 128)], axis=1)
        ws = pl.multiple_of(jnp.clip(qi * tq - WINDOW, 0, seq - SWA_WIN), 128)
        kw = kr_ref[pl.ds(ws, SWA_WIN), :]
        vwt = v_ref[pl.ds(ws, SWA_WIN), :].T.astype(BF16)
        kpos = ws + lax.broadcasted_iota(jnp.int32, (SWA_WIN, tq), 0)
        qpos = qi * tq + lax.broadcasted_iota(jnp.int32, (SWA_WIN, tq), 1)
        bias1 = jnp.where(jnp.abs(qpos - kpos) <= WINDOW, 0.0, NEG)
        bias = jnp.concatenate([bias1] * SWA_GROUP, axis=1)
    else:
        q_ref, k_ref, v_ref, sink_ref = refs[:4]
        o_ref, kc_ref, vc_ref = refs[-3:]
        q = q_ref[...]
        kk = k_ref[...]
        vv = v_ref[...]
        kc_new, vc_new = kc_ref, vc_ref
        if n_prev:
            kp_ref, vp_ref = refs[4:6]
            for e in range(n_prev):
                kc_ref[e] = kp_ref[e] if n_prev > 1 else kp_ref[...]
                vc_ref[e] = vp_ref[e] if n_prev > 1 else vp_ref[...]
            kc_new, vc_new = kc_ref.at[n_prev], vc_ref.at[n_prev]
        for kv in range(KV_SWA):
            kc_new[kv] = kk[:, kv * DH_SWA:(kv + 1) * DH_SWA]
            vc_new[kv] = vv[:, kv * DH_SWA:(kv + 1) * DH_SWA]
        kw = kk.astype(BF16)
        vwt = vv.T.astype(BF16)
    q = q * (DH_SWA ** -0.5 * LOG2E)
    lanes = [slice(kv * DH_SWA, (kv + 1) * DH_SWA) for kv in range(KV_SWA)]

    def logits(kv):
        q4 = jnp.concatenate([q[:, (kv * SWA_GROUP + g) * DH_SWA:(kv * SWA_GROUP + g + 1) * DH_SWA]
                              for g in range(SWA_GROUP)], axis=0).astype(BF16)
        s_loc = _dot_nt(kw[:, lanes[kv]], q4)
        if latent:
            return [_dot_nt(ck_ref[kv].astype(BF16), q4), s_loc + bias]
        return [s_loc]

    def values(kv, es):
        vals_t = [cvt_ref[lanes[kv], :], vwt[lanes[kv], :]] if latent else [vwt[lanes[kv], :]]
        acc = None
        for e, vt in zip(es, vals_t):
            t = _dot(vt, e.astype(BF16))
            acc = t if acc is None else acc + t
        return acc

    parts = logits(0)
    accs = []
    for kv in range(KV_SWA):
        nxt = logits(kv + 1) if kv + 1 < KV_SWA else None
        es, inv = _col_softmax(parts, extra=sink_ref[kv] * LOG2E)
        accs.append((es, inv))
        parts = nxt
    pieces = []
    for kv, (es, inv) in enumerate(accs):
        acc = values(kv, es) * inv
        pieces += [acc[:, g * tq:(g + 1) * tq] for g in range(SWA_GROUP)]
    o_ref[...] = jnp.concatenate(pieces, axis=0).T.astype(BF16)


def _swa_attention(proj, sink, *, n_seq, seq, cache=None, rope=None, cast=None, prev=None):
    latent = cache is not None
    n_prev = 0
    tq = SWA_TQ
    nq = seq // tq
    hq = H_SWA * DH_SWA
    sink_b = jnp.repeat(sink.reshape(KV_SWA, SWA_GROUP), tq, axis=1).reshape(KV_SWA, 1, SWA_GROUP * tq)
    in_specs = [pl.BlockSpec((tq, hq), lambda b, i: (b * nq + i, 0)),
                pl.BlockSpec((seq, 128), lambda b, i: (b, O_SK // 128)),
                pl.BlockSpec((seq, 128), lambda b, i: (b, O_SV // 128)),
                pl.BlockSpec((KV_SWA, 1, SWA_GROUP * tq), lambda b, i: (0, 0, 0))]
    args = [proj, proj, proj, sink_b]
    scratch = []
    out_specs = [pl.BlockSpec((tq, hq), lambda b, i: (b * nq + i, 0))]
    out_shape = [jax.ShapeDtypeStruct((n_seq * seq, hq), BF16)]
    if latent:
        ck, cv = cache
        cos, sin = rope
        in_specs += [pl.BlockSpec((None, KV_SWA, PAST_LEN, DH_SWA), lambda b, i: (b, 0, 0, 0)),
                     pl.BlockSpec((None, KV_SWA, PAST_LEN, DH_SWA), lambda b, i: (b, 0, 0, 0)),
                     pl.BlockSpec((tq, 128), lambda b, i: (i, 0)),
                     pl.BlockSpec((tq, 128), lambda b, i: (i, 0)),
                     pl.BlockSpec((seq, 128), lambda b, i: (0, 0)),
                     pl.BlockSpec((seq, 128), lambda b, i: (0, 0))]
        args += [ck, cv, cos, sin, cos, sin]
        w_args, w_in, w_out, w_shape = _weight_cast_plan(cast, n_seq * nq, lambda b, i: b * nq + i)
        args += w_args
        in_specs += w_in
        out_specs += w_out
        out_shape += w_shape
        scratch = [pltpu.VMEM((seq, 128), BF16), pltpu.VMEM((KV_SWA * DH_SWA, PAST_LEN), BF16)]
    else:
        n_prev = 0 if prev is None else (1 if prev[0].ndim == 4 else prev[0].shape[1])
        one = (KV_SWA, seq, DH_SWA)
        if n_prev:
            lead = (None,) if n_prev == 1 else (None, n_prev)
            in_specs += [pl.BlockSpec(lead + one, lambda b, i: (b,) + (0,) * (len(lead) + 2))] * 2
            args += list(prev)
        lead_out = (None, n_prev + 1) if n_prev else (None,)
        cache_spec = pl.BlockSpec(lead_out + one, lambda b, i: (b,) + (0,) * (len(lead_out) + 2))
        cache_shape = jax.ShapeDtypeStruct((n_seq,) + lead_out[1:] + one, F32)
        out_specs += [cache_spec, cache_spec]
        out_shape += [cache_shape, cache_shape]
    return pl.pallas_call(
        functools.partial(_swa_body, latent=latent, seq=seq, n_prev=n_prev),
        grid=(n_seq, nq),
        in_specs=in_specs,
        out_specs=out_specs,
        out_shape=out_shape,
        scratch_shapes=scratch,
        compiler_params=_cparams(2),
        name="swa_attention",
    )(*args)


GLA_BLOCK = 128
GLA_LEVELS = 7
GLA_SAFE = 60.0
GLA_GROUP = 4
GLA_GATE_ROWS = 1024


def _split3(x):
    hi = x.astype(BF16)
    r1 = x - hi.astype(F32)
    mid = r1.astype(BF16)
    lo = (r1 - mid.astype(F32)).astype(BF16)
    return hi, mid, lo


def _gla_gate_body(r_ref, w2_ref, b2_ref, x_ref, tot_ref):
    T = GLA_BLOCK
    ti = lax.broadcasted_iota(jnp.int32, (T, T), 0)
    si = lax.broadcasted_iota(jnp.int32, (T, T), 1)
    r_hi, r_mid, _ = _split3(r_ref[...])
    for d in range(2):
        w_hi, w_mid, _ = _split3(w2_ref[d])
        z = b2_ref[d] + _dot(r_hi, w_hi) + _dot(r_hi, w_mid) + _dot(r_mid, w_hi)
        la = (jnp.minimum(z, 0.0) - jnp.log(1.0 + jnp.exp(-jnp.abs(z)))) * (1.0 / GLA_TAU)
        tri = jnp.where((si >= ti) if d else (si <= ti), 1.0, 0.0).astype(BF16)
        l_hi, l_mid, l_lo = _split3(la)
        for j in range(GLA_GATE_ROWS // T):
            rows = slice(j * T, (j + 1) * T)
            x = _dot(tri, l_hi[rows]) + _dot(tri, l_mid[rows]) + _dot(tri, l_lo[rows])
            x_ref[d, rows, :] = x
            tot_ref[d, j:j + 1, :] = x[0:1, :] if d else x[T - 1:T, :]


def _gla_gates(proj, w2, b2):
    n_rows = proj.shape[0]
    hk = H_GLA * DK_GLA
    tr = GLA_GATE_ROWS
    w2p = jnp.zeros((2, 128, hk), F32)
    w2p = w2p.at[0, 0:GLA_RANK].set(w2[0]).at[1, GLA_RANK:2 * GLA_RANK].set(w2[1])
    return pl.pallas_call(
        _gla_gate_body,
        grid=(n_rows // tr,),
        in_specs=[pl.BlockSpec((tr, 128), lambda i: (i, O_GLR // 128)),
                  pl.BlockSpec((2, 128, hk), lambda i: (0, 0, 0)),
                  pl.BlockSpec((2, 1, hk), lambda i: (0, 0, 0))],
        out_specs=[pl.BlockSpec((2, tr, hk), lambda i: (0, i, 0)),
                   pl.BlockSpec((2, tr // GLA_BLOCK, hk), lambda i: (0, i, 0))],
        out_shape=[jax.ShapeDtypeStruct((2, n_rows, hk), F32),
                   jax.ShapeDtypeStruct((2, n_rows // GLA_BLOCK, hk), F32)],
        compiler_params=_cparams(1),
        name="gla_gates",
    )(proj, w2p, b2.reshape(2, 1, hk))


def _gla_body(*refs, rev, n_blocks, finish):
    if finish:
        (flag_ref, q_ref, k_ref, v_ref, x_ref, s0_ref, of_ref, g_ref, o_ref, sfin_ref, st_ref, att_ref) = refs
    else:
        flag_ref, q_ref, k_ref, v_ref, x_ref, s0_ref, o_ref, sfin_ref, st_ref, att_ref = refs
    n = pl.program_id(1)
    blk = (n_blocks - 1 - n) if rev else n
    safe = flag_ref[pl.program_id(0) * n_blocks + blk] != 0
    T = GLA_BLOCK
    G = GLA_GROUP
    hk = H_GLA * DK_GLA
    heads = [slice(h * DK_GLA, (h + 1) * DK_GLA) for h in range(H_GLA)]
    scale = DK_GLA ** -0.5
    ti = lax.broadcasted_iota(jnp.int32, (T, T), 0)
    si = lax.broadcasted_iota(jnp.int32, (T, T), 1)
    causal = (si >= ti) if rev else (si <= ti)

    @pl.when(n == 0)
    def _():
        st_ref[...] = s0_ref[...]

    @pl.when(safe)
    def _():
        for j in range(G):
            x = x_ref[j]
            qs = (q_ref[j] * scale * jnp.exp(x)).astype(BF16)
            ks = (k_ref[j] * jnp.exp(-x)).astype(BF16)
            for h, hs in enumerate(heads):
                att_ref[j, h] = jnp.where(causal, _dot_nt(qs[:, hs], ks[:, hs]), 0.0)

    @pl.when(jnp.logical_not(safe))
    def _():
        t_idx = lax.broadcasted_iota(jnp.int32, (T, hk), 0)
        xr = ti ^ si
        for j in range(G):
            q = q_ref[j] * scale
            k = k_ref[j]
            x = x_ref[j]
            if rev:
                la = jnp.where(t_idx == T - 1, x, x - pltpu.roll(x, T - 1, 0))
            else:
                la = jnp.where(t_idx == 0, x, x - pltpu.roll(x, 1, 0))
            qb, kb = q.astype(BF16), k.astype(BF16)
            att = [_dot_nt(qb[:, hs], kb[:, hs]) for hs in heads]
            xg = la
            tg = la
            for lvl in range(GLA_LEVELS):
                sz = 1 << lvl
                upper = ((t_idx >> lvl) & 1) == 1
                is_q = jnp.logical_not(upper) if rev else upper
                partner = jnp.where(upper, pltpu.roll(tg, sz, 0), pltpu.roll(tg, T - sz, 0))
                e = jnp.exp(jnp.where(is_q, xg, tg - xg))
                qs = jnp.where(is_q, q * e, 0.0).astype(BF16)
                ks = jnp.where(is_q, 0.0, k * e).astype(BF16)
                for h, hs in enumerate(heads):
                    att[h] = jnp.where(xr >= sz, _dot_nt(qs[:, hs], ks[:, hs]), att[h])
                xg = xg + jnp.where(is_q, partner, 0.0)
                tg = tg + partner
            for h in range(H_GLA):
                att_ref[j, h] = att[h]

    for j in range(G):
        x = x_ref[j]
        tot = x[0:1, :] if rev else x[T - 1:T, :]
        v = v_ref[j]
        qe = (q_ref[j] * scale * jnp.exp(x)).astype(BF16)
        kw = (k_ref[j] * jnp.exp(tot - x)).astype(BF16)
        vt = v.T
        vb = v.astype(BF16)
        dec = jnp.exp(tot)
        for h, hs in enumerate(heads):
            vs = slice(h * DV_GLA, (h + 1) * DV_GLA)
            st = st_ref[j, h]
            o = _dot(att_ref[j, h].astype(BF16), vb[:, vs]) + _dot_nt(qe[:, hs], st.astype(BF16))
            if finish:
                o = o + of_ref[j, :, vs]
                o_ref[j, :, vs] = (_norm_rows(o) * _silu(g_ref[j, :, vs])).astype(BF16)
            else:
                o_ref[j, :, vs] = o
            st_ref[j, h] = st * dec[:, hs] + _dot(vt[vs, :].astype(BF16), kw[:, hs])

    @pl.when(n == n_blocks - 1)
    def _():
        sfin_ref[...] = st_ref[...]


def _gla(proj, w2, b2, s0t, *, n_seq, seq):
    nb = seq // GLA_BLOCK
    G = GLA_GROUP
    hk, hv = H_GLA * DK_GLA, H_GLA * DV_GLA
    x, tot = _gla_gates(proj, w2, b2)
    x = x.reshape(2, n_seq, seq, hk)
    safe = (tot.min(axis=-1) > -GLA_SAFE).reshape(2, n_seq // G, G, nb).all(axis=2)
    flags = safe.astype(jnp.int32).reshape(2, (n_seq // G) * nb)
    p3 = proj.reshape(n_seq, seq, ODD_IN_PAD)
    state_shape = jax.ShapeDtypeStruct((n_seq, H_GLA, DV_GLA, DK_GLA), F32)
    state_spec = pl.BlockSpec((G, H_GLA, DV_GLA, DK_GLA), lambda g, n, f: (g, 0, 0, 0))
    scratch = [pltpu.VMEM((G, H_GLA, DV_GLA, DK_GLA), F32), pltpu.VMEM((G, H_GLA, GLA_BLOCK, GLA_BLOCK), F32)]

    def run(rev, extra_args, extra_cols, out_dtype):
        blk = (lambda n: nb - 1 - n) if rev else (lambda n: n)
        d = 1 if rev else 0
        tok = lambda width, col: pl.BlockSpec((G, GLA_BLOCK, width), lambda g, n, f: (g, blk(n), col))
        grid_spec = pltpu.PrefetchScalarGridSpec(
            num_scalar_prefetch=1,
            grid=(n_seq // G, nb),
            in_specs=[tok(hk, O_GQ // hk), tok(hk, O_GK // hk), tok(hv, O_GV // hv),
                      pl.BlockSpec((None, G, GLA_BLOCK, hk), lambda g, n, f: (d, g, blk(n), 0)),
                      pl.BlockSpec((None, G, H_GLA, DV_GLA, DK_GLA), lambda g, n, f: (d, g, 0, 0, 0))]
            + [tok(hv, col) for col in extra_cols],
            out_specs=[tok(hv, 0), state_spec],
            scratch_shapes=scratch)
        return pl.pallas_call(
            functools.partial(_gla_body, rev=rev, n_blocks=nb, finish=rev),
            grid_spec=grid_spec,
            out_shape=[jax.ShapeDtypeStruct((n_seq, seq, hv), out_dtype), state_shape],
            compiler_params=_cparams(2),
            name="gla_bwd" if rev else "gla_fwd",
        )(flags[d], p3, p3, p3, x, s0t, *extra_args)

    o_f, s_f = run(False, [], [], F32)
    out, s_b = run(True, [o_f, p3], [0, O_GR // hv], BF16)
    return out.reshape(n_seq * seq, hv), jnp.stack([s_f, s_b], axis=0)


def _stacked(caches, n_layers):
    return [c[:, None] for c in caches] if n_layers == 1 else caches


def kernel(x_prompt, x_sample, c, state_ret, cache_diff_k, cache_diff_v, cache_swa_k, cache_swa_v, state_gla,
           c_ctx, w_mod, b_mod, ln_g, ln_b, w_in_even, w_out_even, ret_decay, diff_lam, w_in_odd, w_out_odd,
           swa_sink, gla_w2, gla_b, w_ff1, w_ff2):
    xc = x_prompt.reshape(N_CTX, D_MODEL)
    xl = x_sample.reshape(N_LAT, D_MODEL)
    cond = jnp.concatenate([c_ctx[None, :], c, jnp.zeros((N_COND - 1 - DEC_BATCH, D_MODEL), F32)], axis=0)
    mod = _modulation(cond, w_mod, b_mod).reshape(DEPTH, N_COND, 1, 6 * D_MODEL)
    rope = _rope_tables(DEC_SEQ // GRID_W, DH_DIFF)

    w_in_even_b = w_in_even.astype(BF16)
    w_in_odd_b = _reorder_odd(w_in_odd.astype(BF16))

    ctx = dict(n_seq=BATCH, seq=SEQ)
    lat = dict(n_seq=DEC_BATCH, seq=DEC_SEQ)
    new_ret, new_gla = [], []
    new_diff = new_swa = None
    for l in range(DEPTH):
        if l % 2 == 0:
            e = l // 2
            lam_init = 0.8 - 0.6 * math.exp(-0.3 * l)
            pc = _in_projection(xc, mod[l], w_in_even_b, e, False)
            pl_ = _in_projection(xl, mod[l], w_in_even_b, e, True)
            zero = jnp.zeros((BATCH, 2, H_RET, DK_RET, DV_RET), F32)
            a_ctx, s_ret = _retention(pc, ret_decay[e], zero, **ctx)
            a_lat, _ = _retention(pl_, ret_decay[e], state_ret[:, e], **lat)
            b_ctx, *new_diff = _diff_attention_ctx(pc, diff_lam[e], lam_init, prev=new_diff, **ctx)
            b_lat, *w_post = _diff_attention_latent(pl_, diff_lam[e], lam_init,
                                                    cache=(cache_diff_k[:, e], cache_diff_v[:, e]), rope=rope,
                                                    cast=[(w_out_even, e), (w_ff1, l), (w_ff2, l)], **lat)
            new_ret.append(s_ret)
        else:
            o = l // 2
            pc = _in_projection(xc, mod[l], w_in_odd_b, o, False)
            pl_ = _in_projection(xl, mod[l], w_in_odd_b, o, True)
            a_ctx, *new_swa = _swa_attention(pc, swa_sink[o], prev=new_swa, **ctx)
            a_lat, *w_post = _swa_attention(pl_, swa_sink[o], cache=(cache_swa_k[:, o], cache_swa_v[:, o]),
                                            rope=rope, cast=[(w_out_odd, o), (w_ff1, l), (w_ff2, l)], **lat)
            zero = jnp.zeros((2, BATCH, H_GLA, DV_GLA, DK_GLA), F32)
            b_ctx, s_gla = _gla(pc, gla_w2[o], gla_b[o], zero, **ctx)
            s0t = state_gla[:, o].transpose(1, 0, 2, 4, 3)
            b_lat, _ = _gla(pl_, gla_w2[o], gla_b[o], s0t, **lat)
            new_gla.append(s_gla.transpose(1, 0, 2, 4, 3))
        w_out_b, w1_b, w2_b = w_post
        xc = _post_mixer(xc, a_ctx, b_ctx, mod[l], w_out_b, ln_g, ln_b, w1_b, w2_b, l, False)
        xl = _post_mixer(xl, a_lat, b_lat, mod[l], w_out_b, ln_g, ln_b, w1_b, w2_b, l, True)
    return (xc.reshape(BATCH, SEQ, D_MODEL), xl.reshape(DEC_BATCH, DEC_SEQ, D_MODEL),
            jnp.stack(new_ret, axis=1), *_stacked(new_diff, N_EVEN), *_stacked(new_swa, N_ODD),
            jnp.stack(new_gla, axis=1))
```

```python
import functools
import math

import jax
import jax.numpy as jnp
from jax import lax
from jax.experimental import pallas as pl
from jax.experimental.pallas import tpu as pltpu

F32 = jnp.float32
BF16 = jnp.bfloat16

D_MODEL = 1024
BATCH = 16
SEQ = 256
DEPTH = 4
DEC_BATCH = 4
DEC_SEQ = 2048
PAST_LEN = 256
GRID_W = 64
H_RET, DK_RET, DV_RET = 4, 64, 128
H_DIFF, DH_DIFF, DV_DIFF = 4, 64, 128
H_SWA, KV_SWA, DH_SWA = 8, 2, 64
SWA_GROUP = H_SWA // KV_SWA
WINDOW = 128
H_GLA, DK_GLA, DV_GLA = 4, 64, 128
GLA_RANK = 16
GLA_TAU = 16.0
D_FF = 4 * D_MODEL
ROPE_BASE = 10000.0
N_EVEN = (DEPTH + 1) // 2
N_ODD = DEPTH // 2
ALPHA = (2 * DEPTH) ** 0.25
EVEN_IN = 3072
ODD_IN = 2336
ODD_IN_PAD = 2432
EPS = 1e-5
LOG2E = 1.4426950408889634

E_RQ, E_RK, E_RV, E_RG, E_DQ, E_DK, E_DV = 0, 256, 512, 1024, 1536, 2048, 2560
O_SQ, O_GV, O_GR, O_GQ, O_GK, O_SK, O_SV, O_GLR = 0, 512, 1024, 1536, 1792, 2048, 2176, 2304


def _reorder_odd(w):
    sq, sk, sv, gq, gk, gv, gr, glr = jnp.split(w, [512, 640, 768, 1024, 1280, 1792, 2304], axis=-1)
    pad = jnp.zeros(w.shape[:-1] + (ODD_IN_PAD - ODD_IN,), w.dtype)
    return jnp.concatenate([sq, gv, gr, gq, gk, sk, sv, glr, pad], axis=-1)


N_CTX = BATCH * SEQ
N_LAT = DEC_BATCH * DEC_SEQ
DENSE_TILE = 512
POST_TILE = 1024
N_COND = 8

VMEM_LIMIT = 56 * 1024 * 1024


def _cparams(n_axes):
    return pltpu.CompilerParams(dimension_semantics=("arbitrary",) * n_axes,
                                vmem_limit_bytes=VMEM_LIMIT)


def _dot(a, b):
    return jnp.dot(a, b, preferred_element_type=F32)


def _dot_nt(a, b):
    return lax.dot_general(a, b, (((1,), (1,)), ((), ())), preferred_element_type=F32)


def _silu(x):
    return x * (1.0 / (1.0 + jnp.exp(-x)))


def _norm_rows(x):
    mu = jnp.mean(x, axis=-1, keepdims=True)
    xc = x - mu
    var = jnp.mean(xc * xc, axis=-1, keepdims=True)
    return xc * lax.rsqrt(var + EPS)


def _norm_rows_many(xs):
    mus = [jnp.mean(x, axis=-1, keepdims=True) for x in xs]
    xcs = [x - mu for x, mu in zip(xs, mus)]
    vs = [jnp.mean(xc * xc, axis=-1, keepdims=True) for xc in xcs]
    return [xc * lax.rsqrt(v + EPS) for xc, v in zip(xcs, vs)]


def _mod_row(i, latent, tile):
    return 1 + i // (DEC_SEQ // tile) if latent else 0


def _resident(shape, index_map):
    return pl.BlockSpec(shape, index_map, pipeline_mode=pl.Buffered(1))


def _mod_body(c_ref, w_ref, b_ref, o_ref):
    c = _silu(c_ref[...]).astype(BF16)
    o_ref[...] = _dot(c, w_ref[...].astype(BF16)) + b_ref[...]


def _modulation(cond, w_mod, b_mod):
    tn = 1536
    return pl.pallas_call(
        _mod_body,
        grid=(DEPTH, 6 * D_MODEL // tn),
        in_specs=[pl.BlockSpec((N_COND, D_MODEL), lambda l, j: (0, 0)),
                  pl.BlockSpec((None, D_MODEL, tn), lambda l, j: (l, 0, j)),
                  pl.BlockSpec((None, 1, tn), lambda l, j: (l, 0, j))],
        out_specs=pl.BlockSpec((None, N_COND, tn), lambda l, j: (l, 0, j)),
        out_shape=jax.ShapeDtypeStruct((DEPTH, N_COND, 6 * D_MODEL), F32),
        compiler_params=_cparams(2),
        name="modulation",
    )(cond, w_mod, b_mod.reshape(DEPTH, 1, 6 * D_MODEL))


def _inproj_body(x_ref, mod_ref, w_ref, o_ref):
    sh = mod_ref[:, 0:D_MODEL]
    sc = mod_ref[:, D_MODEL:2 * D_MODEL]
    h = (x_ref[...] * (1.0 + sc) + sh).astype(BF16)
    o_ref[...] = _dot(h, w_ref[...])


def _in_projection(x, mod_l, w, latent):
    n_in = w.shape[1]
    n_rows = x.shape[0]
    tm = DENSE_TILE
    return pl.pallas_call(
        _inproj_body,
        grid=(n_rows // tm,),
        in_specs=[pl.BlockSpec((tm, D_MODEL), lambda i: (i, 0)),
                  pl.BlockSpec((None, 1, 6 * D_MODEL), lambda i: (_mod_row(i, latent, tm), 0, 0)),
                  _resident((D_MODEL, n_in), lambda i: (0, 0))],
        out_specs=pl.BlockSpec((tm, n_in), lambda i: (i, 0)),
        out_shape=jax.ShapeDtypeStruct((n_rows, n_in), F32),
        compiler_params=_cparams(1),
        name="in_projection",
    )(x, mod_l, w)


def _layer_norm(x, g, b):
    return _norm_rows(x) * g + b


def _post_body(x_ref, ma_ref, mb_ref, mod_ref, wo_ref, g_ref, b_ref, w1_ref, w2_ref, o_ref):
    half = D_MODEL // 2
    gt1 = mod_ref[:, 2 * D_MODEL:3 * D_MODEL]
    sh2 = mod_ref[:, 3 * D_MODEL:4 * D_MODEL]
    sc2 = mod_ref[:, 4 * D_MODEL:5 * D_MODEL]
    gt2 = mod_ref[:, 5 * D_MODEL:6 * D_MODEL]
    chunk = 1024
    n_chunks = D_FF // chunk
    top, bot = slice(0, POST_TILE // 2), slice(POST_TILE // 2, POST_TILE)

    def mix(r):
        return _dot(ma_ref[r, :], wo_ref[0:half, :]) + _dot(mb_ref[r, :], wo_ref[half:D_MODEL, :])

    def norm1(r, m):
        x1 = _layer_norm(ALPHA * x_ref[r, :] + gt1 * m, g_ref[0:1, :], b_ref[0:1, :])
        return x1, (x1 * (1.0 + sc2) + sh2).astype(BF16)

    def ffn(hf, j):
        h1 = _dot(hf, w1_ref[:, j * chunk:(j + 1) * chunk])
        h1 = jnp.square(jnp.maximum(h1, 0.0)).astype(BF16)
        return _dot(h1, w2_ref[j * chunk:(j + 1) * chunk, :])

    def norm2(r, x1, ff):
        o_ref[r, :] = _layer_norm(ALPHA * x1 + gt2 * ff, g_ref[1:2, :], b_ref[1:2, :])

    m_top, m_bot = mix(top), mix(bot)
    x1_top, hf_top = norm1(top, m_top)
    ff_top = ffn(hf_top, 0)
    x1_bot, hf_bot = norm1(bot, m_bot)
    for j in range(1, n_chunks):
        ff_top = ff_top + ffn(hf_top, j)
    ff_bot = ffn(hf_bot, 0)
    norm2(top, x1_top, ff_top)
    for j in range(1, n_chunks):
        ff_bot = ff_bot + ffn(hf_bot, j)
    norm2(bot, x1_bot, ff_bot)


def _post_mixer(x, mix_a, mix_b, mod_l, w_out, ln_g, ln_b, w1, w2, layer, latent):
    half = D_MODEL // 2
    n_rows = x.shape[0]
    tm = POST_TILE
    return pl.pallas_call(
        _post_body,
        grid=(n_rows // tm,),
        in_specs=[pl.BlockSpec((tm, D_MODEL), lambda i: (i, 0)),
                  pl.BlockSpec((tm, half), lambda i: (i, 0)),
                  pl.BlockSpec((tm, half), lambda i: (i, 0)),
                  pl.BlockSpec((None, 1, 6 * D_MODEL), lambda i: (_mod_row(i, latent, tm), 0, 0)),
                  _resident((D_MODEL, D_MODEL), lambda i: (0, 0)),
                  _resident((None, 2, D_MODEL), lambda i: (layer, 0, 0)),
                  _resident((None, 2, D_MODEL), lambda i: (layer, 0, 0)),
                  _resident((D_MODEL, D_FF), lambda i: (0, 0)),
                  _resident((D_FF, D_MODEL), lambda i: (0, 0))],
        out_specs=pl.BlockSpec((tm, D_MODEL), lambda i: (i, 0)),
        out_shape=jax.ShapeDtypeStruct((n_rows, D_MODEL), F32),
        compiler_params=_cparams(1),
        name="post_mixer",
    )(x, mix_a, mix_b, mod_l, w_out, ln_g, ln_b, w1, w2)


def _rope_tables(rows, dim):
    row = jnp.repeat(jnp.arange(rows), GRID_W).astype(F32)
    col = jnp.tile(jnp.arange(GRID_W), rows).astype(F32)
    half = dim // 2
    freqs = ROPE_BASE ** (-jnp.arange(0, half, 2, dtype=F32) / half)
    ar = row[:, None] * freqs
    ac = col[:, None] * freqs
    cr, sr, cc, sc = jnp.cos(ar), jnp.sin(ar), jnp.cos(ac), jnp.sin(ac)
    cos = jnp.concatenate([cr, cr, cc, cc], axis=-1)
    sin = jnp.concatenate([-sr, sr, -sc, sc], axis=-1)
    reps = 128 // dim
    return jnp.tile(cos, (1, reps)), jnp.tile(sin, (1, reps))


def _rope128(x, cos, sin):
    lane = lax.broadcasted_iota(jnp.int32, x.shape, 1)
    first = (lane & 31) < 16
    swapped = jnp.where(first, pltpu.roll(x, 112, 1), pltpu.roll(x, 16, 1))
    return x * cos + swapped * sin


RET_CHUNK = 256


def _ret_body(q_ref, k_ref, v_ref, g_ref, rdk_ref, rdh_ref, s0_ref, o_ref, sfin_ref, *, seq):
    C = RET_CHUNK
    n_chunks = seq // C
    scale = DK_RET ** -0.5
    ii = lax.broadcasted_iota(jnp.int32, (C, C), 0)
    jj = lax.broadcasted_iota(jnp.int32, (C, C), 1)
    dist = (ii - jj).astype(F32)
    adist = jnp.abs(dist)
    pos = lax.broadcasted_iota(jnp.int32, (C, H_RET * DK_RET), 0).astype(F32)
    lgf = -jnp.exp(rdk_ref[0])
    lgb = -jnp.exp(rdk_ref[1])
    k_wf = jnp.exp(lgf * (C - 1.0 - pos)) * scale
    k_wb = jnp.exp(lgb * pos) * scale
    q_wf = jnp.exp(lgf * (pos + 1.0))
    q_wb = jnp.exp(lgb * (C - pos))

    uf, ub = [], []
    for n in range(n_chunks):
        kn = k_ref[n * C:(n + 1) * C, :]
        kft = (kn * k_wf).T.astype(BF16)
        kbt = (kn * k_wb).T.astype(BF16)
        ufn, ubn = [], []
        for h in range(H_RET):
            vh = v_ref[n * C:(n + 1) * C, h * DV_RET:(h + 1) * DV_RET].astype(BF16)
            ufn.append(_dot(kft[h * DK_RET:(h + 1) * DK_RET, :], vh))
            ubn.append(_dot(kbt[h * DK_RET:(h + 1) * DK_RET, :], vh))
        uf.append(ufn)
        ub.append(ubn)

    for h in range(H_RET):
        lgf_h = -jnp.exp(rdh_ref[0, h])
        lgb_h = -jnp.exp(rdh_ref[1, h])
        cf = jnp.exp(lgf_h * float(C))[:, 0:DV_RET]
        cb = jnp.exp(lgb_h * float(C))[:, 0:DV_RET]
        dmat = jnp.exp(jnp.where(dist > 0, lgf_h, lgb_h) * adist) + jnp.where(dist == 0, 1.0, 0.0)
        sf = [s0_ref[0, h]]
        for n in range(n_chunks):
            sf.append(cf * sf[n] + uf[n][h])
        sb = [None] * n_chunks
        sb[n_chunks - 1] = s0_ref[1, h]
        for n in range(n_chunks - 1, 0, -1):
            sb[n - 1] = cb * sb[n] + ub[n][h]
        sfin_ref[0, h] = sf[n_chunks]
        sfin_ref[1, h] = cb * sb[0] + ub[0][h]
        lanes = slice(h * DK_RET, (h + 1) * DK_RET)
        vlanes = slice(h * DV_RET, (h + 1) * DV_RET)
        outs = []
        for n in range(n_chunks):
            rows = slice(n * C, (n + 1) * C)
            qh = q_ref[rows, lanes]
            kh = (k_ref[rows, lanes] * scale).astype(BF16)
            vh = v_ref[rows, vlanes].astype(BF16)
            s = _dot_nt(qh.astype(BF16), kh) * dmat
            o = _dot(s.astype(BF16), vh)
            o = o + _dot((qh * q_wf[:, lanes]).astype(BF16), sf[n].astype(BF16))
            outs.append(o + _dot((qh * q_wb[:, lanes]).astype(BF16), sb[n].astype(BF16)))
        for n, y in enumerate(_norm_rows_many(outs)):
            rows = slice(n * C, (n + 1) * C)
            o_ref[rows, vlanes] = (y * _silu(g_ref[rows, vlanes])).astype(BF16)


def _retention(proj, ret_decay_e, s0, *, n_seq, seq):
    rdk = jnp.repeat(ret_decay_e, DK_RET, axis=-1).reshape(2, 1, H_RET * DK_RET)
    rdh = jnp.broadcast_to(ret_decay_e[:, :, None, None], (2, H_RET, 1, RET_CHUNK))
    hk, hv = H_RET * DK_RET, H_RET * DV_RET
    return pl.pallas_call(
        functools.partial(_ret_body, seq=seq),
        grid=(n_seq,),
        in_specs=[pl.BlockSpec((seq, hk), lambda b: (b, E_RQ // hk)),
                  pl.BlockSpec((seq, hk), lambda b: (b, E_RK // hk)),
                  pl.BlockSpec((seq, hv), lambda b: (b, E_RV // hv)),
                  pl.BlockSpec((seq, hv), lambda b: (b, E_RG // hv)),
                  pl.BlockSpec((2, 1, hk), lambda b: (0, 0, 0)),
                  pl.BlockSpec((2, H_RET, 1, RET_CHUNK), lambda b: (0, 0, 0, 0)),
                  pl.BlockSpec((None, 2, H_RET, DK_RET, DV_RET), lambda b: (b, 0, 0, 0, 0))],
        out_specs=[pl.BlockSpec((seq, hv), lambda b: (b, 0)),
                   pl.BlockSpec((None, 2, H_RET, DK_RET, DV_RET), lambda b: (b, 0, 0, 0, 0))],
        out_shape=[jax.ShapeDtypeStruct((n_seq * seq, hv), BF16),
                   jax.ShapeDtypeStruct((n_seq, 2, H_RET, DK_RET, DV_RET), F32)],
        compiler_params=_cparams(1),
        name="retention",
    )(proj, proj, proj, proj, rdk, rdh, s0)


def _cast_weight_chunks(src_refs, dst_refs):
    for src, dst in zip(src_refs, dst_refs):
        dst[...] = src[...].astype(BF16)


def _weight_cast_plan(weights, n_chunks, chunk_of):
    args, in_specs, out_specs, out_shapes = [], [], [], []
    for w, idx in weights:
        rows, cols = w.shape[1], w.shape[2]
        rb = rows // n_chunks
        assert rb * n_chunks == rows and rb % 16 == 0
        args.append(w)
        in_specs.append(pl.BlockSpec((None, rb, cols), lambda *g, idx=idx: (idx, chunk_of(*g), 0)))
        out_specs.append(pl.BlockSpec((rb, cols), lambda *g: (chunk_of(*g), 0)))
        out_shapes.append(jax.ShapeDtypeStruct((rows, cols), BF16))
    return args, in_specs, out_specs, out_shapes


def _col_softmax(parts, extra=None):
    m = parts[0].max(axis=0, keepdims=True)
    for p in parts[1:]:
        m = jnp.maximum(m, p.max(axis=0, keepdims=True))
    if extra is not None:
        m = jnp.maximum(m, extra)
    es = [jnp.exp2(p - m) for p in parts]
    den = es[0].sum(axis=0, keepdims=True)
    for e in es[1:]:
        den = den + e.sum(axis=0, keepdims=True)
    if extra is not None:
        den = den + jnp.exp2(extra - m)
    return es, 1.0 / den


DIFF_TQ = 256
DIFF_CHUNK = 256


def _diff_lambda(p, lam_init):
    a = jnp.sum(p[0:1, :] * p[1:2, :], axis=-1, keepdims=True)
    b = jnp.sum(p[2:3, :] * p[3:4, :], axis=-1, keepdims=True)
    return jnp.exp(a) - jnp.exp(b) + lam_init


def _split_maps(k):
    lane = lax.broadcasted_iota(jnp.int32, k.shape, 1)
    return (jnp.where(lane < DH_DIFF, k, 0.0).astype(BF16), jnp.where(lane >= DH_DIFF, k, 0.0).astype(BF16))


def _diff_latent_body(lam_ref, q_ref, k_ref, v_ref, ck_ref, cv_ref, cq_ref, sq_ref, ckk_ref, skk_ref,
                      *rest, lam_init, n_cast):
    w_src, o_ref, w_dst = rest[:n_cast], rest[n_cast], rest[n_cast + 1:2 * n_cast + 1]
    km_ref, vb_ref, ckm_ref, cvb_ref = rest[2 * n_cast + 1:]
    i = pl.program_id(1)
    _cast_weight_chunks(w_src, w_dst)
    heads = [slice(h * 128, (h + 1) * 128) for h in range(H_DIFF)]

    @pl.when(i == 0)
    def _():
        ckk, skk = ckk_ref[...], skk_ref[...]
        for h, hs in enumerate(heads):
            k1, k2 = _split_maps(_rope128(k_ref[:, hs], ckk, skk))
            km_ref[h, 0] = k1
            km_ref[h, 1] = k2
            vb_ref[h] = v_ref[:, hs].astype(BF16)
            c1, c2 = _split_maps(jnp.concatenate([ck_ref[h, 0], ck_ref[h, 1]], axis=1))
            ckm_ref[h, 0] = c1
            ckm_ref[h, 1] = c2
            cvb_ref[h] = cv_ref[h].astype(BF16)

    lam = _diff_lambda(lam_ref[...], lam_init)
    cq, sq = cq_ref[...], sq_ref[...]
    qbs = [(_rope128(q_ref[:, hs], cq, sq) * (DH_DIFF ** -0.5 * LOG2E)).astype(BF16) for hs in heads]
    ck = DIFF_CHUNK
    n_chunks = (PAST_LEN + km_ref.shape[2]) // ck
    own = lambda c: slice(c * ck - PAST_LEN, (c + 1) * ck - PAST_LEN)
    keys = lambda h, m, c: ckm_ref[h, m] if c == 0 else km_ref[h, m, own(c), :]
    vals = lambda h, c: cvb_ref[h] if c == 0 else vb_ref[h, own(c), :]

    def row_max(chunks):
        m = chunks[0]
        for s in chunks[1:]:
            m = jnp.maximum(m, s)
        return m.max(axis=-1, keepdims=True)

    def add(a, b):
        return b if a is None else a + b

    maps = [(h, m) for h in range(H_DIFF) for m in range(2)]
    logits, exps, psum, acc = {}, {}, {}, {}
    for stage in range(len(maps) + 2):
        p_l = maps[stage] if stage < len(maps) else None
        p_e = maps[stage - 1] if 0 <= stage - 1 < len(maps) else None
        p_v = maps[stage - 2] if 0 <= stage - 2 < len(maps) else None
        if p_l is not None:
            logits[p_l] = []
        if p_e is not None:
            mx = row_max(logits[p_e])
            exps[p_e], psum[p_e] = [], None
        if p_v is not None:
            acc[p_v] = None
        for c in range(n_chunks):
            if p_l is not None:
                logits[p_l].append(_dot_nt(qbs[p_l[0]], keys(p_l[0], p_l[1], c)))
            if p_e is not None:
                e = jnp.exp2(logits[p_e][c] - mx)
                psum[p_e] = add(psum[p_e], e)
                exps[p_e].append(e.astype(BF16))
            if p_v is not None:
                acc[p_v] = add(acc[p_v], _dot(exps[p_v][c], vals(p_v[0], c)))
        if p_e is not None:
            del logits[p_e]
        if p_v is not None:
            del exps[p_v]
    outs = []
    for h in range(H_DIFF):
        inv1 = 1.0 / psum[h, 0].sum(axis=-1, keepdims=True)
        inv2 = 1.0 / psum[h, 1].sum(axis=-1, keepdims=True)
        outs.append(acc[h, 0] * inv1 - acc[h, 1] * (inv2 * lam))
    for hs, y in zip(heads, _norm_rows_many(outs)):
        o_ref[:, hs] = (y * (1.0 - lam_init)).astype(BF16)


def _diff_ctx_body(lam_ref, q_ref, k_ref, v_ref, *rest, lam_init, n_prev):
    o_ref, kc_ref, vc_ref = rest[-3:]
    kc_new, vc_new = kc_ref, vc_ref
    if n_prev:
        kp_ref, vp_ref = rest[0:2]
        for e in range(n_prev):
            kc_ref[e] = kp_ref[e] if n_prev > 1 else kp_ref[...]
            vc_ref[e] = vp_ref[e] if n_prev > 1 else vp_ref[...]
        kc_new, vc_new = kc_ref.at[n_prev], vc_ref.at[n_prev]
    lam = _diff_lambda(lam_ref[...], lam_init)
    heads = [slice(h * 128, (h + 1) * 128) for h in range(H_DIFF)]
    qbs, kms, vbs = [], [], []
    for h, hs in enumerate(heads):
        kk = k_ref[:, hs]
        vv = v_ref[:, hs]
        kc_new[h, 0] = kk[:, 0:DH_DIFF]
        kc_new[h, 1] = kk[:, DH_DIFF:2 * DH_DIFF]
        vc_new[h] = vv
        kms.append(_split_maps(kk))
        vbs.append(vv.astype(BF16))
        qbs.append((q_ref[:, hs] * (DH_DIFF ** -0.5 * LOG2E)).astype(BF16))
    pairs = [(h, m) for h in range(H_DIFF) for m in range(2)]
    logits = {p: _dot_nt(qbs[p[0]], kms[p[0]][p[1]]) for p in pairs}
    exps = {p: jnp.exp2(logits[p] - logits[p].max(axis=-1, keepdims=True)) for p in pairs}
    invs = {p: 1.0 / exps[p].sum(axis=-1, keepdims=True) for p in pairs}
    accs = {p: _dot(exps[p].astype(BF16), vbs[p[0]]) for p in pairs}
    outs = [accs[h, 0] * invs[h, 0] - accs[h, 1] * (invs[h, 1] * lam) for h in range(H_DIFF)]
    for hs, y in zip(heads, _norm_rows_many(outs)):
        o_ref[:, hs] = (y * (1.0 - lam_init)).astype(BF16)


def _diff_attention_ctx(proj, lam_p, lam_init, *, n_seq, seq, prev=None):
    hw = H_DIFF * 128
    n_prev = 0 if prev is None else (1 if prev[0].ndim == 5 else prev[0].shape[1])
    one_k, one_v = (H_DIFF, 2, seq, DH_DIFF), (H_DIFF, seq, DV_DIFF)
    zeros = lambda n: (0,) * n
    in_specs = [pl.BlockSpec((4, DH_DIFF), lambda b: (0, 0)),
                pl.BlockSpec((seq, hw), lambda b: (b, E_DQ // hw)),
                pl.BlockSpec((seq, hw), lambda b: (b, E_DK // hw)),
                pl.BlockSpec((seq, hw), lambda b: (b, E_DV // hw))]
    args = [lam_p, proj, proj, proj]
    if n_prev:
        lead = (None,) if n_prev == 1 else (None, n_prev)
        in_specs += [pl.BlockSpec(lead + one_k, lambda b: (b,) + zeros(len(lead) + 3)),
                     pl.BlockSpec(lead + one_v, lambda b: (b,) + zeros(len(lead) + 2))]
        args += list(prev)
    lead_out = (None, n_prev + 1) if n_prev else (None,)
    out_specs = [pl.BlockSpec((seq, hw), lambda b: (b, 0)),
                 pl.BlockSpec(lead_out + one_k, lambda b: (b,) + zeros(len(lead_out) + 3)),
                 pl.BlockSpec(lead_out + one_v, lambda b: (b,) + zeros(len(lead_out) + 2))]
    out_shape = [jax.ShapeDtypeStruct((n_seq * seq, hw), BF16),
                 jax.ShapeDtypeStruct((n_seq,) + lead_out[1:] + one_k, F32),
                 jax.ShapeDtypeStruct((n_seq,) + lead_out[1:] + one_v, F32)]
    return pl.pallas_call(
        functools.partial(_diff_ctx_body, lam_init=lam_init, n_prev=n_prev),
        grid=(n_seq,),
        in_specs=in_specs,
        out_specs=out_specs,
        out_shape=out_shape,
        compiler_params=_cparams(1),
        name="diff_attention_ctx",
    )(*args)


def _diff_attention_latent(proj, lam_p, lam_init, *, n_seq, seq, cache, rope, cast):
    tq = DIFF_TQ
    nq = seq // tq
    hw = H_DIFF * 128
    ck, cv = cache
    cos, sin = rope
    in_specs = [pl.BlockSpec((4, DH_DIFF), lambda b, i: (0, 0)),
                pl.BlockSpec((tq, hw), lambda b, i: (b * nq + i, E_DQ // hw)),
                pl.BlockSpec((seq, hw), lambda b, i: (b, E_DK // hw)),
                pl.BlockSpec((seq, hw), lambda b, i: (b, E_DV // hw)),
                pl.BlockSpec((None, H_DIFF, 2, PAST_LEN, DH_DIFF), lambda b, i: (b, 0, 0, 0, 0)),
                pl.BlockSpec((None, H_DIFF, PAST_LEN, DV_DIFF), lambda b, i: (b, 0, 0, 0)),
                pl.BlockSpec((tq, 128), lambda b, i: (i, 0)),
                pl.BlockSpec((tq, 128), lambda b, i: (i, 0)),
                pl.BlockSpec((seq, 128), lambda b, i: (0, 0)),
                pl.BlockSpec((seq, 128), lambda b, i: (0, 0))]
    args = [lam_p, proj, proj, proj, ck, cv, cos, sin, cos, sin]
    out_specs = [pl.BlockSpec((tq, hw), lambda b, i: (b * nq + i, 0))]
    out_shape = [jax.ShapeDtypeStruct((n_seq * seq, hw), BF16)]
    w_args, w_in, w_out, w_shape = _weight_cast_plan(cast, n_seq * nq, lambda b, i: b * nq + i)
    scratch = [pltpu.VMEM((H_DIFF, 2, seq, 128), BF16), pltpu.VMEM((H_DIFF, seq, DV_DIFF), BF16),
               pltpu.VMEM((H_DIFF, 2, PAST_LEN, 128), BF16), pltpu.VMEM((H_DIFF, PAST_LEN, DV_DIFF), BF16)]
    return pl.pallas_call(
        functools.partial(_diff_latent_body, lam_init=lam_init, n_cast=len(cast)),
        grid=(n_seq, nq),
        in_specs=in_specs + w_in,
        out_specs=out_specs + w_out,
        out_shape=out_shape + w_shape,
        scratch_shapes=scratch,
        compiler_params=_cparams(2),
        name="diff_attention",
    )(*args, *w_args)


SWA_TQ = 256
SWA_WIN = SWA_TQ + 2 * WINDOW
NEG = -1e30


def _swa_body(*refs, latent, seq, n_prev=0, n_cast=0):
    tq = SWA_TQ
    if latent:
        q_ref, k_ref, v_ref, sink_ref, ck_ref, cv_ref, cq_ref, sq_ref, ckk_ref, skk_ref = refs[:10]
        w_src, o_ref, w_dst = refs[10:10 + n_cast], refs[10 + n_cast], refs[11 + n_cast:11 + 2 * n_cast]
        kr_ref, cvt_ref = refs[11 + 2 * n_cast:]
        _cast_weight_chunks(w_src, w_dst)
        qi = pl.program_id(1)

        @pl.when(qi == 0)
        def _():
            kr_ref[...] = _rope128(k_ref[...], ckk_ref[...], skk_ref[...]).astype(BF16)
            cvt_ref[...] = jnp.concatenate([cv_ref[0], cv_ref[1]], axis=1).T.astype(BF16)

        cq, sq = cq_ref[...], sq_ref[...]
        q = jnp.concatenate([_rope128(q_ref[:, g * 128:(g + 1) * 128], cq, sq)
                             for g in range(H_SWA * DH_SWA // 128)], axis=1)
        ws = pl.multiple_of(jnp.clip(qi * tq - WINDOW, 0, seq - SWA_WIN), 128)
        kw = kr_ref[pl.ds(ws, SWA_WIN), :]
        vwt = v_ref[pl.ds(ws, SWA_WIN), :].T.astype(BF16)
        kpos = ws + lax.broadcasted_iota(jnp.int32, (SWA_WIN, tq), 0)
        qpos = qi * tq + lax.broadcasted_iota(jnp.int32, (SWA_WIN, tq), 1)
        bias1 = jnp.where(jnp.abs(qpos - kpos) <= WINDOW, 0.0, NEG)
        bias = jnp.concatenate([bias1] * SWA_GROUP, axis=1)
    else:
        q_ref, k_ref, v_ref, sink_ref = refs[:4]
        o_ref, kc_ref, vc_ref = refs[-3:]
        q = q_ref[...]
        kk = k_ref[...]
        vv = v_ref[...]
        kc_new, vc_new = kc_ref, vc_ref
        if n_prev:
            kp_ref, vp_ref = refs[4:6]
            for e in range(n_prev):
                kc_ref[e] = kp_ref[e] if n_prev > 1 else kp_ref[...]
                vc_ref[e] = vp_ref[e] if n_prev > 1 else vp_ref[...]
            kc_new, vc_new = kc_ref.at[n_prev], vc_ref.at[n_prev]
        for kv in range(KV_SWA):
            kc_new[kv] = kk[:, kv * DH_SWA:(kv + 1) * DH_SWA]
            vc_new[kv] = vv[:, kv * DH_SWA:(kv + 1) * DH_SWA]
        kw = kk.astype(BF16)
        vwt = vv.T.astype(BF16)
    q = q * (DH_SWA ** -0.5 * LOG2E)
    lanes = [slice(kv * DH_SWA, (kv + 1) * DH_SWA) for kv in range(KV_SWA)]

    def logits(kv):
        q4 = jnp.concatenate([q[:, (kv * SWA_GROUP + g) * DH_SWA:(kv * SWA_GROUP + g + 1) * DH_SWA]
                              for g in range(SWA_GROUP)], axis=0).astype(BF16)
        s_loc = _dot_nt(kw[:, lanes[kv]], q4)
        if latent:
            return [_dot_nt(ck_ref[kv].astype(BF16), q4), s_loc + bias]
        return [s_loc]

    def values(kv, es):
        vals_t = [cvt_ref[lanes[kv], :], vwt[lanes[kv], :]] if latent else [vwt[lanes[kv], :]]
        acc = None
        for e, vt in zip(es, vals_t):
            t = _dot(vt, e.astype(BF16))
            acc = t if acc is None else acc + t
        return acc

    parts = logits(0)
    accs = []
    for kv in range(KV_SWA):
        nxt = logits(kv + 1) if kv + 1 < KV_SWA else None
        es, inv = _col_softmax(parts, extra=sink_ref[kv] * LOG2E)
        accs.append((es, inv))
        parts = nxt
    pieces = []
    for kv, (es, inv) in enumerate(accs):
        acc = values(kv, es) * inv
        pieces += [acc[:, g * tq:(g + 1) * tq] for g in range(SWA_GROUP)]
    o_ref[...] = jnp.concatenate(pieces, axis=0).T.astype(BF16)


def _swa_attention(proj, sink, *, n_seq, seq, cache=None, rope=None, cast=None, prev=None):
    latent = cache is not None
    n_prev = 0
    tq = SWA_TQ
    nq = seq // tq
    hq = H_SWA * DH_SWA
    sink_b = jnp.repeat(sink.reshape(KV_SWA, SWA_GROUP), tq, axis=1).reshape(KV_SWA, 1, SWA_GROUP * tq)
    in_specs = [pl.BlockSpec((tq, hq), lambda b, i: (b * nq + i, 0)),
                pl.BlockSpec((seq, 128), lambda b, i: (b, O_SK // 128)),
                pl.BlockSpec((seq, 128), lambda b, i: (b, O_SV // 128)),
                pl.BlockSpec((KV_SWA, 1, SWA_GROUP * tq), lambda b, i: (0, 0, 0))]
    args = [proj, proj, proj, sink_b]
    scratch = []
    out_specs = [pl.BlockSpec((tq, hq), lambda b, i: (b * nq + i, 0))]
    out_shape = [jax.ShapeDtypeStruct((n_seq * seq, hq), BF16)]
    if latent:
        ck, cv = cache
        cos, sin = rope
        in_specs += [pl.BlockSpec((None, KV_SWA, PAST_LEN, DH_SWA), lambda b, i: (b, 0, 0, 0)),
                     pl.BlockSpec((None, KV_SWA, PAST_LEN, DH_SWA), lambda b, i: (b, 0, 0, 0)),
                     pl.BlockSpec((tq, 128), lambda b, i: (i, 0)),
                     pl.BlockSpec((tq, 128), lambda b, i: (i, 0)),
                     pl.BlockSpec((seq, 128), lambda b, i: (0, 0)),
                     pl.BlockSpec((seq, 128), lambda b, i: (0, 0))]
        args += [ck, cv, cos, sin, cos, sin]
        w_args, w_in, w_out, w_shape = _weight_cast_plan(cast, n_seq * nq, lambda b, i: b * nq + i)
        args += w_args
        in_specs += w_in
        out_specs += w_out
        out_shape += w_shape
        scratch = [pltpu.VMEM((seq, 128), BF16), pltpu.VMEM((KV_SWA * DH_SWA, PAST_LEN), BF16)]
    else:
        n_prev = 0 if prev is None else (1 if prev[0].ndim == 4 else prev[0].shape[1])
        one = (KV_SWA, seq, DH_SWA)
        if n_prev:
            lead = (None,) if n_prev == 1 else (None, n_prev)
            in_specs += [pl.BlockSpec(lead + one, lambda b, i: (b,) + (0,) * (len(lead) + 2))] * 2
            args += list(prev)
        lead_out = (None, n_prev + 1) if n_prev else (None,)
        cache_spec = pl.BlockSpec(lead_out + one, lambda b, i: (b,) + (0,) * (len(lead_out) + 2))
        cache_shape = jax.ShapeDtypeStruct((n_seq,) + lead_out[1:] + one, F32)
        out_specs += [cache_spec, cache_spec]
        out_shape += [cache_shape, cache_shape]
    return pl.pallas_call(
        functools.partial(_swa_body, latent=latent, seq=seq, n_prev=n_prev, n_cast=len(cast or ())),
        grid=(n_seq, nq),
        in_specs=in_specs,
        out_specs=out_specs,
        out_shape=out_shape,
        scratch_shapes=scratch,
        compiler_params=_cparams(2),
        name="swa_attention",
    )(*args)


GLA_BLOCK = 128
GLA_LEVELS = 7
GLA_SAFE = 60.0
GLA_GROUP = 4
GLA_GATE_ROWS = 1024


def _split3(x):
    hi = x.astype(BF16)
    r1 = x - hi.astype(F32)
    mid = r1.astype(BF16)
    lo = (r1 - mid.astype(F32)).astype(BF16)
    return hi, mid, lo


def _gla_gate_body(r_ref, w2_ref, b2_ref, x_ref, tot_ref):
    T = GLA_BLOCK
    ti = lax.broadcasted_iota(jnp.int32, (T, T), 0)
    si = lax.broadcasted_iota(jnp.int32, (T, T), 1)
    r_hi, r_mid, _ = _split3(r_ref[...])
    zs = []
    for d in range(2):
        w_hi, w_mid, _ = _split3(w2_ref[d])
        zs.append(b2_ref[d] + _dot(r_hi, w_hi) + _dot(r_hi, w_mid) + _dot(r_mid, w_hi))
    las = [_split3((jnp.minimum(z, 0.0) - jnp.log(1.0 + jnp.exp(-jnp.abs(z)))) * (1.0 / GLA_TAU)) for z in zs]
    for d in range(2):
        tri = jnp.where((si >= ti) if d else (si <= ti), 1.0, 0.0).astype(BF16)
        l_hi, l_mid, l_lo = las[d]
        for j in range(GLA_GATE_ROWS // T):
            rows = slice(j * T, (j + 1) * T)
            x = _dot(tri, l_hi[rows]) + _dot(tri, l_mid[rows]) + _dot(tri, l_lo[rows])
            x_ref[d, rows, :] = x
            tot_ref[d, j:j + 1, :] = x[0:1, :] if d else x[T - 1:T, :]


def _gla_gates(proj, w2, b2):
    n_rows = proj.shape[0]
    hk = H_GLA * DK_GLA
    tr = GLA_GATE_ROWS
    w2p = jnp.zeros((2, 128, hk), F32)
    w2p = w2p.at[0, 0:GLA_RANK].set(w2[0]).at[1, GLA_RANK:2 * GLA_RANK].set(w2[1])
    return pl.pallas_call(
        _gla_gate_body,
        grid=(n_rows // tr,),
        in_specs=[pl.BlockSpec((tr, 128), lambda i: (i, O_GLR // 128)),
                  pl.BlockSpec((2, 128, hk), lambda i: (0, 0, 0)),
                  pl.BlockSpec((2, 1, hk), lambda i: (0, 0, 0))],
        out_specs=[pl.BlockSpec((2, tr, hk), lambda i: (0, i, 0)),
                   pl.BlockSpec((2, tr // GLA_BLOCK, hk), lambda i: (0, i, 0))],
        out_shape=[jax.ShapeDtypeStruct((2, n_rows, hk), F32),
                   jax.ShapeDtypeStruct((2, n_rows // GLA_BLOCK, hk), F32)],
        compiler_params=_cparams(1),
        name="gla_gates",
    )(proj, w2p, b2.reshape(2, 1, hk))


def _gla_body(*refs, rev, n_blocks, finish):
    if finish:
        (flag_ref, q_ref, k_ref, v_ref, x_ref, s0_ref, of_ref, g_ref, o_ref, sfin_ref, st_ref, att_ref) = refs
    else:
        flag_ref, q_ref, k_ref, v_ref, x_ref, s0_ref, o_ref, sfin_ref, st_ref, att_ref = refs
    n = pl.program_id(1)
    blk = (n_blocks - 1 - n) if rev else n
    safe = flag_ref[pl.program_id(0) * n_blocks + blk] != 0
    T = GLA_BLOCK
    G = q_ref.shape[0]
    hk = H_GLA * DK_GLA
    heads = [slice(h * DK_GLA, (h + 1) * DK_GLA) for h in range(H_GLA)]
    scale = DK_GLA ** -0.5
    ti = lax.broadcasted_iota(jnp.int32, (T, T), 0)
    si = lax.broadcasted_iota(jnp.int32, (T, T), 1)
    causal = (si >= ti) if rev else (si <= ti)

    @pl.when(n == 0)
    def _():
        st_ref[...] = s0_ref[...]

    @pl.when(safe)
    def _():
        for j in range(G):
            x = x_ref[j]
            qs = (q_ref[j] * scale * jnp.exp(x)).astype(BF16)
            ks = (k_ref[j] * jnp.exp(-x)).astype(BF16)
            for h, hs in enumerate(heads):
                att_ref[j, h] = jnp.where(causal, _dot_nt(qs[:, hs], ks[:, hs]), 0.0)

    @pl.when(jnp.logical_not(safe))
    def _():
        t_idx = lax.broadcasted_iota(jnp.int32, (T, hk), 0)
        xr = ti ^ si
        for j in range(G):
            q = q_ref[j] * scale
            k = k_ref[j]
            x = x_ref[j]
            if rev:
                la = jnp.where(t_idx == T - 1, x, x - pltpu.roll(x, T - 1, 0))
            else:
                la = jnp.where(t_idx == 0, x, x - pltpu.roll(x, 1, 0))
            qb, kb = q.astype(BF16), k.astype(BF16)
            att = [_dot_nt(qb[:, hs], kb[:, hs]) for hs in heads]
            xg = la
            tg = la
            for lvl in range(GLA_LEVELS):
                sz = 1 << lvl
                upper = ((t_idx >> lvl) & 1) == 1
                is_q = jnp.logical_not(upper) if rev else upper
                partner = jnp.where(upper, pltpu.roll(tg, sz, 0), pltpu.roll(tg, T - sz, 0))
                e = jnp.exp(jnp.where(is_q, xg, tg - xg))
                qs = jnp.where(is_q, q * e, 0.0).astype(BF16)
                ks = jnp.where(is_q, 0.0, k * e).astype(BF16)
                for h, hs in enumerate(heads):
                    att[h] = jnp.where(xr >= sz, _dot_nt(qs[:, hs], ks[:, hs]), att[h])
                xg = xg + jnp.where(is_q, partner, 0.0)
                tg = tg + partner
            for h in range(H_GLA):
                att_ref[j, h] = att[h]

    outs = {}
    for j in range(G):
        x = x_ref[j]
        tot = x[0:1, :] if rev else x[T - 1:T, :]
        v = v_ref[j]
        qe = (q_ref[j] * scale * jnp.exp(x)).astype(BF16)
        kw = (k_ref[j] * jnp.exp(tot - x)).astype(BF16)
        vt = v.T
        vb = v.astype(BF16)
        dec = jnp.exp(tot)
        for h, hs in enumerate(heads):
            vs = slice(h * DV_GLA, (h + 1) * DV_GLA)
            st = st_ref[j, h]
            o = _dot(att_ref[j, h].astype(BF16), vb[:, vs]) + _dot_nt(qe[:, hs], st.astype(BF16))
            if finish:
                outs[j, h] = o + of_ref[j, :, vs]
            else:
                o_ref[j, :, vs] = o
            st_ref[j, h] = st * dec[:, hs] + _dot(vt[vs, :].astype(BF16), kw[:, hs])
    if finish:
        keys = sorted(outs)
        normed = _norm_rows_many([outs[k] for k in keys])
        for (j, h), y in zip(keys, normed):
            vs = slice(h * DV_GLA, (h + 1) * DV_GLA)
            o_ref[j, :, vs] = (y * _silu(g_ref[j, :, vs])).astype(BF16)

    @pl.when(n == n_blocks - 1)
    def _():
        sfin_ref[...] = st_ref[...]


def _gla(proj, w2, b2, s0t, *, n_seq, seq):
    nb = seq // GLA_BLOCK
    G = min(GLA_GROUP, n_seq)
    hk, hv = H_GLA * DK_GLA, H_GLA * DV_GLA
    x, tot = _gla_gates(proj, w2, b2)
    x = x.reshape(2, n_seq, seq, hk)
    safe = (tot.min(axis=-1) > -GLA_SAFE).reshape(2, n_seq // G, G, nb).all(axis=2)
    flags = safe.astype(jnp.int32).reshape(2, (n_seq // G) * nb)
    p3 = proj.reshape(n_seq, seq, ODD_IN_PAD)
    state_shape = jax.ShapeDtypeStruct((n_seq, H_GLA, DV_GLA, DK_GLA), F32)
    state_spec = pl.BlockSpec((G, H_GLA, DV_GLA, DK_GLA), lambda g, n, f: (g, 0, 0, 0))
    scratch = [pltpu.VMEM((G, H_GLA, DV_GLA, DK_GLA), F32), pltpu.VMEM((G, H_GLA, GLA_BLOCK, GLA_BLOCK), F32)]

    def run(rev, extra_args, extra_cols, out_dtype):
        blk = (lambda n: nb - 1 - n) if rev else (lambda n: n)
        d = 1 if rev else 0
        tok = lambda width, col: pl.BlockSpec((G, GLA_BLOCK, width), lambda g, n, f: (g, blk(n), col))
        grid_spec = pltpu.PrefetchScalarGridSpec(
            num_scalar_prefetch=1,
            grid=(n_seq // G, nb),
            in_specs=[tok(hk, O_GQ // hk), tok(hk, O_GK // hk), tok(hv, O_GV // hv),
                      pl.BlockSpec((None, G, GLA_BLOCK, hk), lambda g, n, f: (d, g, blk(n), 0)),
                      pl.BlockSpec((None, G, H_GLA, DV_GLA, DK_GLA), lambda g, n, f: (d, g, 0, 0, 0))]
            + [tok(hv, col) for col in extra_cols],
            out_specs=[tok(hv, 0), state_spec],
            scratch_shapes=scratch)
        return pl.pallas_call(
            functools.partial(_gla_body, rev=rev, n_blocks=nb, finish=rev),
            grid_spec=grid_spec,
            out_shape=[jax.ShapeDtypeStruct((n_seq, seq, hv), out_dtype), state_shape],
            compiler_params=_cparams(2),
            name="gla_bwd" if rev else "gla_fwd",
        )(flags[d], p3, p3, p3, x, s0t, *extra_args)

    o_f, s_f = run(False, [], [], F32)
    out, s_b = run(True, [o_f, p3], [0, O_GR // hv], BF16)
    return out.reshape(n_seq * seq, hv), jnp.stack([s_f, s_b], axis=0)


def _stacked(caches, n_layers):
    return [c[:, None] for c in caches] if n_layers == 1 else caches


def kernel(x_prompt, x_sample, c, state_ret, cache_diff_k, cache_diff_v, cache_swa_k, cache_swa_v, state_gla,
           c_ctx, w_mod, b_mod, ln_g, ln_b, w_in_even, w_out_even, ret_decay, diff_lam, w_in_odd, w_out_odd,
           swa_sink, gla_w2, gla_b, w_ff1, w_ff2):
    xc = x_prompt.reshape(N_CTX, D_MODEL)
    xl = x_sample.reshape(N_LAT, D_MODEL)
    cond = jnp.concatenate([c_ctx[None, :], c, jnp.zeros((N_COND - 1 - DEC_BATCH, D_MODEL), F32)], axis=0)
    mod = _modulation(cond, w_mod, b_mod).reshape(DEPTH, N_COND, 1, 6 * D_MODEL)
    rope = _rope_tables(DEC_SEQ // GRID_W, DH_DIFF)

    w_in_even_b = w_in_even.astype(BF16)
    w_in_odd_b = _reorder_odd(w_in_odd.astype(BF16))

    ctx = dict(n_seq=BATCH, seq=SEQ)
    lat = dict(n_seq=DEC_BATCH, seq=DEC_SEQ)
    new_ret, new_gla = [], []
    new_diff = new_swa = None
    for l in range(DEPTH):
        if l % 2 == 0:
            e = l // 2
            lam_init = 0.8 - 0.6 * math.exp(-0.3 * l)
            pc = _in_projection(xc, mod[l], w_in_even_b[e], False)
            pl_ = _in_projection(xl, mod[l], w_in_even_b[e], True)
            zero = jnp.zeros((BATCH, 2, H_RET, DK_RET, DV_RET), F32)
            a_ctx, s_ret = _retention(pc, ret_decay[e], zero, **ctx)
            a_lat, _ = _retention(pl_, ret_decay[e], state_ret[:, e], **lat)
            b_ctx, *new_diff = _diff_attention_ctx(pc, diff_lam[e], lam_init, prev=new_diff, **ctx)
            b_lat, *w_post = _diff_attention_latent(pl_, diff_lam[e], lam_init,
                                                    cache=(cache_diff_k[:, e], cache_diff_v[:, e]), rope=rope,
                                                    cast=[(w_out_even, e), (w_ff1, l), (w_ff2, l)], **lat)
            new_ret.append(s_ret)
        else:
            o = l // 2
            pc = _in_projection(xc, mod[l], w_in_odd_b[o], False)
            pl_ = _in_projection(xl, mod[l], w_in_odd_b[o], True)
            a_ctx, *new_swa = _swa_attention(pc, swa_sink[o], prev=new_swa, **ctx)
            a_lat, *w_post = _swa_attention(pl_, swa_sink[o], cache=(cache_swa_k[:, o], cache_swa_v[:, o]),
                                            rope=rope, cast=[(w_out_odd, o), (w_ff1, l), (w_ff2, l)], **lat)
            zero = jnp.zeros((2, BATCH, H_GLA, DV_GLA, DK_GLA), F32)
            b_ctx, s_gla = _gla(pc, gla_w2[o], gla_b[o], zero, **ctx)
            s0t = state_gla[:, o].transpose(1, 0, 2, 4, 3)
            b_lat, _ = _gla(pl_, gla_w2[o], gla_b[o], s0t, **lat)
            new_gla.append(s_gla.transpose(1, 0, 2, 4, 3))
        w_out_b, w1_b, w2_b = w_post
        xc = _post_mixer(xc, a_ctx, b_ctx, mod[l], w_out_b, ln_g, ln_b, w1_b, w2_b, l, False)
        xl = _post_mixer(xl, a_lat, b_lat, mod[l], w_out_b, ln_g, ln_b, w1_b, w2_b, l, True)
    return (xc.reshape(BATCH, SEQ, D_MODEL), xl.reshape(DEC_BATCH, DEC_SEQ, D_MODEL),
            jnp.stack(new_ret, axis=1), *_stacked(new_diff, N_EVEN), *_stacked(new_swa, N_ODD),
            jnp.stack(new_gla, axis=1))
```

```python
import functools
import math

import jax
import jax.numpy as jnp
from jax import lax
from jax.experimental import pallas as pl
from jax.experimental.pallas import tpu as pltpu

F32 = jnp.float32
BF16 = jnp.bfloat16

D_MODEL = 1024
BATCH = 16
SEQ = 256
DEPTH = 4
DEC_BATCH = 4
DEC_SEQ = 2048
PAST_LEN = 256
GRID_W = 64
H_RET, DK_RET, DV_RET = 4, 64, 128
H_DIFF, DH_DIFF, DV_DIFF = 4, 64, 128
H_SWA, KV_SWA, DH_SWA = 8, 2, 64
SWA_GROUP = H_SWA // KV_SWA
WINDOW = 128
H_GLA, DK_GLA, DV_GLA = 4, 64, 128
GLA_RANK = 16
GLA_TAU = 16.0
D_FF = 4 * D_MODEL
ROPE_BASE = 10000.0
N_EVEN = (DEPTH + 1) // 2
N_ODD = DEPTH // 2
ALPHA = (2 * DEPTH) ** 0.25
EVEN_IN = 3072
ODD_IN = 2336
ODD_IN_PAD = 2432
EPS = 1e-5
LOG2E = 1.4426950408889634

E_RQ, E_RK, E_RV, E_RG, E_DQ, E_DK, E_DV = 0, 256, 512, 1024, 1536, 2048, 2560
O_SQ, O_GV, O_GR, O_GQ, O_GK, O_SK, O_SV, O_GLR = 0, 512, 1024, 1536, 1792, 2048, 2176, 2304


def _reorder_odd(w):
    sq, sk, sv, gq, gk, gv, gr, glr = jnp.split(w, [512, 640, 768, 1024, 1280, 1792, 2304], axis=-1)
    pad = jnp.zeros(w.shape[:-1] + (ODD_IN_PAD - ODD_IN,), w.dtype)
    return jnp.concatenate([sq, gv, gr, gq, gk, sk, sv, glr, pad], axis=-1)


N_CTX = BATCH * SEQ
N_LAT = DEC_BATCH * DEC_SEQ
DENSE_TILE = 512
POST_TILE = 1024
N_COND = 8

VMEM_LIMIT = 56 * 1024 * 1024


def _cparams(n_axes):
    return pltpu.CompilerParams(dimension_semantics=("arbitrary",) * n_axes,
                                vmem_limit_bytes=VMEM_LIMIT)


def _dot(a, b):
    return jnp.dot(a, b, preferred_element_type=F32)


def _dot_nt(a, b):
    return lax.dot_general(a, b, (((1,), (1,)), ((), ())), preferred_element_type=F32)


def _silu(x):
    return x * (1.0 / (1.0 + jnp.exp(-x)))


def _norm_rows(x):
    mu = jnp.mean(x, axis=-1, keepdims=True)
    xc = x - mu
    var = jnp.mean(xc * xc, axis=-1, keepdims=True)
    return xc * lax.rsqrt(var + EPS)


def _norm_rows_many(xs):
    mus = [jnp.mean(x, axis=-1, keepdims=True) for x in xs]
    xcs = [x - mu for x, mu in zip(xs, mus)]
    vs = [jnp.mean(xc * xc, axis=-1, keepdims=True) for xc in xcs]
    return [xc * lax.rsqrt(v + EPS) for xc, v in zip(xcs, vs)]


def _mod_row(i, latent, tile):
    return 1 + i // (DEC_SEQ // tile) if latent else 0


def _resident(shape, index_map):
    return pl.BlockSpec(shape, index_map, pipeline_mode=pl.Buffered(1))


def _mod_body(c_ref, w_ref, b_ref, o_ref):
    c = _silu(c_ref[...]).astype(BF16)
    o_ref[...] = _dot(c, w_ref[...].astype(BF16)) + b_ref[...]


def _modulation(cond, w_mod, b_mod):
    tn = 1536
    return pl.pallas_call(
        _mod_body,
        grid=(DEPTH, 6 * D_MODEL // tn),
        in_specs=[pl.BlockSpec((N_COND, D_MODEL), lambda l, j: (0, 0)),
                  pl.BlockSpec((None, D_MODEL, tn), lambda l, j: (l, 0, j)),
                  pl.BlockSpec((None, 1, tn), lambda l, j: (l, 0, j))],
        out_specs=pl.BlockSpec((None, N_COND, tn), lambda l, j: (l, 0, j)),
        out_shape=jax.ShapeDtypeStruct((DEPTH, N_COND, 6 * D_MODEL), F32),
        compiler_params=_cparams(2),
        name="modulation",
    )(cond, w_mod, b_mod.reshape(DEPTH, 1, 6 * D_MODEL))


def _inproj_body(x_ref, mod_ref, w_ref, o_ref):
    sh = mod_ref[:, 0:D_MODEL]
    sc = mod_ref[:, D_MODEL:2 * D_MODEL]
    h = (x_ref[...] * (1.0 + sc) + sh).astype(BF16)
    o_ref[...] = _dot(h, w_ref[...])


def _in_projection(x, mod_l, w, latent):
    n_in = w.shape[1]
    n_rows = x.shape[0]
    tm = DENSE_TILE
    return pl.pallas_call(
        _inproj_body,
        grid=(n_rows // tm,),
        in_specs=[pl.BlockSpec((tm, D_MODEL), lambda i: (i, 0)),
                  pl.BlockSpec((None, 1, 6 * D_MODEL), lambda i: (_mod_row(i, latent, tm), 0, 0)),
                  _resident((D_MODEL, n_in), lambda i: (0, 0))],
        out_specs=pl.BlockSpec((tm, n_in), lambda i: (i, 0)),
        out_shape=jax.ShapeDtypeStruct((n_rows, n_in), F32),
        compiler_params=_cparams(1),
        name="in_projection",
    )(x, mod_l, w)


def _layer_norm(x, g, b):
    return _norm_rows(x) * g + b


def _post_body(x_ref, ma_ref, mb_ref, mod_ref, wo_ref, g_ref, b_ref, w1_ref, w2_ref, o_ref):
    half = D_MODEL // 2
    gt1 = mod_ref[:, 2 * D_MODEL:3 * D_MODEL]
    sh2 = mod_ref[:, 3 * D_MODEL:4 * D_MODEL]
    sc2 = mod_ref[:, 4 * D_MODEL:5 * D_MODEL]
    gt2 = mod_ref[:, 5 * D_MODEL:6 * D_MODEL]
    chunk = 1024
    n_chunks = D_FF // chunk
    top, bot = slice(0, POST_TILE // 2), slice(POST_TILE // 2, POST_TILE)

    def mix(r):
        return _dot(ma_ref[r, :], wo_ref[0:half, :]) + _dot(mb_ref[r, :], wo_ref[half:D_MODEL, :])

    def norm1(r, m):
        x1 = _layer_norm(ALPHA * x_ref[r, :] + gt1 * m, g_ref[0:1, :], b_ref[0:1, :])
        return x1, (x1 * (1.0 + sc2) + sh2).astype(BF16)

    def ffn(hf, j):
        h1 = _dot(hf, w1_ref[:, j * chunk:(j + 1) * chunk])
        h1 = jnp.square(jnp.maximum(h1, 0.0)).astype(BF16)
        return _dot(h1, w2_ref[j * chunk:(j + 1) * chunk, :])

    def norm2(r, x1, ff):
        o_ref[r, :] = _layer_norm(ALPHA * x1 + gt2 * ff, g_ref[1:2, :], b_ref[1:2, :])

    m_top, m_bot = mix(top), mix(bot)
    x1_top, hf_top = norm1(top, m_top)
    ff_top = ffn(hf_top, 0)
    x1_bot, hf_bot = norm1(bot, m_bot)
    for j in range(1, n_chunks):
        ff_top = ff_top + ffn(hf_top, j)
    ff_bot = ffn(hf_bot, 0)
    norm2(top, x1_top, ff_top)
    for j in range(1, n_chunks):
        ff_bot = ff_bot + ffn(hf_bot, j)
    norm2(bot, x1_bot, ff_bot)


def _post_mixer(x, mix_a, mix_b, mod_l, w_out, ln_g, ln_b, w1, w2, layer, latent):
    half = D_MODEL // 2
    n_rows = x.shape[0]
    tm = POST_TILE
    return pl.pallas_call(
        _post_body,
        grid=(n_rows // tm,),
        in_specs=[pl.BlockSpec((tm, D_MODEL), lambda i: (i, 0)),
                  pl.BlockSpec((tm, half), lambda i: (i, 0)),
                  pl.BlockSpec((tm, half), lambda i: (i, 0)),
                  pl.BlockSpec((None, 1, 6 * D_MODEL), lambda i: (_mod_row(i, latent, tm), 0, 0)),
                  _resident((D_MODEL, D_MODEL), lambda i: (0, 0)),
                  _resident((None, 2, D_MODEL), lambda i: (layer, 0, 0)),
                  _resident((None, 2, D_MODEL), lambda i: (layer, 0, 0)),
                  _resident((D_MODEL, D_FF), lambda i: (0, 0)),
                  _resident((D_FF, D_MODEL), lambda i: (0, 0))],
        out_specs=pl.BlockSpec((tm, D_MODEL), lambda i: (i, 0)),
        out_shape=jax.ShapeDtypeStruct((n_rows, D_MODEL), F32),
        compiler_params=_cparams(1),
        name="post_mixer",
    )(x, mix_a, mix_b, mod_l, w_out, ln_g, ln_b, w1, w2)


def _rope_tables(rows, dim):
    row = jnp.repeat(jnp.arange(rows), GRID_W).astype(F32)
    col = jnp.tile(jnp.arange(GRID_W), rows).astype(F32)
    half = dim // 2
    freqs = ROPE_BASE ** (-jnp.arange(0, half, 2, dtype=F32) / half)
    ar = row[:, None] * freqs
    ac = col[:, None] * freqs
    cr, sr, cc, sc = jnp.cos(ar), jnp.sin(ar), jnp.cos(ac), jnp.sin(ac)
    cos = jnp.concatenate([cr, cr, cc, cc], axis=-1)
    sin = jnp.concatenate([-sr, sr, -sc, sc], axis=-1)
    reps = 128 // dim
    return jnp.tile(cos, (1, reps)), jnp.tile(sin, (1, reps))


def _rope128(x, cos, sin):
    lane = lax.broadcasted_iota(jnp.int32, x.shape, 1)
    first = (lane & 31) < 16
    swapped = jnp.where(first, pltpu.roll(x, 112, 1), pltpu.roll(x, 16, 1))
    return x * cos + swapped * sin


RET_CHUNK = 256


def _ret_body(q_ref, k_ref, v_ref, g_ref, rdk_ref, rdh_ref, s0_ref, o_ref, sfin_ref, *, seq):
    C = RET_CHUNK
    n_chunks = seq // C
    scale = DK_RET ** -0.5
    ii = lax.broadcasted_iota(jnp.int32, (C, C), 0)
    jj = lax.broadcasted_iota(jnp.int32, (C, C), 1)
    dist = (ii - jj).astype(F32)
    adist = jnp.abs(dist)
    pos = lax.broadcasted_iota(jnp.int32, (C, H_RET * DK_RET), 0).astype(F32)
    lgf = -jnp.exp(rdk_ref[0])
    lgb = -jnp.exp(rdk_ref[1])
    k_wf = jnp.exp(lgf * (C - 1.0 - pos)) * scale
    k_wb = jnp.exp(lgb * pos) * scale
    q_wf = jnp.exp(lgf * (pos + 1.0))
    q_wb = jnp.exp(lgb * (C - pos))

    uf, ub = [], []
    for n in range(n_chunks):
        kn = k_ref[n * C:(n + 1) * C, :]
        kft = (kn * k_wf).T.astype(BF16)
        kbt = (kn * k_wb).T.astype(BF16)
        ufn, ubn = [], []
        for h in range(H_RET):
            vh = v_ref[n * C:(n + 1) * C, h * DV_RET:(h + 1) * DV_RET].astype(BF16)
            ufn.append(_dot(kft[h * DK_RET:(h + 1) * DK_RET, :], vh))
            ubn.append(_dot(kbt[h * DK_RET:(h + 1) * DK_RET, :], vh))
        uf.append(ufn)
        ub.append(ubn)

    for h in range(H_RET):
        lgf_h = -jnp.exp(rdh_ref[0, h])
        lgb_h = -jnp.exp(rdh_ref[1, h])
        cf = jnp.exp(lgf_h * float(C))[:, 0:DV_RET]
        cb = jnp.exp(lgb_h * float(C))[:, 0:DV_RET]
        dmat = jnp.exp(jnp.where(dist > 0, lgf_h, lgb_h) * adist) + jnp.where(dist == 0, 1.0, 0.0)
        sf = [s0_ref[0, h]]
        for n in range(n_chunks):
            sf.append(cf * sf[n] + uf[n][h])
        sb = [None] * n_chunks
        sb[n_chunks - 1] = s0_ref[1, h]
        for n in range(n_chunks - 1, 0, -1):
            sb[n - 1] = cb * sb[n] + ub[n][h]
        sfin_ref[0, h] = sf[n_chunks]
        sfin_ref[1, h] = cb * sb[0] + ub[0][h]
        lanes = slice(h * DK_RET, (h + 1) * DK_RET)
        vlanes = slice(h * DV_RET, (h + 1) * DV_RET)
        outs = []
        for n in range(n_chunks):
            rows = slice(n * C, (n + 1) * C)
            qh = q_ref[rows, lanes]
            kh = (k_ref[rows, lanes] * scale).astype(BF16)
            vh = v_ref[rows, vlanes].astype(BF16)
            s = _dot_nt(qh.astype(BF16), kh) * dmat
            o = _dot(s.astype(BF16), vh)
            o = o + _dot((qh * q_wf[:, lanes]).astype(BF16), sf[n].astype(BF16))
            outs.append(o + _dot((qh * q_wb[:, lanes]).astype(BF16), sb[n].astype(BF16)))
        for n, y in enumerate(_norm_rows_many(outs)):
            rows = slice(n * C, (n + 1) * C)
            o_ref[rows, vlanes] = (y * _silu(g_ref[rows, vlanes])).astype(BF16)


def _retention(proj, ret_decay_e, s0, *, n_seq, seq):
    rdk = jnp.repeat(ret_decay_e, DK_RET, axis=-1).reshape(2, 1, H_RET * DK_RET)
    rdh = jnp.broadcast_to(ret_decay_e[:, :, None, None], (2, H_RET, 1, RET_CHUNK))
    hk, hv = H_RET * DK_RET, H_RET * DV_RET
    return pl.pallas_call(
        functools.partial(_ret_body, seq=seq),
        grid=(n_seq,),
        in_specs=[pl.BlockSpec((seq, hk), lambda b: (b, E_RQ // hk)),
                  pl.BlockSpec((seq, hk), lambda b: (b, E_RK // hk)),
                  pl.BlockSpec((seq, hv), lambda b: (b, E_RV // hv)),
                  pl.BlockSpec((seq, hv), lambda b: (b, E_RG // hv)),
                  pl.BlockSpec((2, 1, hk), lambda b: (0, 0, 0)),
                  pl.BlockSpec((2, H_RET, 1, RET_CHUNK), lambda b: (0, 0, 0, 0)),
                  pl.BlockSpec((None, 2, H_RET, DK_RET, DV_RET), lambda b: (b, 0, 0, 0, 0))],
        out_specs=[pl.BlockSpec((seq, hv), lambda b: (b, 0)),
                   pl.BlockSpec((None, 2, H_RET, DK_RET, DV_RET), lambda b: (b, 0, 0, 0, 0))],
        out_shape=[jax.ShapeDtypeStruct((n_seq * seq, hv), BF16),
                   jax.ShapeDtypeStruct((n_seq, 2, H_RET, DK_RET, DV_RET), F32)],
        compiler_params=_cparams(1),
        name="retention",
    )(proj, proj, proj, proj, rdk, rdh, s0)


def _cast_weight_chunks(src_refs, dst_refs):
    for src, dst in zip(src_refs, dst_refs):
        dst[...] = src[...].astype(BF16)


def _weight_cast_plan(weights, n_chunks, chunk_of):
    args, in_specs, out_specs, out_shapes = [], [], [], []
    for w, idx in weights:
        rows, cols = w.shape[1], w.shape[2]
        rb = rows // n_chunks
        assert rb * n_chunks == rows and rb % 16 == 0
        args.append(w)
        in_specs.append(pl.BlockSpec((None, rb, cols), lambda *g, idx=idx: (idx, chunk_of(*g), 0)))
        out_specs.append(pl.BlockSpec((rb, cols), lambda *g: (chunk_of(*g), 0)))
        out_shapes.append(jax.ShapeDtypeStruct((rows, cols), BF16))
    return args, in_specs, out_specs, out_shapes


def _col_softmax(parts, extra=None):
    m = parts[0].max(axis=0, keepdims=True)
    for p in parts[1:]:
        m = jnp.maximum(m, p.max(axis=0, keepdims=True))
    if extra is not None:
        m = jnp.maximum(m, extra)
    es = [jnp.exp2(p - m) for p in parts]
    den = es[0].sum(axis=0, keepdims=True)
    for e in es[1:]:
        den = den + e.sum(axis=0, keepdims=True)
    if extra is not None:
        den = den + jnp.exp2(extra - m)
    return es, 1.0 / den


DIFF_TQ = 256
DIFF_CHUNK = 256


def _diff_lambda(p, lam_init):
    a = jnp.sum(p[0:1, :] * p[1:2, :], axis=-1, keepdims=True)
    b = jnp.sum(p[2:3, :] * p[3:4, :], axis=-1, keepdims=True)
    return jnp.exp(a) - jnp.exp(b) + lam_init


def _split_maps(k):
    lane = lax.broadcasted_iota(jnp.int32, k.shape, 1)
    return (jnp.where(lane < DH_DIFF, k, 0.0).astype(BF16), jnp.where(lane >= DH_DIFF, k, 0.0).astype(BF16))


def _diff_latent_body(lam_ref, q_ref, k_ref, v_ref, ck_ref, cv_ref, cq_ref, sq_ref, ckk_ref, skk_ref,
                      *rest, lam_init, n_cast):
    w_src, o_ref, w_dst = rest[:n_cast], rest[n_cast], rest[n_cast + 1:2 * n_cast + 1]
    km_ref, vb_ref, ckm_ref, cvb_ref = rest[2 * n_cast + 1:]
    i = pl.program_id(1)
    _cast_weight_chunks(w_src, w_dst)
    heads = [slice(h * 128, (h + 1) * 128) for h in range(H_DIFF)]

    @pl.when(i == 0)
    def _():
        ckk, skk = ckk_ref[...], skk_ref[...]
        for h, hs in enumerate(heads):
            k1, k2 = _split_maps(_rope128(k_ref[:, hs], ckk, skk))
            km_ref[h, 0] = k1
            km_ref[h, 1] = k2
            vb_ref[h] = v_ref[:, hs].T.astype(BF16)
            c1, c2 = _split_maps(jnp.concatenate([ck_ref[h, 0], ck_ref[h, 1]], axis=1))
            ckm_ref[h, 0] = c1
            ckm_ref[h, 1] = c2
            cvb_ref[h] = cv_ref[h].T.astype(BF16)

    lam = _diff_lambda(lam_ref[...], lam_init)
    cq, sq = cq_ref[...], sq_ref[...]
    qbs = [(_rope128(q_ref[:, hs], cq, sq) * (DH_DIFF ** -0.5 * LOG2E)).astype(BF16) for hs in heads]
    ck = DIFF_CHUNK
    n_chunks = (PAST_LEN + km_ref.shape[2]) // ck
    own = lambda c: slice(c * ck - PAST_LEN, (c + 1) * ck - PAST_LEN)
    keys = lambda h, m, c: ckm_ref[h, m] if c == 0 else km_ref[h, m, own(c), :]
    vals_t = lambda h, c: cvb_ref[h] if c == 0 else vb_ref[h, :, own(c)]

    def col_max(chunks):
        m = chunks[0]
        for s in chunks[1:]:
            m = jnp.maximum(m, s)
        return m.max(axis=0, keepdims=True)

    def add(a, b):
        return b if a is None else a + b

    maps = [(h, m) for h in range(H_DIFF) for m in range(2)]
    logits, exps, psum, acc = {}, {}, {}, {}
    for stage in range(len(maps) + 2):
        p_l = maps[stage] if stage < len(maps) else None
        p_e = maps[stage - 1] if 0 <= stage - 1 < len(maps) else None
        p_v = maps[stage - 2] if 0 <= stage - 2 < len(maps) else None
        if p_l is not None:
            logits[p_l] = []
        if p_e is not None:
            mx = col_max(logits[p_e])
            exps[p_e], psum[p_e] = [], None
        if p_v is not None:
            acc[p_v] = None
        for c in range(n_chunks):
            if p_l is not None:
                logits[p_l].append(_dot_nt(keys(p_l[0], p_l[1], c), qbs[p_l[0]]))
            if p_e is not None:
                e = jnp.exp2(logits[p_e][c] - mx)
                psum[p_e] = add(psum[p_e], e)
                exps[p_e].append(e.astype(BF16))
            if p_v is not None:
                acc[p_v] = add(acc[p_v], _dot(vals_t(p_v[0], c), exps[p_v][c]))
        if p_e is not None:
            del logits[p_e]
        if p_v is not None:
            del exps[p_v]
    outs = []
    for h in range(H_DIFF):
        inv1 = 1.0 / psum[h, 0].sum(axis=0, keepdims=True)
        inv2 = 1.0 / psum[h, 1].sum(axis=0, keepdims=True)
        outs.append((acc[h, 0] * inv1 - acc[h, 1] * (inv2 * lam)).T)
    for hs, y in zip(heads, _norm_rows_many(outs)):
        o_ref[:, hs] = (y * (1.0 - lam_init)).astype(BF16)


def _diff_ctx_body(lam_ref, q_ref, k_ref, v_ref, *rest, lam_init, n_prev):
    o_ref, kc_ref, vc_ref = rest[-3:]
    kc_new, vc_new = kc_ref, vc_ref
    if n_prev:
        kp_ref, vp_ref = rest[0:2]
        for e in range(n_prev):
            kc_ref[e] = kp_ref[e] if n_prev > 1 else kp_ref[...]
            vc_ref[e] = vp_ref[e] if n_prev > 1 else vp_ref[...]
        kc_new, vc_new = kc_ref.at[n_prev], vc_ref.at[n_prev]
    lam = _diff_lambda(lam_ref[...], lam_init)
    heads = [slice(h * 128, (h + 1) * 128) for h in range(H_DIFF)]
    qbs, kms, vbs = [], [], []
    for h, hs in enumerate(heads):
        kk = k_ref[:, hs]
        vv = v_ref[:, hs]
        kc_new[h, 0] = kk[:, 0:DH_DIFF]
        kc_new[h, 1] = kk[:, DH_DIFF:2 * DH_DIFF]
        vc_new[h] = vv
        kms.append(_split_maps(kk))
        vbs.append(vv.astype(BF16))
        qbs.append((q_ref[:, hs] * (DH_DIFF ** -0.5 * LOG2E)).astype(BF16))
    pairs = [(h, m) for h in range(H_DIFF) for m in range(2)]
    logits = {p: _dot_nt(qbs[p[0]], kms[p[0]][p[1]]) for p in pairs}
    exps = {p: jnp.exp2(logits[p] - logits[p].max(axis=-1, keepdims=True)) for p in pairs}
    invs = {p: 1.0 / exps[p].sum(axis=-1, keepdims=True) for p in pairs}
    accs = {p: _dot(exps[p].astype(BF16), vbs[p[0]]) for p in pairs}
    outs = [accs[h, 0] * invs[h, 0] - accs[h, 1] * (invs[h, 1] * lam) for h in range(H_DIFF)]
    for hs, y in zip(heads, _norm_rows_many(outs)):
        o_ref[:, hs] = (y * (1.0 - lam_init)).astype(BF16)


def _diff_attention_ctx(proj, lam_p, lam_init, *, n_seq, seq, prev=None):
    hw = H_DIFF * 128
    n_prev = 0 if prev is None else (1 if prev[0].ndim == 5 else prev[0].shape[1])
    one_k, one_v = (H_DIFF, 2, seq, DH_DIFF), (H_DIFF, seq, DV_DIFF)
    zeros = lambda n: (0,) * n
    in_specs = [pl.BlockSpec((4, DH_DIFF), lambda b: (0, 0)),
                pl.BlockSpec((seq, hw), lambda b: (b, E_DQ // hw)),
                pl.BlockSpec((seq, hw), lambda b: (b, E_DK // hw)),
                pl.BlockSpec((seq, hw), lambda b: (b, E_DV // hw))]
    args = [lam_p, proj, proj, proj]
    if n_prev:
        lead = (None,) if n_prev == 1 else (None, n_prev)
        in_specs += [pl.BlockSpec(lead + one_k, lambda b: (b,) + zeros(len(lead) + 3)),
                     pl.BlockSpec(lead + one_v, lambda b: (b,) + zeros(len(lead) + 2))]
        args += list(prev)
    lead_out = (None, n_prev + 1) if n_prev else (None,)
    out_specs = [pl.BlockSpec((seq, hw), lambda b: (b, 0)),
                 pl.BlockSpec(lead_out + one_k, lambda b: (b,) + zeros(len(lead_out) + 3)),
                 pl.BlockSpec(lead_out + one_v, lambda b: (b,) + zeros(len(lead_out) + 2))]
    out_shape = [jax.ShapeDtypeStruct((n_seq * seq, hw), BF16),
                 jax.ShapeDtypeStruct((n_seq,) + lead_out[1:] + one_k, F32),
                 jax.ShapeDtypeStruct((n_seq,) + lead_out[1:] + one_v, F32)]
    return pl.pallas_call(
        functools.partial(_diff_ctx_body, lam_init=lam_init, n_prev=n_prev),
        grid=(n_seq,),
        in_specs=in_specs,
        out_specs=out_specs,
        out_shape=out_shape,
        compiler_params=_cparams(1),
        name="diff_attention_ctx",
    )(*args)


def _diff_attention_latent(proj, lam_p, lam_init, *, n_seq, seq, cache, rope, cast):
    tq = DIFF_TQ
    nq = seq // tq
    hw = H_DIFF * 128
    ck, cv = cache
    cos, sin = rope
    in_specs = [pl.BlockSpec((4, DH_DIFF), lambda b, i: (0, 0)),
                pl.BlockSpec((tq, hw), lambda b, i: (b * nq + i, E_DQ // hw)),
                pl.BlockSpec((seq, hw), lambda b, i: (b, E_DK // hw)),
                pl.BlockSpec((seq, hw), lambda b, i: (b, E_DV // hw)),
                pl.BlockSpec((None, H_DIFF, 2, PAST_LEN, DH_DIFF), lambda b, i: (b, 0, 0, 0, 0)),
                pl.BlockSpec((None, H_DIFF, PAST_LEN, DV_DIFF), lambda b, i: (b, 0, 0, 0)),
                pl.BlockSpec((tq, 128), lambda b, i: (i, 0)),
                pl.BlockSpec((tq, 128), lambda b, i: (i, 0)),
                pl.BlockSpec((seq, 128), lambda b, i: (0, 0)),
                pl.BlockSpec((seq, 128), lambda b, i: (0, 0))]
    args = [lam_p, proj, proj, proj, ck, cv, cos, sin, cos, sin]
    out_specs = [pl.BlockSpec((tq, hw), lambda b, i: (b * nq + i, 0))]
    out_shape = [jax.ShapeDtypeStruct((n_seq * seq, hw), BF16)]
    w_args, w_in, w_out, w_shape = _weight_cast_plan(cast, n_seq * nq, lambda b, i: b * nq + i)
    scratch = [pltpu.VMEM((H_DIFF, 2, seq, 128), BF16), pltpu.VMEM((H_DIFF, DV_DIFF, seq), BF16),
               pltpu.VMEM((H_DIFF, 2, PAST_LEN, 128), BF16), pltpu.VMEM((H_DIFF, DV_DIFF, PAST_LEN), BF16)]
    return pl.pallas_call(
        functools.partial(_diff_latent_body, lam_init=lam_init, n_cast=len(cast)),
        grid=(n_seq, nq),
        in_specs=in_specs + w_in,
        out_specs=out_specs + w_out,
        out_shape=out_shape + w_shape,
        scratch_shapes=scratch,
        compiler_params=_cparams(2),
        name="diff_attention",
    )(*args, *w_args)


SWA_TQ = 256
SWA_WIN = SWA_TQ + 2 * WINDOW
NEG = -1e30


def _swa_body(*refs, latent, seq, n_prev=0, n_cast=0):
    tq = SWA_TQ
    if latent:
        q_ref, k_ref, v_ref, sink_ref, ck_ref, cv_ref, cq_ref, sq_ref, ckk_ref, skk_ref = refs[:10]
        w_src, o_ref, w_dst = refs[10:10 + n_cast], refs[10 + n_cast], refs[11 + n_cast:11 + 2 * n_cast]
        kr_ref, cvt_ref = refs[11 + 2 * n_cast:]
        _cast_weight_chunks(w_src, w_dst)
        qi = pl.program_id(1)

        @pl.when(qi == 0)
        def _():
            kr_ref[...] = _rope128(k_ref[...], ckk_ref[...], skk_ref[...]).astype(BF16)
            cvt_ref[...] = jnp.concatenate([cv_ref[0], cv_ref[1]], axis=1).T.astype(BF16)

        cq, sq = cq_ref[...], sq_ref[...]
        q = jnp.concatenate([_rope128(q_ref[:, g * 128:(g + 1) * 128], cq, sq)
                             for g in range(H_SWA * DH_SWA // 128)], axis=1)
        ws = pl.multiple_of(jnp.clip(qi * tq - WINDOW, 0, seq - SWA_WIN), 128)
        kw = kr_ref[pl.ds(ws, SWA_WIN), :]
        vwt = v_ref[pl.ds(ws, SWA_WIN), :].T.astype(BF16)
        kpos = ws + lax.broadcasted_iota(jnp.int32, (SWA_WIN, tq), 0)
        qpos = qi * tq + lax.broadcasted_iota(jnp.int32, (SWA_WIN, tq), 1)
        bias1 = jnp.where(jnp.abs(qpos - kpos) <= WINDOW, 0.0, NEG)
        bias = jnp.concatenate([bias1] * SWA_GROUP, axis=1)
    else:
        q_ref, k_ref, v_ref, sink_ref = refs[:4]
        o_ref, kc_ref, vc_ref = refs[-3:]
        q = q_ref[...]
        kk = k_ref[...]
        vv = v_ref[...]
        kc_new, vc_new = kc_ref, vc_ref
        if n_prev:
            kp_ref, vp_ref = refs[4:6]
            for e in range(n_prev):
                kc_ref[e] = kp_ref[e] if n_prev > 1 else kp_ref[...]
                vc_ref[e] = vp_ref[e] if n_prev > 1 else vp_ref[...]
            kc_new, vc_new = kc_ref.at[n_prev], vc_ref.at[n_prev]
        for kv in range(KV_SWA):
            kc_new[kv] = kk[:, kv * DH_SWA:(kv + 1) * DH_SWA]
            vc_new[kv] = vv[:, kv * DH_SWA:(kv + 1) * DH_SWA]
        kw = kk.astype(BF16)
        vwt = vv.T.astype(BF16)
    q = q * (DH_SWA ** -0.5 * LOG2E)
    lanes = [slice(kv * DH_SWA, (kv + 1) * DH_SWA) for kv in range(KV_SWA)]

    def logits(kv):
        q4 = jnp.concatenate([q[:, (kv * SWA_GROUP + g) * DH_SWA:(kv * SWA_GROUP + g + 1) * DH_SWA]
                              for g in range(SWA_GROUP)], axis=0).astype(BF16)
        s_loc = _dot_nt(kw[:, lanes[kv]], q4)
        if latent:
            return [_dot_nt(ck_ref[kv].astype(BF16), q4), s_loc + bias]
        return [s_loc]

    def values(kv, es):
        vals_t = [cvt_ref[lanes[kv], :], vwt[lanes[kv], :]] if latent else [vwt[lanes[kv], :]]
        acc = None
        for e, vt in zip(es, vals_t):
            t = _dot(vt, e.astype(BF16))
            acc = t if acc is None else acc + t
        return acc

    parts = logits(0)
    accs = []
    for kv in range(KV_SWA):
        nxt = logits(kv + 1) if kv + 1 < KV_SWA else None
        es, inv = _col_softmax(parts, extra=sink_ref[kv] * LOG2E)
        accs.append((es, inv))
        parts = nxt
    pieces = []
    for kv, (es, inv) in enumerate(accs):
        acc = values(kv, es) * inv
        pieces += [acc[:, g * tq:(g + 1) * tq] for g in range(SWA_GROUP)]
    o_ref[...] = jnp.concatenate(pieces, axis=0).T.astype(BF16)


def _swa_attention(proj, sink, *, n_seq, seq, cache=None, rope=None, cast=None, prev=None):
    latent = cache is not None
    n_prev = 0
    tq = SWA_TQ
    nq = seq // tq
    hq = H_SWA * DH_SWA
    sink_b = jnp.repeat(sink.reshape(KV_SWA, SWA_GROUP), tq, axis=1).reshape(KV_SWA, 1, SWA_GROUP * tq)
    in_specs = [pl.BlockSpec((tq, hq), lambda b, i: (b * nq + i, 0)),
                pl.BlockSpec((seq, 128), lambda b, i: (b, O_SK // 128)),
                pl.BlockSpec((seq, 128), lambda b, i: (b, O_SV // 128)),
                pl.BlockSpec((KV_SWA, 1, SWA_GROUP * tq), lambda b, i: (0, 0, 0))]
    args = [proj, proj, proj, sink_b]
    scratch = []
    out_specs = [pl.BlockSpec((tq, hq), lambda b, i: (b * nq + i, 0))]
    out_shape = [jax.ShapeDtypeStruct((n_seq * seq, hq), BF16)]
    if latent:
        ck, cv = cache
        cos, sin = rope
        in_specs += [pl.BlockSpec((None, KV_SWA, PAST_LEN, DH_SWA), lambda b, i: (b, 0, 0, 0)),
                     pl.BlockSpec((None, KV_SWA, PAST_LEN, DH_SWA), lambda b, i: (b, 0, 0, 0)),
                     pl.BlockSpec((tq, 128), lambda b, i: (i, 0)),
                     pl.BlockSpec((tq, 128), lambda b, i: (i, 0)),
                     pl.BlockSpec((seq, 128), lambda b, i: (0, 0)),
                     pl.BlockSpec((seq, 128), lambda b, i: (0, 0))]
        args += [ck, cv, cos, sin, cos, sin]
        w_args, w_in, w_out, w_shape = _weight_cast_plan(cast, n_seq * nq, lambda b, i: b * nq + i)
        args += w_args
        in_specs += w_in
        out_specs += w_out
        out_shape += w_shape
        scratch = [pltpu.VMEM((seq, 128), BF16), pltpu.VMEM((KV_SWA * DH_SWA, PAST_LEN), BF16)]
    else:
        n_prev = 0 if prev is None else (1 if prev[0].ndim == 4 else prev[0].shape[1])
        one = (KV_SWA, seq, DH_SWA)
        if n_prev:
            lead = (None,) if n_prev == 1 else (None, n_prev)
            in_specs += [pl.BlockSpec(lead + one, lambda b, i: (b,) + (0,) * (len(lead) + 2))] * 2
            args += list(prev)
        lead_out = (None, n_prev + 1) if n_prev else (None,)
        cache_spec = pl.BlockSpec(lead_out + one, lambda b, i: (b,) + (0,) * (len(lead_out) + 2))
        cache_shape = jax.ShapeDtypeStruct((n_seq,) + lead_out[1:] + one, F32)
        out_specs += [cache_spec, cache_spec]
        out_shape += [cache_shape, cache_shape]
    return pl.pallas_call(
        functools.partial(_swa_body, latent=latent, seq=seq, n_prev=n_prev, n_cast=len(cast or ())),
        grid=(n_seq, nq),
        in_specs=in_specs,
        out_specs=out_specs,
        out_shape=out_shape,
        scratch_shapes=scratch,
        compiler_params=_cparams(2),
        name="swa_attention",
    )(*args)


GLA_BLOCK = 128
GLA_LEVELS = 7
GLA_SAFE = 60.0
GLA_GROUP = 4
GLA_GATE_ROWS = 1024


def _split3(x):
    hi = x.astype(BF16)
    r1 = x - hi.astype(F32)
    mid = r1.astype(BF16)
    lo = (r1 - mid.astype(F32)).astype(BF16)
    return hi, mid, lo


def _gla_gate_body(r_ref, w2_ref, b2_ref, x_ref, tot_ref):
    T = GLA_BLOCK
    ti = lax.broadcasted_iota(jnp.int32, (T, T), 0)
    si = lax.broadcasted_iota(jnp.int32, (T, T), 1)
    r_hi, r_mid, _ = _split3(r_ref[...])
    zs = []
    for d in range(2):
        w_hi, w_mid, _ = _split3(w2_ref[d])
        zs.append(b2_ref[d] + _dot(r_hi, w_hi) + _dot(r_hi, w_mid) + _dot(r_mid, w_hi))
    las = [_split3((jnp.minimum(z, 0.0) - jnp.log(1.0 + jnp.exp(-jnp.abs(z)))) * (1.0 / GLA_TAU)) for z in zs]
    for d in range(2):
        tri = jnp.where((si >= ti) if d else (si <= ti), 1.0, 0.0).astype(BF16)
        l_hi, l_mid, l_lo = las[d]
        for j in range(GLA_GATE_ROWS // T):
            rows = slice(j * T, (j + 1) * T)
            x = _dot(tri, l_hi[rows]) + _dot(tri, l_mid[rows]) + _dot(tri, l_lo[rows])
            x_ref[d, rows, :] = x
            tot_ref[d, j:j + 1, :] = x[0:1, :] if d else x[T - 1:T, :]


def _gla_gates(proj, w2, b2):
    n_rows = proj.shape[0]
    hk = H_GLA * DK_GLA
    tr = GLA_GATE_ROWS
    w2p = jnp.zeros((2, 128, hk), F32)
    w2p = w2p.at[0, 0:GLA_RANK].set(w2[0]).at[1, GLA_RANK:2 * GLA_RANK].set(w2[1])
    return pl.pallas_call(
        _gla_gate_body,
        grid=(n_rows // tr,),
        in_specs=[pl.BlockSpec((tr, 128), lambda i: (i, O_GLR // 128)),
                  pl.BlockSpec((2, 128, hk), lambda i: (0, 0, 0)),
                  pl.BlockSpec((2, 1, hk), lambda i: (0, 0, 0))],
        out_specs=[pl.BlockSpec((2, tr, hk), lambda i: (0, i, 0)),
                   pl.BlockSpec((2, tr // GLA_BLOCK, hk), lambda i: (0, i, 0))],
        out_shape=[jax.ShapeDtypeStruct((2, n_rows, hk), F32),
                   jax.ShapeDtypeStruct((2, n_rows // GLA_BLOCK, hk), F32)],
        compiler_params=_cparams(1),
        name="gla_gates",
    )(proj, w2p, b2.reshape(2, 1, hk))


def _gla_body(*refs, rev, n_blocks, finish):
    if finish:
        (flag_ref, q_ref, k_ref, v_ref, x_ref, s0_ref, of_ref, g_ref, o_ref, sfin_ref, st_ref, att_ref) = refs
    else:
        flag_ref, q_ref, k_ref, v_ref, x_ref, s0_ref, o_ref, sfin_ref, st_ref, att_ref = refs
    n = pl.program_id(1)
    blk = (n_blocks - 1 - n) if rev else n
    safe = flag_ref[pl.program_id(0) * n_blocks + blk] != 0
    T = GLA_BLOCK
    G = q_ref.shape[0]
    hk = H_GLA * DK_GLA
    heads = [slice(h * DK_GLA, (h + 1) * DK_GLA) for h in range(H_GLA)]
    scale = DK_GLA ** -0.5
    ti = lax.broadcasted_iota(jnp.int32, (T, T), 0)
    si = lax.broadcasted_iota(jnp.int32, (T, T), 1)
    causal = (si >= ti) if rev else (si <= ti)

    @pl.when(n == 0)
    def _():
        st_ref[...] = s0_ref[...]

    @pl.when(safe)
    def _():
        for j in range(G):
            x = x_ref[j]
            qs = (q_ref[j] * scale * jnp.exp(x)).astype(BF16)
            ks = (k_ref[j] * jnp.exp(-x)).astype(BF16)
            for h, hs in enumerate(heads):
                att_ref[j, h] = jnp.where(causal, _dot_nt(qs[:, hs], ks[:, hs]), 0.0)

    @pl.when(jnp.logical_not(safe))
    def _():
        t_idx = lax.broadcasted_iota(jnp.int32, (T, hk), 0)
        xr = ti ^ si
        for j in range(G):
            q = q_ref[j] * scale
            k = k_ref[j]
            x = x_ref[j]
            if rev:
                la = jnp.where(t_idx == T - 1, x, x - pltpu.roll(x, T - 1, 0))
            else:
                la = jnp.where(t_idx == 0, x, x - pltpu.roll(x, 1, 0))
            qb, kb = q.astype(BF16), k.astype(BF16)
            att = [_dot_nt(qb[:, hs], kb[:, hs]) for hs in heads]
            xg = la
            tg = la
            for lvl in range(GLA_LEVELS):
                sz = 1 << lvl
                upper = ((t_idx >> lvl) & 1) == 1
                is_q = jnp.logical_not(upper) if rev else upper
                partner = jnp.where(upper, pltpu.roll(tg, sz, 0), pltpu.roll(tg, T - sz, 0))
                e = jnp.exp(jnp.where(is_q, xg, tg - xg))
                qs = jnp.where(is_q, q * e, 0.0).astype(BF16)
                ks = jnp.where(is_q, 0.0, k * e).astype(BF16)
                for h, hs in enumerate(heads):
                    att[h] = jnp.where(xr >= sz, _dot_nt(qs[:, hs], ks[:, hs]), att[h])
                xg = xg + jnp.where(is_q, partner, 0.0)
                tg = tg + partner
            for h in range(H_GLA):
                att_ref[j, h] = att[h]

    outs = {}
    for j in range(G):
        x = x_ref[j]
        tot = x[0:1, :] if rev else x[T - 1:T, :]
        v = v_ref[j]
        qe = (q_ref[j] * scale * jnp.exp(x)).astype(BF16)
        kw = (k_ref[j] * jnp.exp(tot - x)).astype(BF16)
        vt = v.T
        vb = v.astype(BF16)
        dec = jnp.exp(tot)
        for h, hs in enumerate(heads):
            vs = slice(h * DV_GLA, (h + 1) * DV_GLA)
            st = st_ref[j, h]
            o = _dot(att_ref[j, h].astype(BF16), vb[:, vs]) + _dot_nt(qe[:, hs], st.astype(BF16))
            if finish:
                outs[j, h] = o + of_ref[j, :, vs]
            else:
                o_ref[j, :, vs] = o
            st_ref[j, h] = st * dec[:, hs] + _dot(vt[vs, :].astype(BF16), kw[:, hs])
    if finish:
        keys = sorted(outs)
        normed = _norm_rows_many([outs[k] for k in keys])
        for (j, h), y in zip(keys, normed):
            vs = slice(h * DV_GLA, (h + 1) * DV_GLA)
            o_ref[j, :, vs] = (y * _silu(g_ref[j, :, vs])).astype(BF16)

    @pl.when(n == n_blocks - 1)
    def _():
        sfin_ref[...] = st_ref[...]


def _gla(proj, w2, b2, s0t, *, n_seq, seq):
    nb = seq // GLA_BLOCK
    G = min(GLA_GROUP, n_seq)
    hk, hv = H_GLA * DK_GLA, H_GLA * DV_GLA
    x, tot = _gla_gates(proj, w2, b2)
    x = x.reshape(2, n_seq, seq, hk)
    safe = (tot.min(axis=-1) > -GLA_SAFE).reshape(2, n_seq // G, G, nb).all(axis=2)
    flags = safe.astype(jnp.int32).reshape(2, (n_seq // G) * nb)
    p3 = proj.reshape(n_seq, seq, ODD_IN_PAD)
    state_shape = jax.ShapeDtypeStruct((n_seq, H_GLA, DV_GLA, DK_GLA), F32)
    state_spec = pl.BlockSpec((G, H_GLA, DV_GLA, DK_GLA), lambda g, n, f: (g, 0, 0, 0))
    scratch = [pltpu.VMEM((G, H_GLA, DV_GLA, DK_GLA), F32), pltpu.VMEM((G, H_GLA, GLA_BLOCK, GLA_BLOCK), F32)]

    def run(rev, extra_args, extra_cols, out_dtype):
        blk = (lambda n: nb - 1 - n) if rev else (lambda n: n)
        d = 1 if rev else 0
        tok = lambda width, col: pl.BlockSpec((G, GLA_BLOCK, width), lambda g, n, f: (g, blk(n), col))
        grid_spec = pltpu.PrefetchScalarGridSpec(
            num_scalar_prefetch=1,
            grid=(n_seq // G, nb),
            in_specs=[tok(hk, O_GQ // hk), tok(hk, O_GK // hk), tok(hv, O_GV // hv),
                      pl.BlockSpec((None, G, GLA_BLOCK, hk), lambda g, n, f: (d, g, blk(n), 0)),
                      pl.BlockSpec((None, G, H_GLA, DV_GLA, DK_GLA), lambda g, n, f: (d, g, 0, 0, 0))]
            + [tok(hv, col) for col in extra_cols],
            out_specs=[tok(hv, 0), state_spec],
            scratch_shapes=scratch)
        return pl.pallas_call(
            functools.partial(_gla_body, rev=rev, n_blocks=nb, finish=rev),
            grid_spec=grid_spec,
            out_shape=[jax.ShapeDtypeStruct((n_seq, seq, hv), out_dtype), state_shape],
            compiler_params=_cparams(2),
            name="gla_bwd" if rev else "gla_fwd",
        )(flags[d], p3, p3, p3, x, s0t, *extra_args)

    o_f, s_f = run(False, [], [], F32)
    out, s_b = run(True, [o_f, p3], [0, O_GR // hv], BF16)
    return out.reshape(n_seq * seq, hv), jnp.stack([s_f, s_b], axis=0)


def _stacked(caches, n_layers):
    return [c[:, None] for c in caches] if n_layers == 1 else caches


def kernel(x_prompt, x_sample, c, state_ret, cache_diff_k, cache_diff_v, cache_swa_k, cache_swa_v, state_gla,
           c_ctx, w_mod, b_mod, ln_g, ln_b, w_in_even, w_out_even, ret_decay, diff_lam, w_in_odd, w_out_odd,
           swa_sink, gla_w2, gla_b, w_ff1, w_ff2):
    xc = x_prompt.reshape(N_CTX, D_MODEL)
    xl = x_sample.reshape(N_LAT, D_MODEL)
    cond = jnp.concatenate([c_ctx[None, :], c, jnp.zeros((N_COND - 1 - DEC_BATCH, D_MODEL), F32)], axis=0)
    mod = _modulation(cond, w_mod, b_mod).reshape(DEPTH, N_COND, 1, 6 * D_MODEL)
    rope = _rope_tables(DEC_SEQ // GRID_W, DH_DIFF)

    w_in_even_b = w_in_even.astype(BF16)
    w_in_odd_b = _reorder_odd(w_in_odd.astype(BF16))

    ctx = dict(n_seq=BATCH, seq=SEQ)
    lat = dict(n_seq=DEC_BATCH, seq=DEC_SEQ)
    new_ret, new_gla = [], []
    new_diff = new_swa = None
    for l in range(DEPTH):
        if l % 2 == 0:
            e = l // 2
            lam_init = 0.8 - 0.6 * math.exp(-0.3 * l)
            pc = _in_projection(xc, mod[l], w_in_even_b[e], False)
            pl_ = _in_projection(xl, mod[l], w_in_even_b[e], True)
            zero = jnp.zeros((BATCH, 2, H_RET, DK_RET, DV_RET), F32)
            a_ctx, s_ret = _retention(pc, ret_decay[e], zero, **ctx)
            a_lat, _ = _retention(pl_, ret_decay[e], state_ret[:, e], **lat)
            b_ctx, *new_diff = _diff_attention_ctx(pc, diff_lam[e], lam_init, prev=new_diff, **ctx)
            b_lat, *w_post = _diff_attention_latent(pl_, diff_lam[e], lam_init,
                                                    cache=(cache_diff_k[:, e], cache_diff_v[:, e]), rope=rope,
                                                    cast=[(w_out_even, e), (w_ff1, l), (w_ff2, l)], **lat)
            new_ret.append(s_ret)
        else:
            o = l // 2
            pc = _in_projection(xc, mod[l], w_in_odd_b[o], False)
            pl_ = _in_projection(xl, mod[l], w_in_odd_b[o], True)
            a_ctx, *new_swa = _swa_attention(pc, swa_sink[o], prev=new_swa, **ctx)
            a_lat, *w_post = _swa_attention(pl_, swa_sink[o], cache=(cache_swa_k[:, o], cache_swa_v[:, o]),
                                            rope=rope, cast=[(w_out_odd, o), (w_ff1, l), (w_ff2, l)], **lat)
            zero = jnp.zeros((2, BATCH, H_GLA, DV_GLA, DK_GLA), F32)
            b_ctx, s_gla = _gla(pc, gla_w2[o], gla_b[o], zero, **ctx)
            s0t = state_gla[:, o].transpose(1, 0, 2, 4, 3)
            b_lat, _ = _gla(pl_, gla_w2[o], gla_b[o], s0t, **lat)
            new_gla.append(s_gla.transpose(1, 0, 2, 4, 3))
        w_out_b, w1_b, w2_b = w_post
        xc = _post_mixer(xc, a_ctx, b_ctx, mod[l], w_out_b, ln_g, ln_b, w1_b, w2_b, l, False)
        xl = _post_mixer(xl, a_lat, b_lat, mod[l], w_out_b, ln_g, ln_b, w1_b, w2_b, l, True)
    return (xc.reshape(BATCH, SEQ, D_MODEL), xl.reshape(DEC_BATCH, DEC_SEQ, D_MODEL),
            jnp.stack(new_ret, axis=1), *_stacked(new_diff, N_EVEN), *_stacked(new_swa, N_ODD),
            jnp.stack(new_gla, axis=1))
```

```python
import functools
import math

import jax
import jax.numpy as jnp
from jax import lax
from jax.experimental import pallas as pl
from jax.experimental.pallas import tpu as pltpu

F32 = jnp.float32
BF16 = jnp.bfloat16

D_MODEL = 1024
BATCH = 16
SEQ = 256
DEPTH = 4
DEC_BATCH = 4
DEC_SEQ = 2048
PAST_LEN = 256
GRID_W = 64
H_RET, DK_RET, DV_RET = 4, 64, 128
H_DIFF, DH_DIFF, DV_DIFF = 4, 64, 128
H_SWA, KV_SWA, DH_SWA = 8, 2, 64
SWA_GROUP = H_SWA // KV_SWA
WINDOW = 128
H_GLA, DK_GLA, DV_GLA = 4, 64, 128
GLA_RANK = 16
GLA_TAU = 16.0
D_FF = 4 * D_MODEL
ROPE_BASE = 10000.0
N_EVEN = (DEPTH + 1) // 2
N_ODD = DEPTH // 2
ALPHA = (2 * DEPTH) ** 0.25
EVEN_IN = 3072
ODD_IN = 2336
ODD_IN_PAD = 2432
EPS = 1e-5
LOG2E = 1.4426950408889634

E_RQ, E_RK, E_RV, E_RG, E_DQ, E_DK, E_DV = 0, 256, 512, 1024, 1536, 2048, 2560
O_SQ, O_GV, O_GR, O_GQ, O_GK, O_SK, O_SV, O_GLR = 0, 512, 1024, 1536, 1792, 2048, 2176, 2304


def _reorder_odd(w):
    sq, sk, sv, gq, gk, gv, gr, glr = jnp.split(w, [512, 640, 768, 1024, 1280, 1792, 2304], axis=-1)
    pad = jnp.zeros(w.shape[:-1] + (ODD_IN_PAD - ODD_IN,), w.dtype)
    return jnp.concatenate([sq, gv, gr, gq, gk, sk, sv, glr, pad], axis=-1)


N_CTX = BATCH * SEQ
N_LAT = DEC_BATCH * DEC_SEQ
DENSE_TILE = 1024
POST_TILE = 1024
N_COND = 8

VMEM_LIMIT = 56 * 1024 * 1024


def _cparams(n_axes):
    return pltpu.CompilerParams(dimension_semantics=("arbitrary",) * n_axes,
                                vmem_limit_bytes=VMEM_LIMIT)


def _dot(a, b):
    return jnp.dot(a, b, preferred_element_type=F32)


def _dot_nt(a, b):
    return lax.dot_general(a, b, (((1,), (1,)), ((), ())), preferred_element_type=F32)


def _silu(x):
    return x * (1.0 / (1.0 + jnp.exp(-x)))


def _norm_rows(x):
    mu = jnp.mean(x, axis=-1, keepdims=True)
    xc = x - mu
    var = jnp.mean(xc * xc, axis=-1, keepdims=True)
    return xc * lax.rsqrt(var + EPS)


def _norm_rows_many(xs):
    mus = [jnp.mean(x, axis=-1, keepdims=True) for x in xs]
    xcs = [x - mu for x, mu in zip(xs, mus)]
    vs = [jnp.mean(xc * xc, axis=-1, keepdims=True) for xc in xcs]
    return [xc * lax.rsqrt(v + EPS) for xc, v in zip(xcs, vs)]


def _mod_row(i, latent, tile):
    return 1 + i // (DEC_SEQ // tile) if latent else 0


def _resident(shape, index_map):
    return pl.BlockSpec(shape, index_map, pipeline_mode=pl.Buffered(1))


def _mod_body(c_ref, w_ref, b_ref, o_ref):
    c = _silu(c_ref[...]).astype(BF16)
    o_ref[...] = _dot(c, w_ref[...].astype(BF16)) + b_ref[...]


def _modulation(cond, w_mod, b_mod):
    tn = 1536
    return pl.pallas_call(
        _mod_body,
        grid=(DEPTH, 6 * D_MODEL // tn),
        in_specs=[pl.BlockSpec((N_COND, D_MODEL), lambda l, j: (0, 0)),
                  pl.BlockSpec((None, D_MODEL, tn), lambda l, j: (l, 0, j)),
                  pl.BlockSpec((None, 1, tn), lambda l, j: (l, 0, j))],
        out_specs=pl.BlockSpec((None, N_COND, tn), lambda l, j: (l, 0, j)),
        out_shape=jax.ShapeDtypeStruct((DEPTH, N_COND, 6 * D_MODEL), F32),
        compiler_params=_cparams(2),
        name="modulation",
    )(cond, w_mod, b_mod.reshape(DEPTH, 1, 6 * D_MODEL))


def _inproj_body(x_ref, mod_ref, w_ref, o_ref):
    sh = mod_ref[:, 0:D_MODEL]
    sc = mod_ref[:, D_MODEL:2 * D_MODEL]
    h = (x_ref[...] * (1.0 + sc) + sh).astype(BF16)
    o_ref[...] = _dot(h, w_ref[...])


def _in_projection(x, mod_l, w, latent):
    n_in = w.shape[1]
    n_rows = x.shape[0]
    tm = DENSE_TILE
    return pl.pallas_call(
        _inproj_body,
        grid=(n_rows // tm,),
        in_specs=[pl.BlockSpec((tm, D_MODEL), lambda i: (i, 0)),
                  pl.BlockSpec((None, 1, 6 * D_MODEL), lambda i: (_mod_row(i, latent, tm), 0, 0)),
                  _resident((D_MODEL, n_in), lambda i: (0, 0))],
        out_specs=pl.BlockSpec((tm, n_in), lambda i: (i, 0)),
        out_shape=jax.ShapeDtypeStruct((n_rows, n_in), F32),
        compiler_params=_cparams(1),
        name="in_projection",
    )(x, mod_l, w)


def _layer_norm(x, g, b):
    return _norm_rows(x) * g + b


def _post_body(x_ref, ma_ref, mb_ref, mod_ref, wo_ref, g_ref, b_ref, w1_ref, w2_ref, o_ref):
    half = D_MODEL // 2
    gt1 = mod_ref[:, 2 * D_MODEL:3 * D_MODEL]
    sh2 = mod_ref[:, 3 * D_MODEL:4 * D_MODEL]
    sc2 = mod_ref[:, 4 * D_MODEL:5 * D_MODEL]
    gt2 = mod_ref[:, 5 * D_MODEL:6 * D_MODEL]
    chunk = 1024
    n_chunks = D_FF // chunk
    top, bot = slice(0, POST_TILE // 2), slice(POST_TILE // 2, POST_TILE)

    def mix(r):
        return _dot(ma_ref[r, :], wo_ref[0:half, :]) + _dot(mb_ref[r, :], wo_ref[half:D_MODEL, :])

    def norm1(r, m):
        x1 = _layer_norm(ALPHA * x_ref[r, :] + gt1 * m, g_ref[0:1, :], b_ref[0:1, :])
        return x1, (x1 * (1.0 + sc2) + sh2).astype(BF16)

    def ffn(hf, j):
        h1 = _dot(hf, w1_ref[:, j * chunk:(j + 1) * chunk])
        h1 = jnp.square(jnp.maximum(h1, 0.0)).astype(BF16)
        return _dot(h1, w2_ref[j * chunk:(j + 1) * chunk, :])

    def norm2(r, x1, ff):
        o_ref[r, :] = _layer_norm(ALPHA * x1 + gt2 * ff, g_ref[1:2, :], b_ref[1:2, :])

    m_top, m_bot = mix(top), mix(bot)
    x1_top, hf_top = norm1(top, m_top)
    ff_top = ffn(hf_top, 0)
    x1_bot, hf_bot = norm1(bot, m_bot)
    for j in range(1, n_chunks):
        ff_top = ff_top + ffn(hf_top, j)
    ff_bot = ffn(hf_bot, 0)
    norm2(top, x1_top, ff_top)
    for j in range(1, n_chunks):
        ff_bot = ff_bot + ffn(hf_bot, j)
    norm2(bot, x1_bot, ff_bot)


def _post_mixer(x, mix_a, mix_b, mod_l, w_out, ln_g, ln_b, w1, w2, layer, latent):
    half = D_MODEL // 2
    n_rows = x.shape[0]
    tm = POST_TILE
    return pl.pallas_call(
        _post_body,
        grid=(n_rows // tm,),
        in_specs=[pl.BlockSpec((tm, D_MODEL), lambda i: (i, 0)),
                  pl.BlockSpec((tm, half), lambda i: (i, 0)),
                  pl.BlockSpec((tm, half), lambda i: (i, 0)),
                  pl.BlockSpec((None, 1, 6 * D_MODEL), lambda i: (_mod_row(i, latent, tm), 0, 0)),
                  _resident((D_MODEL, D_MODEL), lambda i: (0, 0)),
                  _resident((None, 2, D_MODEL), lambda i: (layer, 0, 0)),
                  _resident((None, 2, D_MODEL), lambda i: (layer, 0, 0)),
                  _resident((D_MODEL, D_FF), lambda i: (0, 0)),
                  _resident((D_FF, D_MODEL), lambda i: (0, 0))],
        out_specs=pl.BlockSpec((tm, D_MODEL), lambda i: (i, 0)),
        out_shape=jax.ShapeDtypeStruct((n_rows, D_MODEL), F32),
        compiler_params=_cparams(1),
        name="post_mixer",
    )(x, mix_a, mix_b, mod_l, w_out, ln_g, ln_b, w1, w2)


def _rope_tables(rows, dim):
    row = jnp.repeat(jnp.arange(rows), GRID_W).astype(F32)
    col = jnp.tile(jnp.arange(GRID_W), rows).astype(F32)
    half = dim // 2
    freqs = ROPE_BASE ** (-jnp.arange(0, half, 2, dtype=F32) / half)
    ar = row[:, None] * freqs
    ac = col[:, None] * freqs
    cr, sr, cc, sc = jnp.cos(ar), jnp.sin(ar), jnp.cos(ac), jnp.sin(ac)
    cos = jnp.concatenate([cr, cr, cc, cc], axis=-1)
    sin = jnp.concatenate([-sr, sr, -sc, sc], axis=-1)
    reps = 128 // dim
    return jnp.tile(cos, (1, reps)), jnp.tile(sin, (1, reps))


def _rope128(x, cos, sin):
    lane = lax.broadcasted_iota(jnp.int32, x.shape, 1)
    first = (lane & 31) < 16
    swapped = jnp.where(first, pltpu.roll(x, 112, 1), pltpu.roll(x, 16, 1))
    return x * cos + swapped * sin


RET_CHUNK = 256


def _ret_body(q_ref, k_ref, v_ref, g_ref, rdk_ref, rdh_ref, s0_ref, o_ref, sfin_ref, *, seq):
    C = RET_CHUNK
    n_chunks = seq // C
    scale = DK_RET ** -0.5
    ii = lax.broadcasted_iota(jnp.int32, (C, C), 0)
    jj = lax.broadcasted_iota(jnp.int32, (C, C), 1)
    dist = (ii - jj).astype(F32)
    adist = jnp.abs(dist)
    pos = lax.broadcasted_iota(jnp.int32, (C, H_RET * DK_RET), 0).astype(F32)
    lgf = -jnp.exp(rdk_ref[0])
    lgb = -jnp.exp(rdk_ref[1])
    k_wf = jnp.exp(lgf * (C - 1.0 - pos)) * scale
    k_wb = jnp.exp(lgb * pos) * scale
    q_wf = jnp.exp(lgf * (pos + 1.0))
    q_wb = jnp.exp(lgb * (C - pos))

    uf, ub = [], []
    for n in range(n_chunks):
        kn = k_ref[n * C:(n + 1) * C, :]
        kft = (kn * k_wf).T.astype(BF16)
        kbt = (kn * k_wb).T.astype(BF16)
        ufn, ubn = [], []
        for h in range(H_RET):
            vh = v_ref[n * C:(n + 1) * C, h * DV_RET:(h + 1) * DV_RET].astype(BF16)
            ufn.append(_dot(kft[h * DK_RET:(h + 1) * DK_RET, :], vh))
            ubn.append(_dot(kbt[h * DK_RET:(h + 1) * DK_RET, :], vh))
        uf.append(ufn)
        ub.append(ubn)

    for h in range(H_RET):
        lgf_h = -jnp.exp(rdh_ref[0, h])
        lgb_h = -jnp.exp(rdh_ref[1, h])
        cf = jnp.exp(lgf_h * float(C))[:, 0:DV_RET]
        cb = jnp.exp(lgb_h * float(C))[:, 0:DV_RET]
        dmat = jnp.exp(jnp.where(dist > 0, lgf_h, lgb_h) * adist) + jnp.where(dist == 0, 1.0, 0.0)
        sf = [s0_ref[0, h]]
        for n in range(n_chunks):
            sf.append(cf * sf[n] + uf[n][h])
        sb = [None] * n_chunks
        sb[n_chunks - 1] = s0_ref[1, h]
        for n in range(n_chunks - 1, 0, -1):
            sb[n - 1] = cb * sb[n] + ub[n][h]
        sfin_ref[0, h] = sf[n_chunks]
        sfin_ref[1, h] = cb * sb[0] + ub[0][h]
        lanes = slice(h * DK_RET, (h + 1) * DK_RET)
        vlanes = slice(h * DV_RET, (h + 1) * DV_RET)
        outs = []
        for n in range(n_chunks):
            rows = slice(n * C, (n + 1) * C)
            qh = q_ref[rows, lanes]
            kh = (k_ref[rows, lanes] * scale).astype(BF16)
            vh = v_ref[rows, vlanes].astype(BF16)
            s = _dot_nt(qh.astype(BF16), kh) * dmat
            o = _dot(s.astype(BF16), vh)
            o = o + _dot((qh * q_wf[:, lanes]).astype(BF16), sf[n].astype(BF16))
            outs.append(o + _dot((qh * q_wb[:, lanes]).astype(BF16), sb[n].astype(BF16)))
        for n, y in enumerate(_norm_rows_many(outs)):
            rows = slice(n * C, (n + 1) * C)
            o_ref[rows, vlanes] = (y * _silu(g_ref[rows, vlanes])).astype(BF16)


def _retention(proj, ret_decay_e, s0, *, n_seq, seq):
    rdk = jnp.repeat(ret_decay_e, DK_RET, axis=-1).reshape(2, 1, H_RET * DK_RET)
    rdh = jnp.broadcast_to(ret_decay_e[:, :, None, None], (2, H_RET, 1, RET_CHUNK))
    hk, hv = H_RET * DK_RET, H_RET * DV_RET
    return pl.pallas_call(
        functools.partial(_ret_body, seq=seq),
        grid=(n_seq,),
        in_specs=[pl.BlockSpec((seq, hk), lambda b: (b, E_RQ // hk)),
                  pl.BlockSpec((seq, hk), lambda b: (b, E_RK // hk)),
                  pl.BlockSpec((seq, hv), lambda b: (b, E_RV // hv)),
                  pl.BlockSpec((seq, hv), lambda b: (b, E_RG // hv)),
                  pl.BlockSpec((2, 1, hk), lambda b: (0, 0, 0)),
                  pl.BlockSpec((2, H_RET, 1, RET_CHUNK), lambda b: (0, 0, 0, 0)),
                  pl.BlockSpec((None, 2, H_RET, DK_RET, DV_RET), lambda b: (b, 0, 0, 0, 0))],
        out_specs=[pl.BlockSpec((seq, hv), lambda b: (b, 0)),
                   pl.BlockSpec((None, 2, H_RET, DK_RET, DV_RET), lambda b: (b, 0, 0, 0, 0))],
        out_shape=[jax.ShapeDtypeStruct((n_seq * seq, hv), BF16),
                   jax.ShapeDtypeStruct((n_seq, 2, H_RET, DK_RET, DV_RET), F32)],
        compiler_params=_cparams(1),
        name="retention",
    )(proj, proj, proj, proj, rdk, rdh, s0)


def _cast_weight_chunks(src_refs, dst_refs):
    for src, dst in zip(src_refs, dst_refs):
        dst[...] = src[...].astype(BF16)


def _weight_cast_plan(weights, n_chunks, chunk_of):
    args, in_specs, out_specs, out_shapes = [], [], [], []
    for w, idx in weights:
        rows, cols = w.shape[1], w.shape[2]
        rb = rows // n_chunks
        assert rb * n_chunks == rows and rb % 16 == 0
        args.append(w)
        in_specs.append(pl.BlockSpec((None, rb, cols), lambda *g, idx=idx: (idx, chunk_of(*g), 0)))
        out_specs.append(pl.BlockSpec((rb, cols), lambda *g: (chunk_of(*g), 0)))
        out_shapes.append(jax.ShapeDtypeStruct((rows, cols), BF16))
    return args, in_specs, out_specs, out_shapes


def _col_softmax(parts, extra=None):
    m = parts[0].max(axis=0, keepdims=True)
    for p in parts[1:]:
        m = jnp.maximum(m, p.max(axis=0, keepdims=True))
    if extra is not None:
        m = jnp.maximum(m, extra)
    es = [jnp.exp2(p - m) for p in parts]
    den = es[0].sum(axis=0, keepdims=True)
    for e in es[1:]:
        den = den + e.sum(axis=0, keepdims=True)
    if extra is not None:
        den = den + jnp.exp2(extra - m)
    return es, 1.0 / den


DIFF_TQ = 256
DIFF_CHUNK = 256


def _diff_lambda(p, lam_init):
    a = jnp.sum(p[0:1, :] * p[1:2, :], axis=-1, keepdims=True)
    b = jnp.sum(p[2:3, :] * p[3:4, :], axis=-1, keepdims=True)
    return jnp.exp(a) - jnp.exp(b) + lam_init


def _split_maps(k):
    lane = lax.broadcasted_iota(jnp.int32, k.shape, 1)
    return (jnp.where(lane < DH_DIFF, k, 0.0).astype(BF16), jnp.where(lane >= DH_DIFF, k, 0.0).astype(BF16))


def _diff_latent_body(lam_ref, q_ref, k_ref, v_ref, ck_ref, cv_ref, cq_ref, sq_ref, ckk_ref, skk_ref,
                      *rest, lam_init, n_cast):
    w_src, o_ref, w_dst = rest[:n_cast], rest[n_cast], rest[n_cast + 1:2 * n_cast + 1]
    km_ref, vb_ref, ckm_ref, cvb_ref = rest[2 * n_cast + 1:]
    i = pl.program_id(1)
    _cast_weight_chunks(w_src, w_dst)
    heads = [slice(h * 128, (h + 1) * 128) for h in range(H_DIFF)]

    @pl.when(i == 0)
    def _():
        ckk, skk = ckk_ref[...], skk_ref[...]
        for h, hs in enumerate(heads):
            k1, k2 = _split_maps(_rope128(k_ref[:, hs], ckk, skk))
            km_ref[h, 0] = k1
            km_ref[h, 1] = k2
            vb_ref[h] = v_ref[:, hs].T.astype(BF16)
            c1, c2 = _split_maps(jnp.concatenate([ck_ref[h, 0], ck_ref[h, 1]], axis=1))
            ckm_ref[h, 0] = c1
            ckm_ref[h, 1] = c2
            cvb_ref[h] = cv_ref[h].T.astype(BF16)

    lam = _diff_lambda(lam_ref[...], lam_init)
    cq, sq = cq_ref[...], sq_ref[...]
    qbs = [(_rope128(q_ref[:, hs], cq, sq) * (DH_DIFF ** -0.5 * LOG2E)).astype(BF16) for hs in heads]
    ck = DIFF_CHUNK
    n_chunks = (PAST_LEN + km_ref.shape[2]) // ck
    own = lambda c: slice(c * ck - PAST_LEN, (c + 1) * ck - PAST_LEN)
    keys = lambda h, m, c: ckm_ref[h, m] if c == 0 else km_ref[h, m, own(c), :]
    vals_t = lambda h, c: cvb_ref[h] if c == 0 else vb_ref[h, :, own(c)]

    def col_max(chunks):
        m = chunks[0]
        for s in chunks[1:]:
            m = jnp.maximum(m, s)
        return m.max(axis=0, keepdims=True)

    def add(a, b):
        return b if a is None else a + b

    maps = [(h, m) for h in range(H_DIFF) for m in range(2)]
    logits, exps, psum, acc = {}, {}, {}, {}
    for stage in range(len(maps) + 2):
        p_l = maps[stage] if stage < len(maps) else None
        p_e = maps[stage - 1] if 0 <= stage - 1 < len(maps) else None
        p_v = maps[stage - 2] if 0 <= stage - 2 < len(maps) else None
        if p_l is not None:
            logits[p_l] = []
        if p_e is not None:
            mx = col_max(logits[p_e])
            exps[p_e], psum[p_e] = [], None
        if p_v is not None:
            acc[p_v] = None
        for c in range(n_chunks):
            if p_l is not None:
                logits[p_l].append(_dot_nt(keys(p_l[0], p_l[1], c), qbs[p_l[0]]))
            if p_e is not None:
                e = jnp.exp2(logits[p_e][c] - mx)
                psum[p_e] = add(psum[p_e], e)
                exps[p_e].append(e.astype(BF16))
            if p_v is not None:
                acc[p_v] = add(acc[p_v], _dot(vals_t(p_v[0], c), exps[p_v][c]))
        if p_e is not None:
            del logits[p_e]
        if p_v is not None:
            del exps[p_v]
    outs = []
    for h in range(H_DIFF):
        inv1 = 1.0 / psum[h, 0].sum(axis=0, keepdims=True)
        inv2 = 1.0 / psum[h, 1].sum(axis=0, keepdims=True)
        outs.append((acc[h, 0] * inv1 - acc[h, 1] * (inv2 * lam)).T)
    for hs, y in zip(heads, _norm_rows_many(outs)):
        o_ref[:, hs] = (y * (1.0 - lam_init)).astype(BF16)


def _diff_ctx_body(lam_ref, q_ref, k_ref, v_ref, *rest, lam_init, n_prev):
    o_ref, kc_ref, vc_ref = rest[-3:]
    kc_new, vc_new = kc_ref, vc_ref
    if n_prev:
        kp_ref, vp_ref = rest[0:2]
        for e in range(n_prev):
            kc_ref[e] = kp_ref[e] if n_prev > 1 else kp_ref[...]
            vc_ref[e] = vp_ref[e] if n_prev > 1 else vp_ref[...]
        kc_new, vc_new = kc_ref.at[n_prev], vc_ref.at[n_prev]
    lam = _diff_lambda(lam_ref[...], lam_init)
    heads = [slice(h * 128, (h + 1) * 128) for h in range(H_DIFF)]
    qbs, kms, vbs = [], [], []
    for h, hs in enumerate(heads):
        kk = k_ref[:, hs]
        vv = v_ref[:, hs]
        kc_new[h, 0] = kk[:, 0:DH_DIFF]
        kc_new[h, 1] = kk[:, DH_DIFF:2 * DH_DIFF]
        vc_new[h] = vv
        kms.append(_split_maps(kk))
        vbs.append(vv.astype(BF16))
        qbs.append((q_ref[:, hs] * (DH_DIFF ** -0.5 * LOG2E)).astype(BF16))
    pairs = [(h, m) for h in range(H_DIFF) for m in range(2)]
    logits = {p: _dot_nt(qbs[p[0]], kms[p[0]][p[1]]) for p in pairs}
    exps = {p: jnp.exp2(logits[p] - logits[p].max(axis=-1, keepdims=True)) for p in pairs}
    invs = {p: 1.0 / exps[p].sum(axis=-1, keepdims=True) for p in pairs}
    accs = {p: _dot(exps[p].astype(BF16), vbs[p[0]]) for p in pairs}
    outs = [accs[h, 0] * invs[h, 0] - accs[h, 1] * (invs[h, 1] * lam) for h in range(H_DIFF)]
    for hs, y in zip(heads, _norm_rows_many(outs)):
        o_ref[:, hs] = (y * (1.0 - lam_init)).astype(BF16)


def _diff_attention_ctx(proj, lam_p, lam_init, *, n_seq, seq, prev=None):
    hw = H_DIFF * 128
    n_prev = 0 if prev is None else (1 if prev[0].ndim == 5 else prev[0].shape[1])
    one_k, one_v = (H_DIFF, 2, seq, DH_DIFF), (H_DIFF, seq, DV_DIFF)
    zeros = lambda n: (0,) * n
    in_specs = [pl.BlockSpec((4, DH_DIFF), lambda b: (0, 0)),
                pl.BlockSpec((seq, hw), lambda b: (b, E_DQ // hw)),
                pl.BlockSpec((seq, hw), lambda b: (b, E_DK // hw)),
                pl.BlockSpec((seq, hw), lambda b: (b, E_DV // hw))]
    args = [lam_p, proj, proj, proj]
    if n_prev:
        lead = (None,) if n_prev == 1 else (None, n_prev)
        in_specs += [pl.BlockSpec(lead + one_k, lambda b: (b,) + zeros(len(lead) + 3)),
                     pl.BlockSpec(lead + one_v, lambda b: (b,) + zeros(len(lead) + 2))]
        args += list(prev)
    lead_out = (None, n_prev + 1) if n_prev else (None,)
    out_specs = [pl.BlockSpec((seq, hw), lambda b: (b, 0)),
                 pl.BlockSpec(lead_out + one_k, lambda b: (b,) + zeros(len(lead_out) + 3)),
                 pl.BlockSpec(lead_out + one_v, lambda b: (b,) + zeros(len(lead_out) + 2))]
    out_shape = [jax.ShapeDtypeStruct((n_seq * seq, hw), BF16),
                 jax.ShapeDtypeStruct((n_seq,) + lead_out[1:] + one_k, F32),
                 jax.ShapeDtypeStruct((n_seq,) + lead_out[1:] + one_v, F32)]
    return pl.pallas_call(
        functools.partial(_diff_ctx_body, lam_init=lam_init, n_prev=n_prev),
        grid=(n_seq,),
        in_specs=in_specs,
        out_specs=out_specs,
        out_shape=out_shape,
        compiler_params=_cparams(1),
        name="diff_attention_ctx",
    )(*args)


def _diff_attention_latent(proj, lam_p, lam_init, *, n_seq, seq, cache, rope, cast):
    tq = DIFF_TQ
    nq = seq // tq
    hw = H_DIFF * 128
    ck, cv = cache
    cos, sin = rope
    in_specs = [pl.BlockSpec((4, DH_DIFF), lambda b, i: (0, 0)),
                pl.BlockSpec((tq, hw), lambda b, i: (b * nq + i, E_DQ // hw)),
                pl.BlockSpec((seq, hw), lambda b, i: (b, E_DK // hw)),
                pl.BlockSpec((seq, hw), lambda b, i: (b, E_DV // hw)),
                pl.BlockSpec((None, H_DIFF, 2, PAST_LEN, DH_DIFF), lambda b, i: (b, 0, 0, 0, 0)),
                pl.BlockSpec((None, H_DIFF, PAST_LEN, DV_DIFF), lambda b, i: (b, 0, 0, 0)),
                pl.BlockSpec((tq, 128), lambda b, i: (i, 0)),
                pl.BlockSpec((tq, 128), lambda b, i: (i, 0)),
                pl.BlockSpec((seq, 128), lambda b, i: (0, 0)),
                pl.BlockSpec((seq, 128), lambda b, i: (0, 0))]
    args = [lam_p, proj, proj, proj, ck, cv, cos, sin, cos, sin]
    out_specs = [pl.BlockSpec((tq, hw), lambda b, i: (b * nq + i, 0))]
    out_shape = [jax.ShapeDtypeStruct((n_seq * seq, hw), BF16)]
    w_args, w_in, w_out, w_shape = _weight_cast_plan(cast, n_seq * nq, lambda b, i: b * nq + i)
    scratch = [pltpu.VMEM((H_DIFF, 2, seq, 128), BF16), pltpu.VMEM((H_DIFF, DV_DIFF, seq), BF16),
               pltpu.VMEM((H_DIFF, 2, PAST_LEN, 128), BF16), pltpu.VMEM((H_DIFF, DV_DIFF, PAST_LEN), BF16)]
    return pl.pallas_call(
        functools.partial(_diff_latent_body, lam_init=lam_init, n_cast=len(cast)),
        grid=(n_seq, nq),
        in_specs=in_specs + w_in,
        out_specs=out_specs + w_out,
        out_shape=out_shape + w_shape,
        scratch_shapes=scratch,
        compiler_params=_cparams(2),
        name="diff_attention",
    )(*args, *w_args)


SWA_TQ = 256
SWA_WIN = SWA_TQ + 2 * WINDOW
NEG = -1e30


def _swa_body(*refs, latent, seq, n_prev=0, n_cast=0):
    tq = SWA_TQ
    if latent:
        q_ref, k_ref, v_ref, sink_ref, ck_ref, cv_ref, cq_ref, sq_ref, ckk_ref, skk_ref = refs[:10]
        w_src, o_ref, w_dst = refs[10:10 + n_cast], refs[10 + n_cast], refs[11 + n_cast:11 + 2 * n_cast]
        kr_ref, cvt_ref = refs[11 + 2 * n_cast:]
        _cast_weight_chunks(w_src, w_dst)
        qi = pl.program_id(1)

        @pl.when(qi == 0)
        def _():
            kr_ref[...] = _rope128(k_ref[...], ckk_ref[...], skk_ref[...]).astype(BF16)
            cvt_ref[...] = jnp.concatenate([cv_ref[0], cv_ref[1]], axis=1).T.astype(BF16)

        cq, sq = cq_ref[...], sq_ref[...]
        q = jnp.concatenate([_rope128(q_ref[:, g * 128:(g + 1) * 128], cq, sq)
                             for g in range(H_SWA * DH_SWA // 128)], axis=1)
        ws = pl.multiple_of(jnp.clip(qi * tq - WINDOW, 0, seq - SWA_WIN), 128)
        kw = kr_ref[pl.ds(ws, SWA_WIN), :]
        vwt = v_ref[pl.ds(ws, SWA_WIN), :].T.astype(BF16)
        kpos = ws + lax.broadcasted_iota(jnp.int32, (SWA_WIN, tq), 0)
        qpos = qi * tq + lax.broadcasted_iota(jnp.int32, (SWA_WIN, tq), 1)
        bias1 = jnp.where(jnp.abs(qpos - kpos) <= WINDOW, 0.0, NEG)
        bias = jnp.concatenate([bias1] * SWA_GROUP, axis=1)
    else:
        q_ref, k_ref, v_ref, sink_ref = refs[:4]
        o_ref, kc_ref, vc_ref = refs[-3:]
        q = q_ref[...]
        kk = k_ref[...]
        vv = v_ref[...]
        kc_new, vc_new = kc_ref, vc_ref
        if n_prev:
            kp_ref, vp_ref = refs[4:6]
            for e in range(n_prev):
                kc_ref[e] = kp_ref[e] if n_prev > 1 else kp_ref[...]
                vc_ref[e] = vp_ref[e] if n_prev > 1 else vp_ref[...]
            kc_new, vc_new = kc_ref.at[n_prev], vc_ref.at[n_prev]
        for kv in range(KV_SWA):
            kc_new[kv] = kk[:, kv * DH_SWA:(kv + 1) * DH_SWA]
            vc_new[kv] = vv[:, kv * DH_SWA:(kv + 1) * DH_SWA]
        kw = kk.astype(BF16)
        vwt = vv.T.astype(BF16)
    q = q * (DH_SWA ** -0.5 * LOG2E)
    lanes = [slice(kv * DH_SWA, (kv + 1) * DH_SWA) for kv in range(KV_SWA)]

    def logits(kv):
        q4 = jnp.concatenate([q[:, (kv * SWA_GROUP + g) * DH_SWA:(kv * SWA_GROUP + g + 1) * DH_SWA]
                              for g in range(SWA_GROUP)], axis=0).astype(BF16)
        s_loc = _dot_nt(kw[:, lanes[kv]], q4)
        if latent:
            return [_dot_nt(ck_ref[kv].astype(BF16), q4), s_loc + bias]
        return [s_loc]

    def values(kv, es):
        vals_t = [cvt_ref[lanes[kv], :], vwt[lanes[kv], :]] if latent else [vwt[lanes[kv], :]]
        acc = None
        for e, vt in zip(es, vals_t):
            t = _dot(vt, e.astype(BF16))
            acc = t if acc is None else acc + t
        return acc

    parts = logits(0)
    accs = []
    for kv in range(KV_SWA):
        nxt = logits(kv + 1) if kv + 1 < KV_SWA else None
        es, inv = _col_softmax(parts, extra=sink_ref[kv] * LOG2E)
        accs.append((es, inv))
        parts = nxt
    pieces = []
    for kv, (es, inv) in enumerate(accs):
        acc = values(kv, es) * inv
        pieces += [acc[:, g * tq:(g + 1) * tq] for g in range(SWA_GROUP)]
    o_ref[...] = jnp.concatenate(pieces, axis=0).T.astype(BF16)


def _swa_attention(proj, sink, *, n_seq, seq, cache=None, rope=None, cast=None, prev=None):
    latent = cache is not None
    n_prev = 0
    tq = SWA_TQ
    nq = seq // tq
    hq = H_SWA * DH_SWA
    sink_b = jnp.repeat(sink.reshape(KV_SWA, SWA_GROUP), tq, axis=1).reshape(KV_SWA, 1, SWA_GROUP * tq)
    in_specs = [pl.BlockSpec((tq, hq), lambda b, i: (b * nq + i, 0)),
                pl.BlockSpec((seq, 128), lambda b, i: (b, O_SK // 128)),
                pl.BlockSpec((seq, 128), lambda b, i: (b, O_SV // 128)),
                pl.BlockSpec((KV_SWA, 1, SWA_GROUP * tq), lambda b, i: (0, 0, 0))]
    args = [proj, proj, proj, sink_b]
    scratch = []
    out_specs = [pl.BlockSpec((tq, hq), lambda b, i: (b * nq + i, 0))]
    out_shape = [jax.ShapeDtypeStruct((n_seq * seq, hq), BF16)]
    if latent:
        ck, cv = cache
        cos, sin = rope
        in_specs += [pl.BlockSpec((None, KV_SWA, PAST_LEN, DH_SWA), lambda b, i: (b, 0, 0, 0)),
                     pl.BlockSpec((None, KV_SWA, PAST_LEN, DH_SWA), lambda b, i: (b, 0, 0, 0)),
                     pl.BlockSpec((tq, 128), lambda b, i: (i, 0)),
                     pl.BlockSpec((tq, 128), lambda b, i: (i, 0)),
                     pl.BlockSpec((seq, 128), lambda b, i: (0, 0)),
                     pl.BlockSpec((seq, 128), lambda b, i: (0, 0))]
        args += [ck, cv, cos, sin, cos, sin]
        w_args, w_in, w_out, w_shape = _weight_cast_plan(cast, n_seq * nq, lambda b, i: b * nq + i)
        args += w_args
        in_specs += w_in
        out_specs += w_out
        out_shape += w_shape
        scratch = [pltpu.VMEM((seq, 128), BF16), pltpu.VMEM((KV_SWA * DH_SWA, PAST_LEN), BF16)]
    else:
        n_prev = 0 if prev is None else (1 if prev[0].ndim == 4 else prev[0].shape[1])
        one = (KV_SWA, seq, DH_SWA)
        if n_prev:
            lead = (None,) if n_prev == 1 else (None, n_prev)
            in_specs += [pl.BlockSpec(lead + one, lambda b, i: (b,) + (0,) * (len(lead) + 2))] * 2
            args += list(prev)
        lead_out = (None, n_prev + 1) if n_prev else (None,)
        cache_spec = pl.BlockSpec(lead_out + one, lambda b, i: (b,) + (0,) * (len(lead_out) + 2))
        cache_shape = jax.ShapeDtypeStruct((n_seq,) + lead_out[1:] + one, F32)
        out_specs += [cache_spec, cache_spec]
        out_shape += [cache_shape, cache_shape]
    return pl.pallas_call(
        functools.partial(_swa_body, latent=latent, seq=seq, n_prev=n_prev, n_cast=len(cast or ())),
        grid=(n_seq, nq),
        in_specs=in_specs,
        out_specs=out_specs,
        out_shape=out_shape,
        scratch_shapes=scratch,
        compiler_params=_cparams(2),
        name="swa_attention",
    )(*args)


GLA_BLOCK = 128
GLA_LEVELS = 7
GLA_SAFE = 60.0
GLA_GROUP = 8
GLA_GATE_ROWS = 1024


def _split3(x):
    hi = x.astype(BF16)
    r1 = x - hi.astype(F32)
    mid = r1.astype(BF16)
    lo = (r1 - mid.astype(F32)).astype(BF16)
    return hi, mid, lo


def _gla_gate_body(r_ref, w2_ref, b2_ref, x_ref, tot_ref):
    T = GLA_BLOCK
    ti = lax.broadcasted_iota(jnp.int32, (T, T), 0)
    si = lax.broadcasted_iota(jnp.int32, (T, T), 1)
    r_hi, r_mid, _ = _split3(r_ref[...])
    zs = []
    for d in range(2):
        w_hi, w_mid, _ = _split3(w2_ref[d])
        zs.append(b2_ref[d] + _dot(r_hi, w_hi) + _dot(r_hi, w_mid) + _dot(r_mid, w_hi))
    las = [_split3((jnp.minimum(z, 0.0) - jnp.log(1.0 + jnp.exp(-jnp.abs(z)))) * (1.0 / GLA_TAU)) for z in zs]
    for d in range(2):
        tri = jnp.where((si >= ti) if d else (si <= ti), 1.0, 0.0).astype(BF16)
        l_hi, l_mid, l_lo = las[d]
        for j in range(GLA_GATE_ROWS // T):
            rows = slice(j * T, (j + 1) * T)
            x = _dot(tri, l_hi[rows]) + _dot(tri, l_mid[rows]) + _dot(tri, l_lo[rows])
            x_ref[d, rows, :] = x
            tot_ref[d, j:j + 1, :] = x[0:1, :] if d else x[T - 1:T, :]


def _gla_gates(proj, w2, b2):
    n_rows = proj.shape[0]
    hk = H_GLA * DK_GLA
    tr = GLA_GATE_ROWS
    w2p = jnp.zeros((2, 128, hk), F32)
    w2p = w2p.at[0, 0:GLA_RANK].set(w2[0]).at[1, GLA_RANK:2 * GLA_RANK].set(w2[1])
    return pl.pallas_call(
        _gla_gate_body,
        grid=(n_rows // tr,),
        in_specs=[pl.BlockSpec((tr, 128), lambda i: (i, O_GLR // 128)),
                  pl.BlockSpec((2, 128, hk), lambda i: (0, 0, 0)),
                  pl.BlockSpec((2, 1, hk), lambda i: (0, 0, 0))],
        out_specs=[pl.BlockSpec((2, tr, hk), lambda i: (0, i, 0)),
                   pl.BlockSpec((2, tr // GLA_BLOCK, hk), lambda i: (0, i, 0))],
        out_shape=[jax.ShapeDtypeStruct((2, n_rows, hk), F32),
                   jax.ShapeDtypeStruct((2, n_rows // GLA_BLOCK, hk), F32)],
        compiler_params=_cparams(1),
        name="gla_gates",
    )(proj, w2p, b2.reshape(2, 1, hk))


def _gla_body(*refs, rev, n_blocks, finish):
    if finish:
        (flag_ref, q_ref, k_ref, v_ref, x_ref, s0_ref, of_ref, g_ref, o_ref, sfin_ref, st_ref, att_ref) = refs
    else:
        flag_ref, q_ref, k_ref, v_ref, x_ref, s0_ref, o_ref, sfin_ref, st_ref, att_ref = refs
    n = pl.program_id(1)
    blk = (n_blocks - 1 - n) if rev else n
    safe = flag_ref[pl.program_id(0) * n_blocks + blk] != 0
    T = GLA_BLOCK
    G = q_ref.shape[0]
    hk = H_GLA * DK_GLA
    heads = [slice(h * DK_GLA, (h + 1) * DK_GLA) for h in range(H_GLA)]
    scale = DK_GLA ** -0.5
    ti = lax.broadcasted_iota(jnp.int32, (T, T), 0)
    si = lax.broadcasted_iota(jnp.int32, (T, T), 1)
    causal = (si >= ti) if rev else (si <= ti)

    @pl.when(n == 0)
    def _():
        st_ref[...] = s0_ref[...]

    @pl.when(safe)
    def _():
        for j in range(G):
            x = x_ref[j]
            qs = (q_ref[j] * scale * jnp.exp(x)).astype(BF16)
            ks = (k_ref[j] * jnp.exp(-x)).astype(BF16)
            for h, hs in enumerate(heads):
                att_ref[j, h] = jnp.where(causal, _dot_nt(qs[:, hs], ks[:, hs]), 0.0)

    @pl.when(jnp.logical_not(safe))
    def _():
        t_idx = lax.broadcasted_iota(jnp.int32, (T, hk), 0)
        xr = ti ^ si
        for j in range(G):
            q = q_ref[j] * scale
            k = k_ref[j]
            x = x_ref[j]
            if rev:
                la = jnp.where(t_idx == T - 1, x, x - pltpu.roll(x, T - 1, 0))
            else:
                la = jnp.where(t_idx == 0, x, x - pltpu.roll(x, 1, 0))
            qb, kb = q.astype(BF16), k.astype(BF16)
            att = [_dot_nt(qb[:, hs], kb[:, hs]) for hs in heads]
            xg = la
            tg = la
            for lvl in range(GLA_LEVELS):
                sz = 1 << lvl
                upper = ((t_idx >> lvl) & 1) == 1
                is_q = jnp.logical_not(upper) if rev else upper
                partner = jnp.where(upper, pltpu.roll(tg, sz, 0), pltpu.roll(tg, T - sz, 0))
                e = jnp.exp(jnp.where(is_q, xg, tg - xg))
                qs = jnp.where(is_q, q * e, 0.0).astype(BF16)
                ks = jnp.where(is_q, 0.0, k * e).astype(BF16)
                for h, hs in enumerate(heads):
                    att[h] = jnp.where(xr >= sz, _dot_nt(qs[:, hs], ks[:, hs]), att[h])
                xg = xg + jnp.where(is_q, partner, 0.0)
                tg = tg + partner
            for h in range(H_GLA):
                att_ref[j, h] = att[h]

    prep = []
    for j in range(G):
        x = x_ref[j]
        tot = x[0:1, :] if rev else x[T - 1:T, :]
        v = v_ref[j]
        prep.append(dict(qe=(q_ref[j] * scale * jnp.exp(x)).astype(BF16),
                         kw=(k_ref[j] * jnp.exp(tot - x)).astype(BF16),
                         vt=v.T.astype(BF16),
                         vb=v.astype(BF16), dec=jnp.exp(tot)))
    outs = {}
    for j, p in enumerate(prep):
        for h, hs in enumerate(heads):
            vs = slice(h * DV_GLA, (h + 1) * DV_GLA)
            o = _dot(att_ref[j, h].astype(BF16), p["vb"][:, vs]) + _dot_nt(p["qe"][:, hs], st_ref[j, h].astype(BF16))
            if finish:
                outs[j, h] = o + of_ref[j, :, vs]
            else:
                o_ref[j, :, vs] = o
    for j, p in enumerate(prep):
        for h, hs in enumerate(heads):
            vs = slice(h * DV_GLA, (h + 1) * DV_GLA)
            st_ref[j, h] = st_ref[j, h] * p["dec"][:, hs] + _dot(p["vt"][vs, :], p["kw"][:, hs])
    if finish:
        keys = sorted(outs)
        normed = _norm_rows_many([outs[k] for k in keys])
        for (j, h), y in zip(keys, normed):
            vs = slice(h * DV_GLA, (h + 1) * DV_GLA)
            o_ref[j, :, vs] = (y * _silu(g_ref[j, :, vs])).astype(BF16)

    @pl.when(n == n_blocks - 1)
    def _():
        sfin_ref[...] = st_ref[...]


def _gla(proj, w2, b2, s0t, *, n_seq, seq):
    nb = seq // GLA_BLOCK
    G = min(GLA_GROUP, n_seq)
    hk, hv = H_GLA * DK_GLA, H_GLA * DV_GLA
    x, tot = _gla_gates(proj, w2, b2)
    x = x.reshape(2, n_seq, seq, hk)
    safe = (tot.min(axis=-1) > -GLA_SAFE).reshape(2, n_seq // G, G, nb).all(axis=2)
    flags = safe.astype(jnp.int32).reshape(2, (n_seq // G) * nb)
    p3 = proj.reshape(n_seq, seq, ODD_IN_PAD)
    state_shape = jax.ShapeDtypeStruct((n_seq, H_GLA, DV_GLA, DK_GLA), F32)
    state_spec = pl.BlockSpec((G, H_GLA, DV_GLA, DK_GLA), lambda g, n, f: (g, 0, 0, 0))
    scratch = [pltpu.VMEM((G, H_GLA, DV_GLA, DK_GLA), F32), pltpu.VMEM((G, H_GLA, GLA_BLOCK, GLA_BLOCK), F32)]

    def run(rev, extra_args, extra_cols, out_dtype):
        blk = (lambda n: nb - 1 - n) if rev else (lambda n: n)
        d = 1 if rev else 0
        tok = lambda width, col: pl.BlockSpec((G, GLA_BLOCK, width), lambda g, n, f: (g, blk(n), col))
        grid_spec = pltpu.PrefetchScalarGridSpec(
            num_scalar_prefetch=1,
            grid=(n_seq // G, nb),
            in_specs=[tok(hk, O_GQ // hk), tok(hk, O_GK // hk), tok(hv, O_GV // hv),
                      pl.BlockSpec((None, G, GLA_BLOCK, hk), lambda g, n, f: (d, g, blk(n), 0)),
                      pl.BlockSpec((None, G, H_GLA, DV_GLA, DK_GLA), lambda g, n, f: (d, g, 0, 0, 0))]
            + [tok(hv, col) for col in extra_cols],
            out_specs=[tok(hv, 0), state_spec],
            scratch_shapes=scratch)
        return pl.pallas_call(
            functools.partial(_gla_body, rev=rev, n_blocks=nb, finish=rev),
            grid_spec=grid_spec,
            out_shape=[jax.ShapeDtypeStruct((n_seq, seq, hv), out_dtype), state_shape],
            compiler_params=_cparams(2),
            name="gla_bwd" if rev else "gla_fwd",
        )(flags[d], p3, p3, p3, x, s0t, *extra_args)

    o_f, s_f = run(False, [], [], F32)
    out, s_b = run(True, [o_f, p3], [0, O_GR // hv], BF16)
    return out.reshape(n_seq * seq, hv), jnp.stack([s_f, s_b], axis=0)


def _stacked(caches, n_layers):
    return [c[:, None] for c in caches] if n_layers == 1 else caches


def kernel(x_prompt, x_sample, c, state_ret, cache_diff_k, cache_diff_v, cache_swa_k, cache_swa_v, state_gla,
           c_ctx, w_mod, b_mod, ln_g, ln_b, w_in_even, w_out_even, ret_decay, diff_lam, w_in_odd, w_out_odd,
           swa_sink, gla_w2, gla_b, w_ff1, w_ff2):
    xc = x_prompt.reshape(N_CTX, D_MODEL)
    xl = x_sample.reshape(N_LAT, D_MODEL)
    cond = jnp.concatenate([c_ctx[None, :], c, jnp.zeros((N_COND - 1 - DEC_BATCH, D_MODEL), F32)], axis=0)
    mod = _modulation(cond, w_mod, b_mod).reshape(DEPTH, N_COND, 1, 6 * D_MODEL)
    rope = _rope_tables(DEC_SEQ // GRID_W, DH_DIFF)

    w_in_even_b = w_in_even.astype(BF16)
    w_in_odd_b = _reorder_odd(w_in_odd.astype(BF16))

    ctx = dict(n_seq=BATCH, seq=SEQ)
    lat = dict(n_seq=DEC_BATCH, seq=DEC_SEQ)
    new_ret, new_gla = [], []
    new_diff = new_swa = None
    for l in range(DEPTH):
        if l % 2 == 0:
            e = l // 2
            lam_init = 0.8 - 0.6 * math.exp(-0.3 * l)
            pc = _in_projection(xc, mod[l], w_in_even_b[e], False)
            pl_ = _in_projection(xl, mod[l], w_in_even_b[e], True)
            zero = jnp.zeros((BATCH, 2, H_RET, DK_RET, DV_RET), F32)
            a_ctx, s_ret = _retention(pc, ret_decay[e], zero, **ctx)
            a_lat, _ = _retention(pl_, ret_decay[e], state_ret[:, e], **lat)
            b_ctx, *new_diff = _diff_attention_ctx(pc, diff_lam[e], lam_init, prev=new_diff, **ctx)
            b_lat, *w_post = _diff_attention_latent(pl_, diff_lam[e], lam_init,
                                                    cache=(cache_diff_k[:, e], cache_diff_v[:, e]), rope=rope,
                                                    cast=[(w_out_even, e), (w_ff1, l), (w_ff2, l)], **lat)
            new_ret.append(s_ret)
        else:
            o = l // 2
            pc = _in_projection(xc, mod[l], w_in_odd_b[o], False)
            pl_ = _in_projection(xl, mod[l], w_in_odd_b[o], True)
            a_ctx, *new_swa = _swa_attention(pc, swa_sink[o], prev=new_swa, **ctx)
            a_lat, *w_post = _swa_attention(pl_, swa_sink[o], cache=(cache_swa_k[:, o], cache_swa_v[:, o]),
                                            rope=rope, cast=[(w_out_odd, o), (w_ff1, l), (w_ff2, l)], **lat)
            zero = jnp.zeros((2, BATCH, H_GLA, DV_GLA, DK_GLA), F32)
            b_ctx, s_gla = _gla(pc, gla_w2[o], gla_b[o], zero, **ctx)
            s0t = state_gla[:, o].transpose(1, 0, 2, 4, 3)
            b_lat, _ = _gla(pl_, gla_w2[o], gla_b[o], s0t, **lat)
            new_gla.append(s_gla.transpose(1, 0, 2, 4, 3))
        w_out_b, w1_b, w2_b = w_post
        xc = _post_mixer(xc, a_ctx, b_ctx, mod[l], w_out_b, ln_g, ln_b, w1_b, w2_b, l, False)
        xl = _post_mixer(xl, a_lat, b_lat, mod[l], w_out_b, ln_g, ln_b, w1_b, w2_b, l, True)
    return (xc.reshape(BATCH, SEQ, D_MODEL), xl.reshape(DEC_BATCH, DEC_SEQ, D_MODEL),
            jnp.stack(new_ret, axis=1), *_stacked(new_diff, N_EVEN), *_stacked(new_swa, N_ODD),
            jnp.stack(new_gla, axis=1))
```

```python
import functools
import math

import jax
import jax.numpy as jnp
from jax import lax
from jax.experimental import pallas as pl
from jax.experimental.pallas import tpu as pltpu

F32 = jnp.float32
BF16 = jnp.bfloat16

D_MODEL = 1024
BATCH = 16
SEQ = 256
DEPTH = 4
DEC_BATCH = 4
DEC_SEQ = 2048
PAST_LEN = 256
GRID_W = 64
H_RET, DK_RET, DV_RET = 4, 64, 128
H_DIFF, DH_DIFF, DV_DIFF = 4, 64, 128
H_SWA, KV_SWA, DH_SWA = 8, 2, 64
SWA_GROUP = H_SWA // KV_SWA
WINDOW = 128
H_GLA, DK_GLA, DV_GLA = 4, 64, 128
GLA_RANK = 16
GLA_TAU = 16.0
D_FF = 4 * D_MODEL
ROPE_BASE = 10000.0
N_EVEN = (DEPTH + 1) // 2
N_ODD = DEPTH // 2
ALPHA = (2 * DEPTH) ** 0.25
EVEN_IN = 3072
ODD_IN = 2336
ODD_IN_PAD = 2432
EPS = 1e-5
LOG2E = 1.4426950408889634

E_RQ, E_RK, E_RV, E_RG, E_DQ, E_DK, E_DV = 0, 256, 512, 1024, 1536, 2048, 2560
O_SQ, O_GV, O_GR, O_GQ, O_GK, O_SK, O_SV, O_GLR = 0, 512, 1024, 1536, 1792, 2048, 2176, 2304


def _reorder_odd(w):
    sq, sk, sv, gq, gk, gv, gr, glr = jnp.split(w, [512, 640, 768, 1024, 1280, 1792, 2304], axis=-1)
    pad = jnp.zeros(w.shape[:-1] + (ODD_IN_PAD - ODD_IN,), w.dtype)
    return jnp.concatenate([sq, gv, gr, gq, gk, sk, sv, glr, pad], axis=-1)


N_CTX = BATCH * SEQ
N_LAT = DEC_BATCH * DEC_SEQ
DENSE_TILE = 512
POST_TILE = 1024
N_COND = 8

VMEM_LIMIT = 56 * 1024 * 1024


def _cparams(n_axes):
    return pltpu.CompilerParams(dimension_semantics=("arbitrary",) * n_axes,
                                vmem_limit_bytes=VMEM_LIMIT)


def _dot(a, b):
    return jnp.dot(a, b, preferred_element_type=F32)


def _dot_nt(a, b):
    return lax.dot_general(a, b, (((1,), (1,)), ((), ())), preferred_element_type=F32)


def _silu(x):
    return x * (1.0 / (1.0 + jnp.exp(-x)))


def _norm_rows(x):
    mu = jnp.mean(x, axis=-1, keepdims=True)
    xc = x - mu
    var = jnp.mean(xc * xc, axis=-1, keepdims=True)
    return xc * lax.rsqrt(var + EPS)


def _norm_rows_many(xs):
    mus = [jnp.mean(x, axis=-1, keepdims=True) for x in xs]
    xcs = [x - mu for x, mu in zip(xs, mus)]
    vs = [jnp.mean(xc * xc, axis=-1, keepdims=True) for xc in xcs]
    return [xc * lax.rsqrt(v + EPS) for xc, v in zip(xcs, vs)]


def _mod_row(i, latent, tile):
    return 1 + i // (DEC_SEQ // tile) if latent else 0


def _resident(shape, index_map):
    return pl.BlockSpec(shape, index_map, pipeline_mode=pl.Buffered(1))


def _mod_body(c_ref, w_ref, b_ref, o_ref):
    c = _silu(c_ref[...]).astype(BF16)
    o_ref[...] = _dot(c, w_ref[...].astype(BF16)) + b_ref[...]


def _modulation(cond, w_mod, b_mod):
    tn = 1536
    return pl.pallas_call(
        _mod_body,
        grid=(DEPTH, 6 * D_MODEL // tn),
        in_specs=[pl.BlockSpec((N_COND, D_MODEL), lambda l, j: (0, 0)),
                  pl.BlockSpec((None, D_MODEL, tn), lambda l, j: (l, 0, j)),
                  pl.BlockSpec((None, 1, tn), lambda l, j: (l, 0, j))],
        out_specs=pl.BlockSpec((None, N_COND, tn), lambda l, j: (l, 0, j)),
        out_shape=jax.ShapeDtypeStruct((DEPTH, N_COND, 6 * D_MODEL), F32),
        compiler_params=_cparams(2),
        name="modulation",
    )(cond, w_mod, b_mod.reshape(DEPTH, 1, 6 * D_MODEL))


def _inproj_body(x_ref, mod_ref, w_ref, o_ref):
    sh = mod_ref[:, 0:D_MODEL]
    sc = mod_ref[:, D_MODEL:2 * D_MODEL]
    h = (x_ref[...] * (1.0 + sc) + sh).astype(BF16)
    o_ref[...] = _dot(h, w_ref[...])


def _in_projection(x, mod_l, w, latent):
    n_in = w.shape[1]
    n_rows = x.shape[0]
    tm = DENSE_TILE
    return pl.pallas_call(
        _inproj_body,
        grid=(n_rows // tm,),
        in_specs=[pl.BlockSpec((tm, D_MODEL), lambda i: (i, 0)),
                  pl.BlockSpec((None, 1, 6 * D_MODEL), lambda i: (_mod_row(i, latent, tm), 0, 0)),
                  _resident((D_MODEL, n_in), lambda i: (0, 0))],
        out_specs=pl.BlockSpec((tm, n_in), lambda i: (i, 0)),
        out_shape=jax.ShapeDtypeStruct((n_rows, n_in), F32),
        compiler_params=_cparams(1),
        name="in_projection",
    )(x, mod_l, w)


def _layer_norm(x, g, b):
    return _norm_rows(x) * g + b


def _post_body(x_ref, ma_ref, mb_ref, mod_ref, wo_ref, g_ref, b_ref, w1_ref, w2_ref, o_ref):
    half = D_MODEL // 2
    gt1 = mod_ref[:, 2 * D_MODEL:3 * D_MODEL]
    sh2 = mod_ref[:, 3 * D_MODEL:4 * D_MODEL]
    sc2 = mod_ref[:, 4 * D_MODEL:5 * D_MODEL]
    gt2 = mod_ref[:, 5 * D_MODEL:6 * D_MODEL]
    chunk = 1024
    n_chunks = D_FF // chunk
    top, bot = slice(0, POST_TILE // 2), slice(POST_TILE // 2, POST_TILE)

    def mix(r):
        return _dot(ma_ref[r, :], wo_ref[0:half, :]) + _dot(mb_ref[r, :], wo_ref[half:D_MODEL, :])

    def norm1(r, m):
        x1 = _layer_norm(ALPHA * x_ref[r, :] + gt1 * m, g_ref[0:1, :], b_ref[0:1, :])
        return x1, (x1 * (1.0 + sc2) + sh2).astype(BF16)

    def ffn(hf, j):
        h1 = _dot(hf, w1_ref[:, j * chunk:(j + 1) * chunk])
        h1 = jnp.square(jnp.maximum(h1, 0.0)).astype(BF16)
        return _dot(h1, w2_ref[j * chunk:(j + 1) * chunk, :])

    def norm2(r, x1, ff):
        o_ref[r, :] = _layer_norm(ALPHA * x1 + gt2 * ff, g_ref[1:2, :], b_ref[1:2, :])

    m_top, m_bot = mix(top), mix(bot)
    x1_top, hf_top = norm1(top, m_top)
    ff_top = ffn(hf_top, 0)
    x1_bot, hf_bot = norm1(bot, m_bot)
    for j in range(1, n_chunks):
        ff_top = ff_top + ffn(hf_top, j)
    ff_bot = ffn(hf_bot, 0)
    norm2(top, x1_top, ff_top)
    for j in range(1, n_chunks):
        ff_bot = ff_bot + ffn(hf_bot, j)
    norm2(bot, x1_bot, ff_bot)


def _post_mixer(x, mix_a, mix_b, mod_l, w_out, ln_g, ln_b, w1, w2, layer, latent):
    half = D_MODEL // 2
    n_rows = x.shape[0]
    tm = POST_TILE
    return pl.pallas_call(
        _post_body,
        grid=(n_rows // tm,),
        in_specs=[pl.BlockSpec((tm, D_MODEL), lambda i: (i, 0)),
                  pl.BlockSpec((tm, half), lambda i: (i, 0)),
                  pl.BlockSpec((tm, half), lambda i: (i, 0)),
                  pl.BlockSpec((None, 1, 6 * D_MODEL), lambda i: (_mod_row(i, latent, tm), 0, 0)),
                  _resident((D_MODEL, D_MODEL), lambda i: (0, 0)),
                  _resident((None, 2, D_MODEL), lambda i: (layer, 0, 0)),
                  _resident((None, 2, D_MODEL), lambda i: (layer, 0, 0)),
                  _resident((D_MODEL, D_FF), lambda i: (0, 0)),
                  _resident((D_FF, D_MODEL), lambda i: (0, 0))],
        out_specs=pl.BlockSpec((tm, D_MODEL), lambda i: (i, 0)),
        out_shape=jax.ShapeDtypeStruct((n_rows, D_MODEL), F32),
        compiler_params=_cparams(1),
        name="post_mixer",
    )(x, mix_a, mix_b, mod_l, w_out, ln_g, ln_b, w1, w2)


def _rope_tables(rows, dim):
    row = jnp.repeat(jnp.arange(rows), GRID_W).astype(F32)
    col = jnp.tile(jnp.arange(GRID_W), rows).astype(F32)
    half = dim // 2
    freqs = ROPE_BASE ** (-jnp.arange(0, half, 2, dtype=F32) / half)
    ar = row[:, None] * freqs
    ac = col[:, None] * freqs
    cr, sr, cc, sc = jnp.cos(ar), jnp.sin(ar), jnp.cos(ac), jnp.sin(ac)
    cos = jnp.concatenate([cr, cr, cc, cc], axis=-1)
    sin = jnp.concatenate([-sr, sr, -sc, sc], axis=-1)
    reps = 128 // dim
    return jnp.tile(cos, (1, reps)), jnp.tile(sin, (1, reps))


def _rope128(x, cos, sin):
    lane = lax.broadcasted_iota(jnp.int32, x.shape, 1)
    first = (lane & 31) < 16
    swapped = jnp.where(first, pltpu.roll(x, 112, 1), pltpu.roll(x, 16, 1))
    return x * cos + swapped * sin


RET_CHUNK = 256


def _ret_body(q_ref, k_ref, v_ref, g_ref, rdk_ref, rdh_ref, s0_ref, o_ref, sfin_ref, *, seq):
    C = RET_CHUNK
    n_chunks = seq // C
    scale = DK_RET ** -0.5
    ii = lax.broadcasted_iota(jnp.int32, (C, C), 0)
    jj = lax.broadcasted_iota(jnp.int32, (C, C), 1)
    dist = (ii - jj).astype(F32)
    adist = jnp.abs(dist)
    pos = lax.broadcasted_iota(jnp.int32, (C, H_RET * DK_RET), 0).astype(F32)
    lgf = -jnp.exp(rdk_ref[0])
    lgb = -jnp.exp(rdk_ref[1])
    k_wf = jnp.exp(lgf * (C - 1.0 - pos)) * scale
    k_wb = jnp.exp(lgb * pos) * scale
    q_wf = jnp.exp(lgf * (pos + 1.0))
    q_wb = jnp.exp(lgb * (C - pos))

    uf, ub = [], []
    for n in range(n_chunks):
        kn = k_ref[n * C:(n + 1) * C, :]
        kft = (kn * k_wf).T.astype(BF16)
        kbt = (kn * k_wb).T.astype(BF16)
        ufn, ubn = [], []
        for h in range(H_RET):
            vh = v_ref[n * C:(n + 1) * C, h * DV_RET:(h + 1) * DV_RET].astype(BF16)
            ufn.append(_dot(kft[h * DK_RET:(h + 1) * DK_RET, :], vh))
            ubn.append(_dot(kbt[h * DK_RET:(h + 1) * DK_RET, :], vh))
        uf.append(ufn)
        ub.append(ubn)

    for h in range(H_RET):
        lgf_h = -jnp.exp(rdh_ref[0, h])
        lgb_h = -jnp.exp(rdh_ref[1, h])
        cf = jnp.exp(lgf_h * float(C))[:, 0:DV_RET]
        cb = jnp.exp(lgb_h * float(C))[:, 0:DV_RET]
        dmat = jnp.exp(jnp.where(dist > 0, lgf_h, lgb_h) * adist) + jnp.where(dist == 0, 1.0, 0.0)
        sf = [s0_ref[0, h]]
        for n in range(n_chunks):
            sf.append(cf * sf[n] + uf[n][h])
        sb = [None] * n_chunks
        sb[n_chunks - 1] = s0_ref[1, h]
        for n in range(n_chunks - 1, 0, -1):
            sb[n - 1] = cb * sb[n] + ub[n][h]
        sfin_ref[0, h] = sf[n_chunks]
        sfin_ref[1, h] = cb * sb[0] + ub[0][h]
        lanes = slice(h * DK_RET, (h + 1) * DK_RET)
        vlanes = slice(h * DV_RET, (h + 1) * DV_RET)
        outs = []
        for n in range(n_chunks):
            rows = slice(n * C, (n + 1) * C)
            qh = q_ref[rows, lanes]
            kh = (k_ref[rows, lanes] * scale).astype(BF16)
            vh = v_ref[rows, vlanes].astype(BF16)
            s = _dot_nt(qh.astype(BF16), kh) * dmat
            o = _dot(s.astype(BF16), vh)
            o = o + _dot((qh * q_wf[:, lanes]).astype(BF16), sf[n].astype(BF16))
            outs.append(o + _dot((qh * q_wb[:, lanes]).astype(BF16), sb[n].astype(BF16)))
        for n, y in enumerate(_norm_rows_many(outs)):
            rows = slice(n * C, (n + 1) * C)
            o_ref[rows, vlanes] = (y * _silu(g_ref[rows, vlanes])).astype(BF16)


def _retention(proj, ret_decay_e, s0, *, n_seq, seq):
    rdk = jnp.repeat(ret_decay_e, DK_RET, axis=-1).reshape(2, 1, H_RET * DK_RET)
    rdh = jnp.broadcast_to(ret_decay_e[:, :, None, None], (2, H_RET, 1, RET_CHUNK))
    hk, hv = H_RET * DK_RET, H_RET * DV_RET
    return pl.pallas_call(
        functools.partial(_ret_body, seq=seq),
        grid=(n_seq,),
        in_specs=[pl.BlockSpec((seq, hk), lambda b: (b, E_RQ // hk)),
                  pl.BlockSpec((seq, hk), lambda b: (b, E_RK // hk)),
                  pl.BlockSpec((seq, hv), lambda b: (b, E_RV // hv)),
                  pl.BlockSpec((seq, hv), lambda b: (b, E_RG // hv)),
                  pl.BlockSpec((2, 1, hk), lambda b: (0, 0, 0)),
                  pl.BlockSpec((2, H_RET, 1, RET_CHUNK), lambda b: (0, 0, 0, 0)),
                  pl.BlockSpec((None, 2, H_RET, DK_RET, DV_RET), lambda b: (b, 0, 0, 0, 0))],
        out_specs=[pl.BlockSpec((seq, hv), lambda b: (b, 0)),
                   pl.BlockSpec((None, 2, H_RET, DK_RET, DV_RET), lambda b: (b, 0, 0, 0, 0))],
        out_shape=[jax.ShapeDtypeStruct((n_seq * seq, hv), BF16),
                   jax.ShapeDtypeStruct((n_seq, 2, H_RET, DK_RET, DV_RET), F32)],
        compiler_params=_cparams(1),
        name="retention",
    )(proj, proj, proj, proj, rdk, rdh, s0)


def _cast_weight_chunks(src_refs, dst_refs):
    for src, dst in zip(src_refs, dst_refs):
        dst[...] = src[...].astype(BF16)


def _weight_cast_plan(weights, n_chunks, chunk_of):
    args, in_specs, out_specs, out_shapes = [], [], [], []
    for w, idx in weights:
        rows, cols = w.shape[1], w.shape[2]
        rb = rows // n_chunks
        assert rb * n_chunks == rows and rb % 16 == 0
        args.append(w)
        in_specs.append(pl.BlockSpec((None, rb, cols), lambda *g, idx=idx: (idx, chunk_of(*g), 0)))
        out_specs.append(pl.BlockSpec((rb, cols), lambda *g: (chunk_of(*g), 0)))
        out_shapes.append(jax.ShapeDtypeStruct((rows, cols), BF16))
    return args, in_specs, out_specs, out_shapes


def _col_softmax(parts, extra=None):
    m = parts[0].max(axis=0, keepdims=True)
    for p in parts[1:]:
        m = jnp.maximum(m, p.max(axis=0, keepdims=True))
    if extra is not None:
        m = jnp.maximum(m, extra)
    es = [jnp.exp2(p - m) for p in parts]
    den = es[0].sum(axis=0, keepdims=True)
    for e in es[1:]:
        den = den + e.sum(axis=0, keepdims=True)
    if extra is not None:
        den = den + jnp.exp2(extra - m)
    return es, 1.0 / den


DIFF_TQ = 256
DIFF_CHUNK = 256


def _diff_lambda(p, lam_init):
    a = jnp.sum(p[0:1, :] * p[1:2, :], axis=-1, keepdims=True)
    b = jnp.sum(p[2:3, :] * p[3:4, :], axis=-1, keepdims=True)
    return jnp.exp(a) - jnp.exp(b) + lam_init


def _split_maps(k):
    lane = lax.broadcasted_iota(jnp.int32, k.shape, 1)
    return (jnp.where(lane < DH_DIFF, k, 0.0).astype(BF16), jnp.where(lane >= DH_DIFF, k, 0.0).astype(BF16))


def _diff_latent_body(lam_ref, q_ref, k_ref, v_ref, ck_ref, cv_ref, cq_ref, sq_ref, ckk_ref, skk_ref,
                      *rest, lam_init, n_cast):
    w_src, o_ref, w_dst = rest[:n_cast], rest[n_cast], rest[n_cast + 1:2 * n_cast + 1]
    km_ref, vb_ref, ckm_ref, cvb_ref = rest[2 * n_cast + 1:]
    i = pl.program_id(1)
    _cast_weight_chunks(w_src, w_dst)
    heads = [slice(h * 128, (h + 1) * 128) for h in range(H_DIFF)]

    @pl.when(i == 0)
    def _():
        ckk, skk = ckk_ref[...], skk_ref[...]
        for h, hs in enumerate(heads):
            k1, k2 = _split_maps(_rope128(k_ref[:, hs], ckk, skk))
            km_ref[h, 0] = k1
            km_ref[h, 1] = k2
            vb_ref[h] = v_ref[:, hs].T.astype(BF16)
            c1, c2 = _split_maps(jnp.concatenate([ck_ref[h, 0], ck_ref[h, 1]], axis=1))
            ckm_ref[h, 0] = c1
            ckm_ref[h, 1] = c2
            cvb_ref[h] = cv_ref[h].T.astype(BF16)

    lam = _diff_lambda(lam_ref[...], lam_init)
    cq, sq = cq_ref[...], sq_ref[...]
    qbs = [(_rope128(q_ref[:, hs], cq, sq) * (DH_DIFF ** -0.5 * LOG2E)).astype(BF16) for hs in heads]
    ck = DIFF_CHUNK
    n_chunks = (PAST_LEN + km_ref.shape[2]) // ck
    own = lambda c: slice(c * ck - PAST_LEN, (c + 1) * ck - PAST_LEN)
    keys = lambda h, m, c: ckm_ref[h, m] if c == 0 else km_ref[h, m, own(c), :]
    vals_t = lambda h, c: cvb_ref[h] if c == 0 else vb_ref[h, :, own(c)]

    def col_max(chunks):
        m = chunks[0]
        for s in chunks[1:]:
            m = jnp.maximum(m, s)
        return m.max(axis=0, keepdims=True)

    def add(a, b):
        return b if a is None else a + b

    maps = [(h, m) for h in range(H_DIFF) for m in range(2)]
    logits, exps, psum, acc = {}, {}, {}, {}
    for stage in range(len(maps) + 2):
        p_l = maps[stage] if stage < len(maps) else None
        p_e = maps[stage - 1] if 0 <= stage - 1 < len(maps) else None
        p_v = maps[stage - 2] if 0 <= stage - 2 < len(maps) else None
        if p_l is not None:
            logits[p_l] = []
        if p_e is not None:
            mx = col_max(logits[p_e])
            exps[p_e], psum[p_e] = [], None
        if p_v is not None:
            acc[p_v] = None
        for c in range(n_chunks):
            if p_l is not None:
                logits[p_l].append(_dot_nt(keys(p_l[0], p_l[1], c), qbs[p_l[0]]))
            if p_e is not None:
                e = jnp.exp2(logits[p_e][c] - mx)
                psum[p_e] = add(psum[p_e], e)
                exps[p_e].append(e.astype(BF16))
            if p_v is not None:
                acc[p_v] = add(acc[p_v], _dot(vals_t(p_v[0], c), exps[p_v][c]))
        if p_e is not None:
            del logits[p_e]
        if p_v is not None:
            del exps[p_v]
    outs = []
    for h in range(H_DIFF):
        inv1 = 1.0 / psum[h, 0].sum(axis=0, keepdims=True)
        inv2 = 1.0 / psum[h, 1].sum(axis=0, keepdims=True)
        outs.append((acc[h, 0] * inv1 - acc[h, 1] * (inv2 * lam)).T)
    for hs, y in zip(heads, _norm_rows_many(outs)):
        o_ref[:, hs] = (y * (1.0 - lam_init)).astype(BF16)


def _diff_ctx_body(lam_ref, q_ref, k_ref, v_ref, *rest, lam_init, n_prev):
    o_ref, kc_ref, vc_ref = rest[-3:]
    kc_new, vc_new = kc_ref, vc_ref
    if n_prev:
        kp_ref, vp_ref = rest[0:2]
        for e in range(n_prev):
            kc_ref[e] = kp_ref[e] if n_prev > 1 else kp_ref[...]
            vc_ref[e] = vp_ref[e] if n_prev > 1 else vp_ref[...]
        kc_new, vc_new = kc_ref.at[n_prev], vc_ref.at[n_prev]
    lam = _diff_lambda(lam_ref[...], lam_init)
    heads = [slice(h * 128, (h + 1) * 128) for h in range(H_DIFF)]
    qbs, kms, vbs = [], [], []
    for h, hs in enumerate(heads):
        kk = k_ref[:, hs]
        vv = v_ref[:, hs]
        kc_new[h, 0] = kk[:, 0:DH_DIFF]
        kc_new[h, 1] = kk[:, DH_DIFF:2 * DH_DIFF]
        vc_new[h] = vv
        kms.append(_split_maps(kk))
        vbs.append(vv.astype(BF16))
        qbs.append((q_ref[:, hs] * (DH_DIFF ** -0.5 * LOG2E)).astype(BF16))
    pairs = [(h, m) for h in range(H_DIFF) for m in range(2)]
    logits = {p: _dot_nt(qbs[p[0]], kms[p[0]][p[1]]) for p in pairs}
    exps = {p: jnp.exp2(logits[p] - logits[p].max(axis=-1, keepdims=True)) for p in pairs}
    invs = {p: 1.0 / exps[p].sum(axis=-1, keepdims=True) for p in pairs}
    accs = {p: _dot(exps[p].astype(BF16), vbs[p[0]]) for p in pairs}
    outs = [accs[h, 0] * invs[h, 0] - accs[h, 1] * (invs[h, 1] * lam) for h in range(H_DIFF)]
    for hs, y in zip(heads, _norm_rows_many(outs)):
        o_ref[:, hs] = (y * (1.0 - lam_init)).astype(BF16)


def _diff_attention_ctx(proj, lam_p, lam_init, *, n_seq, seq, prev=None):
    hw = H_DIFF * 128
    n_prev = 0 if prev is None else (1 if prev[0].ndim == 5 else prev[0].shape[1])
    one_k, one_v = (H_DIFF, 2, seq, DH_DIFF), (H_DIFF, seq, DV_DIFF)
    zeros = lambda n: (0,) * n
    in_specs = [pl.BlockSpec((4, DH_DIFF), lambda b: (0, 0)),
                pl.BlockSpec((seq, hw), lambda b: (b, E_DQ // hw)),
                pl.BlockSpec((seq, hw), lambda b: (b, E_DK // hw)),
                pl.BlockSpec((seq, hw), lambda b: (b, E_DV // hw))]
    args = [lam_p, proj, proj, proj]
    if n_prev:
        lead = (None,) if n_prev == 1 else (None, n_prev)
        in_specs += [pl.BlockSpec(lead + one_k, lambda b: (b,) + zeros(len(lead) + 3)),
                     pl.BlockSpec(lead + one_v, lambda b: (b,) + zeros(len(lead) + 2))]
        args += list(prev)
    lead_out = (None, n_prev + 1) if n_prev else (None,)
    out_specs = [pl.BlockSpec((seq, hw), lambda b: (b, 0)),
                 pl.BlockSpec(lead_out + one_k, lambda b: (b,) + zeros(len(lead_out) + 3)),
                 pl.BlockSpec(lead_out + one_v, lambda b: (b,) + zeros(len(lead_out) + 2))]
    out_shape = [jax.ShapeDtypeStruct((n_seq * seq, hw), BF16),
                 jax.ShapeDtypeStruct((n_seq,) + lead_out[1:] + one_k, F32),
                 jax.ShapeDtypeStruct((n_seq,) + lead_out[1:] + one_v, F32)]
    return pl.pallas_call(
        functools.partial(_diff_ctx_body, lam_init=lam_init, n_prev=n_prev),
        grid=(n_seq,),
        in_specs=in_specs,
        out_specs=out_specs,
        out_shape=out_shape,
        compiler_params=_cparams(1),
        name="diff_attention_ctx",
    )(*args)


def _diff_attention_latent(proj, lam_p, lam_init, *, n_seq, seq, cache, rope, cast):
    tq = DIFF_TQ
    nq = seq // tq
    hw = H_DIFF * 128
    ck, cv = cache
    cos, sin = rope
    in_specs = [pl.BlockSpec((4, DH_DIFF), lambda b, i: (0, 0)),
                pl.BlockSpec((tq, hw), lambda b, i: (b * nq + i, E_DQ // hw)),
                pl.BlockSpec((seq, hw), lambda b, i: (b, E_DK // hw)),
                pl.BlockSpec((seq, hw), lambda b, i: (b, E_DV // hw)),
                pl.BlockSpec((None, H_DIFF, 2, PAST_LEN, DH_DIFF), lambda b, i: (b, 0, 0, 0, 0)),
                pl.BlockSpec((None, H_DIFF, PAST_LEN, DV_DIFF), lambda b, i: (b, 0, 0, 0)),
                pl.BlockSpec((tq, 128), lambda b, i: (i, 0)),
                pl.BlockSpec((tq, 128), lambda b, i: (i, 0)),
                pl.BlockSpec((seq, 128), lambda b, i: (0, 0)),
                pl.BlockSpec((seq, 128), lambda b, i: (0, 0))]
    args = [lam_p, proj, proj, proj, ck, cv, cos, sin, cos, sin]
    out_specs = [pl.BlockSpec((tq, hw), lambda b, i: (b * nq + i, 0))]
    out_shape = [jax.ShapeDtypeStruct((n_seq * seq, hw), BF16)]
    w_args, w_in, w_out, w_shape = _weight_cast_plan(cast, n_seq * nq, lambda b, i: b * nq + i)
    scratch = [pltpu.VMEM((H_DIFF, 2, seq, 128), BF16), pltpu.VMEM((H_DIFF, DV_DIFF, seq), BF16),
               pltpu.VMEM((H_DIFF, 2, PAST_LEN, 128), BF16), pltpu.VMEM((H_DIFF, DV_DIFF, PAST_LEN), BF16)]
    return pl.pallas_call(
        functools.partial(_diff_latent_body, lam_init=lam_init, n_cast=len(cast)),
        grid=(n_seq, nq),
        in_specs=in_specs + w_in,
        out_specs=out_specs + w_out,
        out_shape=out_shape + w_shape,
        scratch_shapes=scratch,
        compiler_params=_cparams(2),
        name="diff_attention",
    )(*args, *w_args)


SWA_TQ = 256
SWA_WIN = SWA_TQ + 2 * WINDOW
NEG = -1e30


def _swa_body(*refs, latent, seq, n_prev=0, n_cast=0):
    tq = SWA_TQ
    if latent:
        q_ref, k_ref, v_ref, sink_ref, ck_ref, cv_ref, cq_ref, sq_ref, ckk_ref, skk_ref = refs[:10]
        w_src, o_ref, w_dst = refs[10:10 + n_cast], refs[10 + n_cast], refs[11 + n_cast:11 + 2 * n_cast]
        kr_ref, cvt_ref = refs[11 + 2 * n_cast:]
        _cast_weight_chunks(w_src, w_dst)
        qi = pl.program_id(1)

        @pl.when(qi == 0)
        def _():
            kr_ref[...] = _rope128(k_ref[...], ckk_ref[...], skk_ref[...]).astype(BF16)
            cvt_ref[...] = jnp.concatenate([cv_ref[0], cv_ref[1]], axis=1).T.astype(BF16)

        cq, sq = cq_ref[...], sq_ref[...]
        q = jnp.concatenate([_rope128(q_ref[:, g * 128:(g + 1) * 128], cq, sq)
                             for g in range(H_SWA * DH_SWA // 128)], axis=1)
        ws = pl.multiple_of(jnp.clip(qi * tq - WINDOW, 0, seq - SWA_WIN), 128)
        kw = kr_ref[pl.ds(ws, SWA_WIN), :]
        vwt = v_ref[pl.ds(ws, SWA_WIN), :].T.astype(BF16)
        kpos = ws + lax.broadcasted_iota(jnp.int32, (SWA_WIN, tq), 0)
        qpos = qi * tq + lax.broadcasted_iota(jnp.int32, (SWA_WIN, tq), 1)
        bias1 = jnp.where(jnp.abs(qpos - kpos) <= WINDOW, 0.0, NEG)
        bias = jnp.concatenate([bias1] * SWA_GROUP, axis=1)
    else:
        q_ref, k_ref, v_ref, sink_ref = refs[:4]
        o_ref, kc_ref, vc_ref = refs[-3:]
        q = q_ref[...]
        kk = k_ref[...]
        vv = v_ref[...]
        kc_new, vc_new = kc_ref, vc_ref
        if n_prev:
            kp_ref, vp_ref = refs[4:6]
            for e in range(n_prev):
                kc_ref[e] = kp_ref[e] if n_prev > 1 else kp_ref[...]
                vc_ref[e] = vp_ref[e] if n_prev > 1 else vp_ref[...]
            kc_new, vc_new = kc_ref.at[n_prev], vc_ref.at[n_prev]
        for kv in range(KV_SWA):
            kc_new[kv] = kk[:, kv * DH_SWA:(kv + 1) * DH_SWA]
            vc_new[kv] = vv[:, kv * DH_SWA:(kv + 1) * DH_SWA]
        kw = kk.astype(BF16)
        vwt = vv.T.astype(BF16)
    q = q * (DH_SWA ** -0.5 * LOG2E)
    lanes = [slice(kv * DH_SWA, (kv + 1) * DH_SWA) for kv in range(KV_SWA)]

    def logits(kv):
        q4 = jnp.concatenate([q[:, (kv * SWA_GROUP + g) * DH_SWA:(kv * SWA_GROUP + g + 1) * DH_SWA]
                              for g in range(SWA_GROUP)], axis=0).astype(BF16)
        s_loc = _dot_nt(kw[:, lanes[kv]], q4)
        if latent:
            return [_dot_nt(ck_ref[kv].astype(BF16), q4), s_loc + bias]
        return [s_loc]

    def values(kv, es):
        vals_t = [cvt_ref[lanes[kv], :], vwt[lanes[kv], :]] if latent else [vwt[lanes[kv], :]]
        acc = None
        for e, vt in zip(es, vals_t):
            t = _dot(vt, e.astype(BF16))
            acc = t if acc is None else acc + t
        return acc

    parts = logits(0)
    accs = []
    for kv in range(KV_SWA):
        nxt = logits(kv + 1) if kv + 1 < KV_SWA else None
        es, inv = _col_softmax(parts, extra=sink_ref[kv] * LOG2E)
        accs.append((es, inv))
        parts = nxt
    pieces = []
    for kv, (es, inv) in enumerate(accs):
        acc = values(kv, es) * inv
        pieces += [acc[:, g * tq:(g + 1) * tq] for g in range(SWA_GROUP)]
    o_ref[...] = jnp.concatenate(pieces, axis=0).T.astype(BF16)


def _swa_attention(proj, sink, *, n_seq, seq, cache=None, rope=None, cast=None, prev=None):
    latent = cache is not None
    n_prev = 0
    tq = SWA_TQ
    nq = seq // tq
    hq = H_SWA * DH_SWA
    sink_b = jnp.repeat(sink.reshape(KV_SWA, SWA_GROUP), tq, axis=1).reshape(KV_SWA, 1, SWA_GROUP * tq)
    in_specs = [pl.BlockSpec((tq, hq), lambda b, i: (b * nq + i, 0)),
                pl.BlockSpec((seq, 128), lambda b, i: (b, O_SK // 128)),
                pl.BlockSpec((seq, 128), lambda b, i: (b, O_SV // 128)),
                pl.BlockSpec((KV_SWA, 1, SWA_GROUP * tq), lambda b, i: (0, 0, 0))]
    args = [proj, proj, proj, sink_b]
    scratch = []
    out_specs = [pl.BlockSpec((tq, hq), lambda b, i: (b * nq + i, 0))]
    out_shape = [jax.ShapeDtypeStruct((n_seq * seq, hq), BF16)]
    if latent:
        ck, cv = cache
        cos, sin = rope
        in_specs += [pl.BlockSpec((None, KV_SWA, PAST_LEN, DH_SWA), lambda b, i: (b, 0, 0, 0)),
                     pl.BlockSpec((None, KV_SWA, PAST_LEN, DH_SWA), lambda b, i: (b, 0, 0, 0)),
                     pl.BlockSpec((tq, 128), lambda b, i: (i, 0)),
                     pl.BlockSpec((tq, 128), lambda b, i: (i, 0)),
                     pl.BlockSpec((seq, 128), lambda b, i: (0, 0)),
                     pl.BlockSpec((seq, 128), lambda b, i: (0, 0))]
        args += [ck, cv, cos, sin, cos, sin]
        w_args, w_in, w_out, w_shape = _weight_cast_plan(cast, n_seq * nq, lambda b, i: b * nq + i)
        args += w_args
        in_specs += w_in
        out_specs += w_out
        out_shape += w_shape
        scratch = [pltpu.VMEM((seq, 128), BF16), pltpu.VMEM((KV_SWA * DH_SWA, PAST_LEN), BF16)]
    else:
        n_prev = 0 if prev is None else (1 if prev[0].ndim == 4 else prev[0].shape[1])
        one = (KV_SWA, seq, DH_SWA)
        if n_prev:
            lead = (None,) if n_prev == 1 else (None, n_prev)
            in_specs += [pl.BlockSpec(lead + one, lambda b, i: (b,) + (0,) * (len(lead) + 2))] * 2
            args += list(prev)
        lead_out = (None, n_prev + 1) if n_prev else (None,)
        cache_spec = pl.BlockSpec(lead_out + one, lambda b, i: (b,) + (0,) * (len(lead_out) + 2))
        cache_shape = jax.ShapeDtypeStruct((n_seq,) + lead_out[1:] + one, F32)
        out_specs += [cache_spec, cache_spec]
        out_shape += [cache_shape, cache_shape]
    return pl.pallas_call(
        functools.partial(_swa_body, latent=latent, seq=seq, n_prev=n_prev, n_cast=len(cast or ())),
        grid=(n_seq, nq),
        in_specs=in_specs,
        out_specs=out_specs,
        out_shape=out_shape,
        scratch_shapes=scratch,
        compiler_params=_cparams(2),
        name="swa_attention",
    )(*args)


GLA_BLOCK = 128
GLA_LEVELS = 7
GLA_SAFE = 60.0
GLA_GROUP = 8
GLA_GATE_ROWS = 1024


def _split3(x):
    hi = x.astype(BF16)
    r1 = x - hi.astype(F32)
    mid = r1.astype(BF16)
    lo = (r1 - mid.astype(F32)).astype(BF16)
    return hi, mid, lo


def _gla_gate_body(r_ref, w2_ref, b2_ref, x_ref, tot_ref):
    T = GLA_BLOCK
    ti = lax.broadcasted_iota(jnp.int32, (T, T), 0)
    si = lax.broadcasted_iota(jnp.int32, (T, T), 1)
    r_hi, r_mid, _ = _split3(r_ref[...])
    zs = []
    for d in range(2):
        w_hi, w_mid, _ = _split3(w2_ref[d])
        zs.append(b2_ref[d] + _dot(r_hi, w_hi) + _dot(r_hi, w_mid) + _dot(r_mid, w_hi))
    las = [_split3((jnp.minimum(z, 0.0) - jnp.log(1.0 + jnp.exp(-jnp.abs(z)))) * (1.0 / GLA_TAU)) for z in zs]
    for d in range(2):
        tri = jnp.where((si >= ti) if d else (si <= ti), 1.0, 0.0).astype(BF16)
        l_hi, l_mid, l_lo = las[d]
        for j in range(GLA_GATE_ROWS // T):
            rows = slice(j * T, (j + 1) * T)
            x = _dot(tri, l_hi[rows]) + _dot(tri, l_mid[rows]) + _dot(tri, l_lo[rows])
            x_ref[d, rows, :] = x
            tot_ref[d, j:j + 1, :] = x[0:1, :] if d else x[T - 1:T, :]


def _gla_gates(proj, w2, b2):
    n_rows = proj.shape[0]
    hk = H_GLA * DK_GLA
    tr = GLA_GATE_ROWS
    w2p = jnp.zeros((2, 128, hk), F32)
    w2p = w2p.at[0, 0:GLA_RANK].set(w2[0]).at[1, GLA_RANK:2 * GLA_RANK].set(w2[1])
    return pl.pallas_call(
        _gla_gate_body,
        grid=(n_rows // tr,),
        in_specs=[pl.BlockSpec((tr, 128), lambda i: (i, O_GLR // 128)),
                  pl.BlockSpec((2, 128, hk), lambda i: (0, 0, 0)),
                  pl.BlockSpec((2, 1, hk), lambda i: (0, 0, 0))],
        out_specs=[pl.BlockSpec((2, tr, hk), lambda i: (0, i, 0)),
                   pl.BlockSpec((2, tr // GLA_BLOCK, hk), lambda i: (0, i, 0))],
        out_shape=[jax.ShapeDtypeStruct((2, n_rows, hk), F32),
                   jax.ShapeDtypeStruct((2, n_rows // GLA_BLOCK, hk), F32)],
        compiler_params=_cparams(1),
        name="gla_gates",
    )(proj, w2p, b2.reshape(2, 1, hk))


def _gla_body(*refs, rev, n_blocks, finish):
    if finish:
        (flag_ref, q_ref, k_ref, v_ref, x_ref, s0_ref, of_ref, g_ref, o_ref, sfin_ref, st_ref, att_ref) = refs
    else:
        flag_ref, q_ref, k_ref, v_ref, x_ref, s0_ref, o_ref, sfin_ref, st_ref, att_ref = refs
    n = pl.program_id(1)
    blk = (n_blocks - 1 - n) if rev else n
    safe = flag_ref[pl.program_id(0) * n_blocks + blk] != 0
    T = GLA_BLOCK
    G = q_ref.shape[0]
    hk = H_GLA * DK_GLA
    heads = [slice(h * DK_GLA, (h + 1) * DK_GLA) for h in range(H_GLA)]
    scale = DK_GLA ** -0.5
    ti = lax.broadcasted_iota(jnp.int32, (T, T), 0)
    si = lax.broadcasted_iota(jnp.int32, (T, T), 1)
    causal = (si >= ti) if rev else (si <= ti)

    @pl.when(n == 0)
    def _():
        st_ref[...] = s0_ref[...]

    @pl.when(safe)
    def _():
        for j in range(G):
            x = x_ref[j]
            qs = (q_ref[j] * scale * jnp.exp(x)).astype(BF16)
            ks = (k_ref[j] * jnp.exp(-x)).astype(BF16)
            for h, hs in enumerate(heads):
                att_ref[j, h] = jnp.where(causal, _dot_nt(qs[:, hs], ks[:, hs]), 0.0)

    @pl.when(jnp.logical_not(safe))
    def _():
        t_idx = lax.broadcasted_iota(jnp.int32, (T, hk), 0)
        xr = ti ^ si
        for j in range(G):
            q = q_ref[j] * scale
            k = k_ref[j]
            x = x_ref[j]
            if rev:
                la = jnp.where(t_idx == T - 1, x, x - pltpu.roll(x, T - 1, 0))
            else:
                la = jnp.where(t_idx == 0, x, x - pltpu.roll(x, 1, 0))
            qb, kb = q.astype(BF16), k.astype(BF16)
            att = [_dot_nt(qb[:, hs], kb[:, hs]) for hs in heads]
            xg = la
            tg = la
            for lvl in range(GLA_LEVELS):
                sz = 1 << lvl
                upper = ((t_idx >> lvl) & 1) == 1
                is_q = jnp.logical_not(upper) if rev else upper
                partner = jnp.where(upper, pltpu.roll(tg, sz, 0), pltpu.roll(tg, T - sz, 0))
                e = jnp.exp(jnp.where(is_q, xg, tg - xg))
                qs = jnp.where(is_q, q * e, 0.0).astype(BF16)
                ks = jnp.where(is_q, 0.0, k * e).astype(BF16)
                for h, hs in enumerate(heads):
                    att[h] = jnp.where(xr >= sz, _dot_nt(qs[:, hs], ks[:, hs]), att[h])
                xg = xg + jnp.where(is_q, partner, 0.0)
                tg = tg + partner
            for h in range(H_GLA):
                att_ref[j, h] = att[h]

    prep = []
    for j in range(G):
        x = x_ref[j]
        tot = x[0:1, :] if rev else x[T - 1:T, :]
        v = v_ref[j]
        prep.append(dict(qe=(q_ref[j] * scale * jnp.exp(x)).astype(BF16),
                         kw=(k_ref[j] * jnp.exp(tot - x)).astype(BF16),
                         vt=v.T.astype(BF16),
                         vb=v.astype(BF16), dec=jnp.exp(tot)))
    outs = {}
    for j, p in enumerate(prep):
        for h, hs in enumerate(heads):
            vs = slice(h * DV_GLA, (h + 1) * DV_GLA)
            o = _dot(att_ref[j, h].astype(BF16), p["vb"][:, vs]) + _dot_nt(p["qe"][:, hs], st_ref[j, h].astype(BF16))
            if finish:
                outs[j, h] = o + of_ref[j, :, vs]
            else:
                o_ref[j, :, vs] = o
    for j, p in enumerate(prep):
        for h, hs in enumerate(heads):
            vs = slice(h * DV_GLA, (h + 1) * DV_GLA)
            st_ref[j, h] = st_ref[j, h] * p["dec"][:, hs] + _dot(p["vt"][vs, :], p["kw"][:, hs])
    if finish:
        keys = sorted(outs)
        normed = _norm_rows_many([outs[k] for k in keys])
        for (j, h), y in zip(keys, normed):
            vs = slice(h * DV_GLA, (h + 1) * DV_GLA)
            o_ref[j, :, vs] = (y * _silu(g_ref[j, :, vs])).astype(BF16)

    @pl.when(n == n_blocks - 1)
    def _():
        sfin_ref[...] = st_ref[...]


def _gla(proj, w2, b2, s0t, *, n_seq, seq):
    nb = seq // GLA_BLOCK
    G = min(GLA_GROUP, n_seq)
    hk, hv = H_GLA * DK_GLA, H_GLA * DV_GLA
    x, tot = _gla_gates(proj, w2, b2)
    x = x.reshape(2, n_seq, seq, hk)
    safe = (tot.min(axis=-1) > -GLA_SAFE).reshape(2, n_seq // G, G, nb).all(axis=2)
    flags = safe.astype(jnp.int32).reshape(2, (n_seq // G) * nb)
    p3 = proj.reshape(n_seq, seq, ODD_IN_PAD)
    state_shape = jax.ShapeDtypeStruct((n_seq, H_GLA, DV_GLA, DK_GLA), F32)
    state_spec = pl.BlockSpec((G, H_GLA, DV_GLA, DK_GLA), lambda g, n, f: (g, 0, 0, 0))
    scratch = [pltpu.VMEM((G, H_GLA, DV_GLA, DK_GLA), F32), pltpu.VMEM((G, H_GLA, GLA_BLOCK, GLA_BLOCK), F32)]

    def run(rev, extra_args, extra_cols, out_dtype):
        blk = (lambda n: nb - 1 - n) if rev else (lambda n: n)
        d = 1 if rev else 0
        tok = lambda width, col: pl.BlockSpec((G, GLA_BLOCK, width), lambda g, n, f: (g, blk(n), col))
        grid_spec = pltpu.PrefetchScalarGridSpec(
            num_scalar_prefetch=1,
            grid=(n_seq // G, nb),
            in_specs=[tok(hk, O_GQ // hk), tok(hk, O_GK // hk), tok(hv, O_GV // hv),
                      pl.BlockSpec((None, G, GLA_BLOCK, hk), lambda g, n, f: (d, g, blk(n), 0)),
                      pl.BlockSpec((None, G, H_GLA, DV_GLA, DK_GLA), lambda g, n, f: (d, g, 0, 0, 0))]
            + [tok(hv, col) for col in extra_cols],
            out_specs=[tok(hv, 0), state_spec],
            scratch_shapes=scratch)
        return pl.pallas_call(
            functools.partial(_gla_body, rev=rev, n_blocks=nb, finish=rev),
            grid_spec=grid_spec,
            out_shape=[jax.ShapeDtypeStruct((n_seq, seq, hv), out_dtype), state_shape],
            compiler_params=_cparams(2),
            name="gla_bwd" if rev else "gla_fwd",
        )(flags[d], p3, p3, p3, x, s0t, *extra_args)

    o_f, s_f = run(False, [], [], F32)
    out, s_b = run(True, [o_f, p3], [0, O_GR // hv], BF16)
    return out.reshape(n_seq * seq, hv), jnp.stack([s_f, s_b], axis=0)


def _stacked(caches, n_layers):
    return [c[:, None] for c in caches] if n_layers == 1 else caches


def kernel(x_prompt, x_sample, c, state_ret, cache_diff_k, cache_diff_v, cache_swa_k, cache_swa_v, state_gla,
           c_ctx, w_mod, b_mod, ln_g, ln_b, w_in_even, w_out_even, ret_decay, diff_lam, w_in_odd, w_out_odd,
           swa_sink, gla_w2, gla_b, w_ff1, w_ff2):
    xc = x_prompt.reshape(N_CTX, D_MODEL)
    xl = x_sample.reshape(N_LAT, D_MODEL)
    cond = jnp.concatenate([c_ctx[None, :], c, jnp.zeros((N_COND - 1 - DEC_BATCH, D_MODEL), F32)], axis=0)
    mod = _modulation(cond, w_mod, b_mod).reshape(DEPTH, N_COND, 1, 6 * D_MODEL)
    rope = _rope_tables(DEC_SEQ // GRID_W, DH_DIFF)

    w_in_even_b = w_in_even.astype(BF16)
    w_in_odd_b = _reorder_odd(w_in_odd.astype(BF16))

    ctx = dict(n_seq=BATCH, seq=SEQ)
    lat = dict(n_seq=DEC_BATCH, seq=DEC_SEQ)
    new_ret, new_gla = [], []
    new_diff = new_swa = None
    for l in range(DEPTH):
        if l % 2 == 0:
            e = l // 2
            lam_init = 0.8 - 0.6 * math.exp(-0.3 * l)
            pc = _in_projection(xc, mod[l], w_in_even_b[e], False)
            pl_ = _in_projection(xl, mod[l], w_in_even_b[e], True)
            zero = jnp.zeros((BATCH, 2, H_RET, DK_RET, DV_RET), F32)
            a_ctx, s_ret = _retention(pc, ret_decay[e], zero, **ctx)
            a_lat, _ = _retention(pl_, ret_decay[e], state_ret[:, e], **lat)
            b_ctx, *new_diff = _diff_attention_ctx(pc, diff_lam[e], lam_init, prev=new_diff, **ctx)
            b_lat, *w_post = _diff_attention_latent(pl_, diff_lam[e], lam_init,
                                                    cache=(cache_diff_k[:, e], cache_diff_v[:, e]), rope=rope,
                                                    cast=[(w_out_even, e), (w_ff1, l), (w_ff2, l)], **lat)
            new_ret.append(s_ret)
        else:
            o = l // 2
            pc = _in_projection(xc, mod[l], w_in_odd_b[o], False)
            pl_ = _in_projection(xl, mod[l], w_in_odd_b[o], True)
            a_ctx, *new_swa = _swa_attention(pc, swa_sink[o], prev=new_swa, **ctx)
            a_lat, *w_post = _swa_attention(pl_, swa_sink[o], cache=(cache_swa_k[:, o], cache_swa_v[:, o]),
                                            rope=rope, cast=[(w_out_odd, o), (w_ff1, l), (w_ff2, l)], **lat)
            zero = jnp.zeros((2, BATCH, H_GLA, DV_GLA, DK_GLA), F32)
            b_ctx, s_gla = _gla(pc, gla_w2[o], gla_b[o], zero, **ctx)
            s0t = state_gla[:, o].transpose(1, 0, 2, 4, 3)
            b_lat, _ = _gla(pl_, gla_w2[o], gla_b[o], s0t, **lat)
            new_gla.append(s_gla.transpose(1, 0, 2, 4, 3))
        w_out_b, w1_b, w2_b = w_post
        xc = _post_mixer(xc, a_ctx, b_ctx, mod[l], w_out_b, ln_g, ln_b, w1_b, w2_b, l, False)
        xl = _post_mixer(xl, a_lat, b_lat, mod[l], w_out_b, ln_g, ln_b, w1_b, w2_b, l, True)
    return (xc.reshape(BATCH, SEQ, D_MODEL), xl.reshape(DEC_BATCH, DEC_SEQ, D_MODEL),
            jnp.stack(new_ret, axis=1), *_stacked(new_diff, N_EVEN), *_stacked(new_swa, N_ODD),
            jnp.stack(new_gla, axis=1))
```

```python
import functools
import math

import jax
import jax.numpy as jnp
from jax import lax
from jax.experimental import pallas as pl
from jax.experimental.pallas import tpu as pltpu

F32 = jnp.float32
BF16 = jnp.bfloat16

D_MODEL = 1024
BATCH = 16
SEQ = 256
DEPTH = 4
DEC_BATCH = 4
DEC_SEQ = 2048
PAST_LEN = 256
GRID_W = 64
H_RET, DK_RET, DV_RET = 4, 64, 128
H_DIFF, DH_DIFF, DV_DIFF = 4, 64, 128
H_SWA, KV_SWA, DH_SWA = 8, 2, 64
SWA_GROUP = H_SWA // KV_SWA
WINDOW = 128
H_GLA, DK_GLA, DV_GLA = 4, 64, 128
GLA_RANK = 16
GLA_TAU = 16.0
D_FF = 4 * D_MODEL
ROPE_BASE = 10000.0
N_EVEN = (DEPTH + 1) // 2
N_ODD = DEPTH // 2
ALPHA = (2 * DEPTH) ** 0.25
EVEN_IN = 3072
ODD_IN = 2336
ODD_IN_PAD = 2432
EPS = 1e-5
LOG2E = 1.4426950408889634

E_RQ, E_RK, E_RV, E_RG, E_DQ, E_DK, E_DV = 0, 256, 512, 1024, 1536, 2048, 2560
O_SQ, O_GV, O_GR, O_GQ, O_GK, O_SK, O_SV, O_GLR = 0, 512, 1024, 1536, 1792, 2048, 2176, 2304


def _reorder_odd(w, axis=-1):
    sq, sk, sv, gq, gk, gv, gr, glr = jnp.split(w, [512, 640, 768, 1024, 1280, 1792, 2304], axis=axis)
    pad_shape = list(w.shape)
    pad_shape[axis] = ODD_IN_PAD - ODD_IN
    return jnp.concatenate([sq, gv, gr, gq, gk, sk, sv, glr, jnp.zeros(pad_shape, w.dtype)], axis=axis)


N_CTX = BATCH * SEQ
N_LAT = DEC_BATCH * DEC_SEQ
DENSE_TILE = 512
POST_TILE = 1024
N_COND = 8

VMEM_LIMIT = 56 * 1024 * 1024


def _cparams(n_axes):
    return pltpu.CompilerParams(dimension_semantics=("arbitrary",) * n_axes,
                                vmem_limit_bytes=VMEM_LIMIT)


def _dot(a, b):
    return jnp.dot(a, b, preferred_element_type=F32)


def _dot_nt(a, b):
    return lax.dot_general(a, b, (((1,), (1,)), ((), ())), preferred_element_type=F32)


def _silu(x):
    return x * (1.0 / (1.0 + jnp.exp(-x)))


def _norm_rows(x):
    mu = jnp.mean(x, axis=-1, keepdims=True)
    xc = x - mu
    var = jnp.mean(xc * xc, axis=-1, keepdims=True)
    return xc * lax.rsqrt(var + EPS)


def _norm_rows_many(xs):
    mus = [jnp.mean(x, axis=-1, keepdims=True) for x in xs]
    xcs = [x - mu for x, mu in zip(xs, mus)]
    vs = [jnp.mean(xc * xc, axis=-1, keepdims=True) for xc in xcs]
    return [xc * lax.rsqrt(v + EPS) for xc, v in zip(xcs, vs)]


def _mod_row(i, latent, tile):
    return 1 + i // (DEC_SEQ // tile) if latent else 0


def _resident(shape, index_map):
    return pl.BlockSpec(shape, index_map, pipeline_mode=pl.Buffered(1))


def _mod_body(c_ref, w_ref, b_ref, o_ref):
    c = _silu(c_ref[...]).astype(BF16)
    o_ref[...] = _dot(c, w_ref[...].astype(BF16)) + b_ref[...]


def _modulation(cond, w_mod, b_mod):
    tn = 1536
    return pl.pallas_call(
        _mod_body,
        grid=(DEPTH, 6 * D_MODEL // tn),
        in_specs=[pl.BlockSpec((N_COND, D_MODEL), lambda l, j: (0, 0)),
                  pl.BlockSpec((None, D_MODEL, tn), lambda l, j: (l, 0, j)),
                  pl.BlockSpec((None, 1, tn), lambda l, j: (l, 0, j))],
        out_specs=pl.BlockSpec((None, N_COND, tn), lambda l, j: (l, 0, j)),
        out_shape=jax.ShapeDtypeStruct((DEPTH, N_COND, 6 * D_MODEL), F32),
        compiler_params=_cparams(2),
        name="modulation",
    )(cond, w_mod, b_mod.reshape(DEPTH, 1, 6 * D_MODEL))


def _inproj_body(x_ref, mod_ref, w_ref, o_ref, *, w_transposed):
    sh = mod_ref[:, 0:D_MODEL]
    sc = mod_ref[:, D_MODEL:2 * D_MODEL]
    h = (x_ref[...] * (1.0 + sc) + sh).astype(BF16)
    o_ref[...] = _dot_nt(h, w_ref[...]) if w_transposed else _dot(h, w_ref[...])


def _in_projection(x, mod_l, w, latent, w_transposed=False):
    n_in = w.shape[0] if w_transposed else w.shape[1]
    n_rows = x.shape[0]
    tm = DENSE_TILE
    return pl.pallas_call(
        functools.partial(_inproj_body, w_transposed=w_transposed),
        grid=(n_rows // tm,),
        in_specs=[pl.BlockSpec((tm, D_MODEL), lambda i: (i, 0)),
                  pl.BlockSpec((None, 1, 6 * D_MODEL), lambda i: (_mod_row(i, latent, tm), 0, 0)),
                  _resident(w.shape, lambda i: (0, 0))],
        out_specs=pl.BlockSpec((tm, n_in), lambda i: (i, 0)),
        out_shape=jax.ShapeDtypeStruct((n_rows, n_in), F32),
        compiler_params=_cparams(1),
        name="in_projection",
    )(x, mod_l, w)


def _layer_norm(x, g, b):
    return _norm_rows(x) * g + b


def _post_body(x_ref, ma_ref, mb_ref, mod_ref, wo_ref, g_ref, b_ref, w1_ref, w2_ref, o_ref):
    half = D_MODEL // 2
    gt1 = mod_ref[:, 2 * D_MODEL:3 * D_MODEL]
    sh2 = mod_ref[:, 3 * D_MODEL:4 * D_MODEL]
    sc2 = mod_ref[:, 4 * D_MODEL:5 * D_MODEL]
    gt2 = mod_ref[:, 5 * D_MODEL:6 * D_MODEL]
    chunk = 1024
    n_chunks = D_FF // chunk
    top, bot = slice(0, POST_TILE // 2), slice(POST_TILE // 2, POST_TILE)

    def mix(r):
        return _dot(ma_ref[r, :], wo_ref[0:half, :]) + _dot(mb_ref[r, :], wo_ref[half:D_MODEL, :])

    def norm1(r, m):
        x1 = _layer_norm(ALPHA * x_ref[r, :] + gt1 * m, g_ref[0:1, :], b_ref[0:1, :])
        return x1, (x1 * (1.0 + sc2) + sh2).astype(BF16)

    def ffn(hf, j):
        h1 = _dot(hf, w1_ref[:, j * chunk:(j + 1) * chunk])
        h1 = jnp.square(jnp.maximum(h1, 0.0)).astype(BF16)
        return _dot(h1, w2_ref[j * chunk:(j + 1) * chunk, :])

    def norm2(r, x1, ff):
        o_ref[r, :] = _layer_norm(ALPHA * x1 + gt2 * ff, g_ref[1:2, :], b_ref[1:2, :])

    m_top, m_bot = mix(top), mix(bot)
    x1_top, hf_top = norm1(top, m_top)
    ff_top = ffn(hf_top, 0)
    x1_bot, hf_bot = norm1(bot, m_bot)
    for j in range(1, n_chunks):
        ff_top = ff_top + ffn(hf_top, j)
    ff_bot = ffn(hf_bot, 0)
    norm2(top, x1_top, ff_top)
    for j in range(1, n_chunks):
        ff_bot = ff_bot + ffn(hf_bot, j)
    norm2(bot, x1_bot, ff_bot)


def _post_mixer(x, mix_a, mix_b, mod_l, w_out, ln_g, ln_b, w1, w2, layer, latent):
    half = D_MODEL // 2
    n_rows = x.shape[0]
    tm = POST_TILE
    return pl.pallas_call(
        _post_body,
        grid=(n_rows // tm,),
        in_specs=[pl.BlockSpec((tm, D_MODEL), lambda i: (i, 0)),
                  pl.BlockSpec((tm, half), lambda i: (i, 0)),
                  pl.BlockSpec((tm, half), lambda i: (i, 0)),
                  pl.BlockSpec((None, 1, 6 * D_MODEL), lambda i: (_mod_row(i, latent, tm), 0, 0)),
                  _resident((D_MODEL, D_MODEL), lambda i: (0, 0)),
                  _resident((None, 2, D_MODEL), lambda i: (layer, 0, 0)),
                  _resident((None, 2, D_MODEL), lambda i: (layer, 0, 0)),
                  _resident((D_MODEL, D_FF), lambda i: (0, 0)),
                  _resident((D_FF, D_MODEL), lambda i: (0, 0))],
        out_specs=pl.BlockSpec((tm, D_MODEL), lambda i: (i, 0)),
        out_shape=jax.ShapeDtypeStruct((n_rows, D_MODEL), F32),
        compiler_params=_cparams(1),
        name="post_mixer",
    )(x, mix_a, mix_b, mod_l, w_out, ln_g, ln_b, w1, w2)


def _rope_tables(rows, dim):
    row = jnp.repeat(jnp.arange(rows), GRID_W).astype(F32)
    col = jnp.tile(jnp.arange(GRID_W), rows).astype(F32)
    half = dim // 2
    freqs = ROPE_BASE ** (-jnp.arange(0, half, 2, dtype=F32) / half)
    ar = row[:, None] * freqs
    ac = col[:, None] * freqs
    cr, sr, cc, sc = jnp.cos(ar), jnp.sin(ar), jnp.cos(ac), jnp.sin(ac)
    cos = jnp.concatenate([cr, cr, cc, cc], axis=-1)
    sin = jnp.concatenate([-sr, sr, -sc, sc], axis=-1)
    reps = 128 // dim
    return jnp.tile(cos, (1, reps)), jnp.tile(sin, (1, reps))


def _rope128(x, cos, sin):
    lane = lax.broadcasted_iota(jnp.int32, x.shape, 1)
    first = (lane & 31) < 16
    swapped = jnp.where(first, pltpu.roll(x, 112, 1), pltpu.roll(x, 16, 1))
    return x * cos + swapped * sin


RET_CHUNK = 256


def _ret_body(q_ref, k_ref, v_ref, g_ref, rdk_ref, rdh_ref, s0_ref, o_ref, sfin_ref, *, seq):
    C = RET_CHUNK
    n_chunks = seq // C
    scale = DK_RET ** -0.5
    ii = lax.broadcasted_iota(jnp.int32, (C, C), 0)
    jj = lax.broadcasted_iota(jnp.int32, (C, C), 1)
    dist = (ii - jj).astype(F32)
    adist = jnp.abs(dist)
    pos = lax.broadcasted_iota(jnp.int32, (C, H_RET * DK_RET), 0).astype(F32)
    lgf = -jnp.exp(rdk_ref[0])
    lgb = -jnp.exp(rdk_ref[1])
    k_wf = jnp.exp(lgf * (C - 1.0 - pos)) * scale
    k_wb = jnp.exp(lgb * pos) * scale
    q_wf = jnp.exp(lgf * (pos + 1.0))
    q_wb = jnp.exp(lgb * (C - pos))

    uf, ub = [], []
    for n in range(n_chunks):
        kn = k_ref[n * C:(n + 1) * C, :]
        kft = (kn * k_wf).T.astype(BF16)
        kbt = (kn * k_wb).T.astype(BF16)
        ufn, ubn = [], []
        for h in range(H_RET):
            vh = v_ref[n * C:(n + 1) * C, h * DV_RET:(h + 1) * DV_RET].astype(BF16)
            ufn.append(_dot(kft[h * DK_RET:(h + 1) * DK_RET, :], vh))
            ubn.append(_dot(kbt[h * DK_RET:(h + 1) * DK_RET, :], vh))
        uf.append(ufn)
        ub.append(ubn)

    for h in range(H_RET):
        lgf_h = -jnp.exp(rdh_ref[0, h])
        lgb_h = -jnp.exp(rdh_ref[1, h])
        cf = jnp.exp(lgf_h * float(C))[:, 0:DV_RET]
        cb = jnp.exp(lgb_h * float(C))[:, 0:DV_RET]
        dmat = jnp.exp(jnp.where(dist > 0, lgf_h, lgb_h) * adist) + jnp.where(dist == 0, 1.0, 0.0)
        sf = [s0_ref[0, h]]
        for n in range(n_chunks):
            sf.append(cf * sf[n] + uf[n][h])
        sb = [None] * n_chunks
        sb[n_chunks - 1] = s0_ref[1, h]
        for n in range(n_chunks - 1, 0, -1):
            sb[n - 1] = cb * sb[n] + ub[n][h]
        sfin_ref[0, h] = sf[n_chunks]
        sfin_ref[1, h] = cb * sb[0] + ub[0][h]
        lanes = slice(h * DK_RET, (h + 1) * DK_RET)
        vlanes = slice(h * DV_RET, (h + 1) * DV_RET)
        outs = []
        for n in range(n_chunks):
            rows = slice(n * C, (n + 1) * C)
            qh = q_ref[rows, lanes]
            kh = (k_ref[rows, lanes] * scale).astype(BF16)
            vh = v_ref[rows, vlanes].astype(BF16)
            s = _dot_nt(qh.astype(BF16), kh) * dmat
            o = _dot(s.astype(BF16), vh)
            o = o + _dot((qh * q_wf[:, lanes]).astype(BF16), sf[n].astype(BF16))
            outs.append(o + _dot((qh * q_wb[:, lanes]).astype(BF16), sb[n].astype(BF16)))
        for n, y in enumerate(_norm_rows_many(outs)):
            rows = slice(n * C, (n + 1) * C)
            o_ref[rows, vlanes] = (y * _silu(g_ref[rows, vlanes])).astype(BF16)


def _retention(proj, ret_decay_e, s0, *, n_seq, seq):
    rdk = jnp.repeat(ret_decay_e, DK_RET, axis=-1).reshape(2, 1, H_RET * DK_RET)
    rdh = jnp.broadcast_to(ret_decay_e[:, :, None, None], (2, H_RET, 1, RET_CHUNK))
    hk, hv = H_RET * DK_RET, H_RET * DV_RET
    return pl.pallas_call(
        functools.partial(_ret_body, seq=seq),
        grid=(n_seq,),
        in_specs=[pl.BlockSpec((seq, hk), lambda b: (b, E_RQ // hk)),
                  pl.BlockSpec((seq, hk), lambda b: (b, E_RK // hk)),
                  pl.BlockSpec((seq, hv), lambda b: (b, E_RV // hv)),
                  pl.BlockSpec((seq, hv), lambda b: (b, E_RG // hv)),
                  pl.BlockSpec((2, 1, hk), lambda b: (0, 0, 0)),
                  pl.BlockSpec((2, H_RET, 1, RET_CHUNK), lambda b: (0, 0, 0, 0)),
                  pl.BlockSpec((None, 2, H_RET, DK_RET, DV_RET), lambda b: (b, 0, 0, 0, 0))],
        out_specs=[pl.BlockSpec((seq, hv), lambda b: (b, 0)),
                   pl.BlockSpec((None, 2, H_RET, DK_RET, DV_RET), lambda b: (b, 0, 0, 0, 0))],
        out_shape=[jax.ShapeDtypeStruct((n_seq * seq, hv), BF16),
                   jax.ShapeDtypeStruct((n_seq, 2, H_RET, DK_RET, DV_RET), F32)],
        compiler_params=_cparams(1),
        name="retention",
    )(proj, proj, proj, proj, rdk, rdh, s0)


def _cast_weight_chunks(src_refs, dst_refs):
    for src, dst in zip(src_refs, dst_refs):
        dst[...] = src[...].astype(BF16)


def _weight_cast_plan(weights, n_chunks, chunk_of):
    args, in_specs, out_specs, out_shapes = [], [], [], []
    for w, idx in weights:
        rows, cols = w.shape[1], w.shape[2]
        rb = rows // n_chunks
        assert rb * n_chunks == rows and rb % 16 == 0
        args.append(w)
        in_specs.append(pl.BlockSpec((None, rb, cols), lambda *g, idx=idx: (idx, chunk_of(*g), 0)))
        out_specs.append(pl.BlockSpec((rb, cols), lambda *g: (chunk_of(*g), 0)))
        out_shapes.append(jax.ShapeDtypeStruct((rows, cols), BF16))
    return args, in_specs, out_specs, out_shapes


def _col_softmax(parts, extra=None):
    m = parts[0].max(axis=0, keepdims=True)
    for p in parts[1:]:
        m = jnp.maximum(m, p.max(axis=0, keepdims=True))
    if extra is not None:
        m = jnp.maximum(m, extra)
    es = [jnp.exp2(p - m) for p in parts]
    den = es[0].sum(axis=0, keepdims=True)
    for e in es[1:]:
        den = den + e.sum(axis=0, keepdims=True)
    if extra is not None:
        den = den + jnp.exp2(extra - m)
    return es, 1.0 / den


DIFF_TQ = 256
DIFF_CHUNK = 256


def _diff_lambda(p, lam_init):
    a = jnp.sum(p[0:1, :] * p[1:2, :], axis=-1, keepdims=True)
    b = jnp.sum(p[2:3, :] * p[3:4, :], axis=-1, keepdims=True)
    return jnp.exp(a) - jnp.exp(b) + lam_init


def _split_maps(k):
    lane = lax.broadcasted_iota(jnp.int32, k.shape, 1)
    return (jnp.where(lane < DH_DIFF, k, 0.0).astype(BF16), jnp.where(lane >= DH_DIFF, k, 0.0).astype(BF16))


def _diff_latent_body(lam_ref, q_ref, k_ref, v_ref, ck_ref, cv_ref, cq_ref, sq_ref, ckk_ref, skk_ref,
                      *rest, lam_init, n_cast):
    w_src, o_ref, w_dst = rest[:n_cast], rest[n_cast], rest[n_cast + 1:2 * n_cast + 1]
    km_ref, vb_ref, ckm_ref, cvb_ref = rest[2 * n_cast + 1:]
    i = pl.program_id(1)
    _cast_weight_chunks(w_src, w_dst)
    heads = [slice(h * 128, (h + 1) * 128) for h in range(H_DIFF)]

    @pl.when(i == 0)
    def _():
        ckk, skk = ckk_ref[...], skk_ref[...]
        for h, hs in enumerate(heads):
            k1, k2 = _split_maps(_rope128(k_ref[:, hs], ckk, skk))
            km_ref[h, 0] = k1
            km_ref[h, 1] = k2
            vb_ref[h] = v_ref[:, hs].T.astype(BF16)
            c1, c2 = _split_maps(jnp.concatenate([ck_ref[h, 0], ck_ref[h, 1]], axis=1))
            ckm_ref[h, 0] = c1
            ckm_ref[h, 1] = c2
            cvb_ref[h] = cv_ref[h].T.astype(BF16)

    lam = _diff_lambda(lam_ref[...], lam_init)
    cq, sq = cq_ref[...], sq_ref[...]
    qbs = [(_rope128(q_ref[:, hs], cq, sq) * (DH_DIFF ** -0.5 * LOG2E)).astype(BF16) for hs in heads]
    ck = DIFF_CHUNK
    n_chunks = (PAST_LEN + km_ref.shape[2]) // ck
    own = lambda c: slice(c * ck - PAST_LEN, (c + 1) * ck - PAST_LEN)
    keys = lambda h, m, c: ckm_ref[h, m] if c == 0 else km_ref[h, m, own(c), :]
    vals_t = lambda h, c: cvb_ref[h] if c == 0 else vb_ref[h, :, own(c)]

    def col_max(chunks):
        m = chunks[0]
        for s in chunks[1:]:
            m = jnp.maximum(m, s)
        return m.max(axis=0, keepdims=True)

    def add(a, b):
        return b if a is None else a + b

    maps = [(h, m) for h in range(H_DIFF) for m in range(2)]
    logits, exps, psum, acc = {}, {}, {}, {}
    for stage in range(len(maps) + 2):
        p_l = maps[stage] if stage < len(maps) else None
        p_e = maps[stage - 1] if 0 <= stage - 1 < len(maps) else None
        p_v = maps[stage - 2] if 0 <= stage - 2 < len(maps) else None
        if p_l is not None:
            logits[p_l] = []
        if p_e is not None:
            mx = col_max(logits[p_e])
            exps[p_e], psum[p_e] = [], None
        if p_v is not None:
            acc[p_v] = None
        for c in range(n_chunks):
            if p_l is not None:
                logits[p_l].append(_dot_nt(keys(p_l[0], p_l[1], c), qbs[p_l[0]]))
            if p_e is not None:
                e = jnp.exp2(logits[p_e][c] - mx)
                psum[p_e] = add(psum[p_e], e)
                exps[p_e].append(e.astype(BF16))
            if p_v is not None:
                acc[p_v] = add(acc[p_v], _dot(vals_t(p_v[0], c), exps[p_v][c]))
        if p_e is not None:
            del logits[p_e]
        if p_v is not None:
            del exps[p_v]
    outs = []
    for h in range(H_DIFF):
        inv1 = 1.0 / psum[h, 0].sum(axis=0, keepdims=True)
        inv2 = 1.0 / psum[h, 1].sum(axis=0, keepdims=True)
        outs.append((acc[h, 0] * inv1 - acc[h, 1] * (inv2 * lam)).T)
    for hs, y in zip(heads, _norm_rows_many(outs)):
        o_ref[:, hs] = (y * (1.0 - lam_init)).astype(BF16)


def _diff_ctx_body(lam_ref, q_ref, k_ref, v_ref, *rest, lam_init, n_prev):
    o_ref, kc_ref, vc_ref = rest[-3:]
    kc_new, vc_new = kc_ref, vc_ref
    if n_prev:
        kp_ref, vp_ref = rest[0:2]
        for e in range(n_prev):
            kc_ref[e] = kp_ref[e] if n_prev > 1 else kp_ref[...]
            vc_ref[e] = vp_ref[e] if n_prev > 1 else vp_ref[...]
        kc_new, vc_new = kc_ref.at[n_prev], vc_ref.at[n_prev]
    lam = _diff_lambda(lam_ref[...], lam_init)
    heads = [slice(h * 128, (h + 1) * 128) for h in range(H_DIFF)]
    qbs, kms, vbs = [], [], []
    for h, hs in enumerate(heads):
        kk = k_ref[:, hs]
        vv = v_ref[:, hs]
        kc_new[h, 0] = kk[:, 0:DH_DIFF]
        kc_new[h, 1] = kk[:, DH_DIFF:2 * DH_DIFF]
        vc_new[h] = vv
        kms.append(_split_maps(kk))
        vbs.append(vv.astype(BF16))
        qbs.append((q_ref[:, hs] * (DH_DIFF ** -0.5 * LOG2E)).astype(BF16))
    pairs = [(h, m) for h in range(H_DIFF) for m in range(2)]
    logits = {p: _dot_nt(qbs[p[0]], kms[p[0]][p[1]]) for p in pairs}
    exps = {p: jnp.exp2(logits[p] - logits[p].max(axis=-1, keepdims=True)) for p in pairs}
    invs = {p: 1.0 / exps[p].sum(axis=-1, keepdims=True) for p in pairs}
    accs = {p: _dot(exps[p].astype(BF16), vbs[p[0]]) for p in pairs}
    outs = [accs[h, 0] * invs[h, 0] - accs[h, 1] * (invs[h, 1] * lam) for h in range(H_DIFF)]
    for hs, y in zip(heads, _norm_rows_many(outs)):
        o_ref[:, hs] = (y * (1.0 - lam_init)).astype(BF16)


def _diff_attention_ctx(proj, lam_p, lam_init, *, n_seq, seq, prev=None):
    hw = H_DIFF * 128
    n_prev = 0 if prev is None else (1 if prev[0].ndim == 5 else prev[0].shape[1])
    one_k, one_v = (H_DIFF, 2, seq, DH_DIFF), (H_DIFF, seq, DV_DIFF)
    zeros = lambda n: (0,) * n
    in_specs = [pl.BlockSpec((4, DH_DIFF), lambda b: (0, 0)),
                pl.BlockSpec((seq, hw), lambda b: (b, E_DQ // hw)),
                pl.BlockSpec((seq, hw), lambda b: (b, E_DK // hw)),
                pl.BlockSpec((seq, hw), lambda b: (b, E_DV // hw))]
    args = [lam_p, proj, proj, proj]
    if n_prev:
        lead = (None,) if n_prev == 1 else (None, n_prev)
        in_specs += [pl.BlockSpec(lead + one_k, lambda b: (b,) + zeros(len(lead) + 3)),
                     pl.BlockSpec(lead + one_v, lambda b: (b,) + zeros(len(lead) + 2))]
        args += list(prev)
    lead_out = (None, n_prev + 1) if n_prev else (None,)
    out_specs = [pl.BlockSpec((seq, hw), lambda b: (b, 0)),
                 pl.BlockSpec(lead_out + one_k, lambda b: (b,) + zeros(len(lead_out) + 3)),
                 pl.BlockSpec(lead_out + one_v, lambda b: (b,) + zeros(len(lead_out) + 2))]
    out_shape = [jax.ShapeDtypeStruct((n_seq * seq, hw), BF16),
                 jax.ShapeDtypeStruct((n_seq,) + lead_out[1:] + one_k, F32),
                 jax.ShapeDtypeStruct((n_seq,) + lead_out[1:] + one_v, F32)]
    return pl.pallas_call(
        functools.partial(_diff_ctx_body, lam_init=lam_init, n_prev=n_prev),
        grid=(n_seq,),
        in_specs=in_specs,
        out_specs=out_specs,
        out_shape=out_shape,
        compiler_params=_cparams(1),
        name="diff_attention_ctx",
    )(*args)


def _diff_attention_latent(proj, lam_p, lam_init, *, n_seq, seq, cache, rope, cast):
    tq = DIFF_TQ
    nq = seq // tq
    hw = H_DIFF * 128
    ck, cv = cache
    cos, sin = rope
    in_specs = [pl.BlockSpec((4, DH_DIFF), lambda b, i: (0, 0)),
                pl.BlockSpec((tq, hw), lambda b, i: (b * nq + i, E_DQ // hw)),
                pl.BlockSpec((seq, hw), lambda b, i: (b, E_DK // hw)),
                pl.BlockSpec((seq, hw), lambda b, i: (b, E_DV // hw)),
                pl.BlockSpec((None, H_DIFF, 2, PAST_LEN, DH_DIFF), lambda b, i: (b, 0, 0, 0, 0)),
                pl.BlockSpec((None, H_DIFF, PAST_LEN, DV_DIFF), lambda b, i: (b, 0, 0, 0)),
                pl.BlockSpec((tq, 128), lambda b, i: (i, 0)),
                pl.BlockSpec((tq, 128), lambda b, i: (i, 0)),
                pl.BlockSpec((seq, 128), lambda b, i: (0, 0)),
                pl.BlockSpec((seq, 128), lambda b, i: (0, 0))]
    args = [lam_p, proj, proj, proj, ck, cv, cos, sin, cos, sin]
    out_specs = [pl.BlockSpec((tq, hw), lambda b, i: (b * nq + i, 0))]
    out_shape = [jax.ShapeDtypeStruct((n_seq * seq, hw), BF16)]
    w_args, w_in, w_out, w_shape = _weight_cast_plan(cast, n_seq * nq, lambda b, i: b * nq + i)
    scratch = [pltpu.VMEM((H_DIFF, 2, seq, 128), BF16), pltpu.VMEM((H_DIFF, DV_DIFF, seq), BF16),
               pltpu.VMEM((H_DIFF, 2, PAST_LEN, 128), BF16), pltpu.VMEM((H_DIFF, DV_DIFF, PAST_LEN), BF16)]
    return pl.pallas_call(
        functools.partial(_diff_latent_body, lam_init=lam_init, n_cast=len(cast)),
        grid=(n_seq, nq),
        in_specs=in_specs + w_in,
        out_specs=out_specs + w_out,
        out_shape=out_shape + w_shape,
        scratch_shapes=scratch,
        compiler_params=_cparams(2),
        name="diff_attention",
    )(*args, *w_args)


SWA_TQ = 256
SWA_WIN = SWA_TQ + 2 * WINDOW
NEG = -1e30


def _swa_body(*refs, latent, seq, n_prev=0, n_cast=0):
    tq = SWA_TQ
    if latent:
        q_ref, k_ref, v_ref, sink_ref, ck_ref, cv_ref, cq_ref, sq_ref, ckk_ref, skk_ref = refs[:10]
        w_src, o_ref, w_dst = refs[10:10 + n_cast], refs[10 + n_cast], refs[11 + n_cast:11 + 2 * n_cast]
        kr_ref, cvt_ref = refs[11 + 2 * n_cast:]
        _cast_weight_chunks(w_src, w_dst)
        qi = pl.program_id(1)

        @pl.when(qi == 0)
        def _():
            kr_ref[...] = _rope128(k_ref[...], ckk_ref[...], skk_ref[...]).astype(BF16)
            cvt_ref[...] = jnp.concatenate([cv_ref[0], cv_ref[1]], axis=1).T.astype(BF16)

        cq, sq = cq_ref[...], sq_ref[...]
        q = jnp.concatenate([_rope128(q_ref[:, g * 128:(g + 1) * 128], cq, sq)
                             for g in range(H_SWA * DH_SWA // 128)], axis=1)
        ws = pl.multiple_of(jnp.clip(qi * tq - WINDOW, 0, seq - SWA_WIN), 128)
        kw = kr_ref[pl.ds(ws, SWA_WIN), :]
        vwt = v_ref[pl.ds(ws, SWA_WIN), :].T.astype(BF16)
        kpos = ws + lax.broadcasted_iota(jnp.int32, (SWA_WIN, tq), 0)
        qpos = qi * tq + lax.broadcasted_iota(jnp.int32, (SWA_WIN, tq), 1)
        bias1 = jnp.where(jnp.abs(qpos - kpos) <= WINDOW, 0.0, NEG)
        bias = jnp.concatenate([bias1] * SWA_GROUP, axis=1)
    else:
        q_ref, k_ref, v_ref, sink_ref = refs[:4]
        o_ref, kc_ref, vc_ref = refs[-3:]
        q = q_ref[...]
        kk = k_ref[...]
        vv = v_ref[...]
        kc_new, vc_new = kc_ref, vc_ref
        if n_prev:
            kp_ref, vp_ref = refs[4:6]
            for e in range(n_prev):
                kc_ref[e] = kp_ref[e] if n_prev > 1 else kp_ref[...]
                vc_ref[e] = vp_ref[e] if n_prev > 1 else vp_ref[...]
            kc_new, vc_new = kc_ref.at[n_prev], vc_ref.at[n_prev]
        for kv in range(KV_SWA):
            kc_new[kv] = kk[:, kv * DH_SWA:(kv + 1) * DH_SWA]
            vc_new[kv] = vv[:, kv * DH_SWA:(kv + 1) * DH_SWA]
        kw = kk.astype(BF16)
        vwt = vv.T.astype(BF16)
    q = q * (DH_SWA ** -0.5 * LOG2E)
    lanes = [slice(kv * DH_SWA, (kv + 1) * DH_SWA) for kv in range(KV_SWA)]

    def logits(kv):
        q4 = jnp.concatenate([q[:, (kv * SWA_GROUP + g) * DH_SWA:(kv * SWA_GROUP + g + 1) * DH_SWA]
                              for g in range(SWA_GROUP)], axis=0).astype(BF16)
        s_loc = _dot_nt(kw[:, lanes[kv]], q4)
        if latent:
            return [_dot_nt(ck_ref[kv].astype(BF16), q4), s_loc + bias]
        return [s_loc]

    def values(kv, es):
        vals_t = [cvt_ref[lanes[kv], :], vwt[lanes[kv], :]] if latent else [vwt[lanes[kv], :]]
        acc = None
        for e, vt in zip(es, vals_t):
            t = _dot(vt, e.astype(BF16))
            acc = t if acc is None else acc + t
        return acc

    parts = logits(0)
    accs = []
    for kv in range(KV_SWA):
        nxt = logits(kv + 1) if kv + 1 < KV_SWA else None
        es, inv = _col_softmax(parts, extra=sink_ref[kv] * LOG2E)
        accs.append((es, inv))
        parts = nxt
    pieces = []
    for kv, (es, inv) in enumerate(accs):
        acc = values(kv, es) * inv
        pieces += [acc[:, g * tq:(g + 1) * tq] for g in range(SWA_GROUP)]
    o_ref[...] = jnp.concatenate(pieces, axis=0).T.astype(BF16)


def _swa_attention(proj, sink, *, n_seq, seq, cache=None, rope=None, cast=None, prev=None):
    latent = cache is not None
    n_prev = 0
    tq = SWA_TQ
    nq = seq // tq
    hq = H_SWA * DH_SWA
    sink_b = jnp.repeat(sink.reshape(KV_SWA, SWA_GROUP), tq, axis=1).reshape(KV_SWA, 1, SWA_GROUP * tq)
    in_specs = [pl.BlockSpec((tq, hq), lambda b, i: (b * nq + i, 0)),
                pl.BlockSpec((seq, 128), lambda b, i: (b, O_SK // 128)),
                pl.BlockSpec((seq, 128), lambda b, i: (b, O_SV // 128)),
                pl.BlockSpec((KV_SWA, 1, SWA_GROUP * tq), lambda b, i: (0, 0, 0))]
    args = [proj, proj, proj, sink_b]
    scratch = []
    out_specs = [pl.BlockSpec((tq, hq), lambda b, i: (b * nq + i, 0))]
    out_shape = [jax.ShapeDtypeStruct((n_seq * seq, hq), BF16)]
    if latent:
        ck, cv = cache
        cos, sin = rope
        in_specs += [pl.BlockSpec((None, KV_SWA, PAST_LEN, DH_SWA), lambda b, i: (b, 0, 0, 0)),
                     pl.BlockSpec((None, KV_SWA, PAST_LEN, DH_SWA), lambda b, i: (b, 0, 0, 0)),
                     pl.BlockSpec((tq, 128), lambda b, i: (i, 0)),
                     pl.BlockSpec((tq, 128), lambda b, i: (i, 0)),
                     pl.BlockSpec((seq, 128), lambda b, i: (0, 0)),
                     pl.BlockSpec((seq, 128), lambda b, i: (0, 0))]
        args += [ck, cv, cos, sin, cos, sin]
        w_args, w_in, w_out, w_shape = _weight_cast_plan(cast, n_seq * nq, lambda b, i: b * nq + i)
        args += w_args
        in_specs += w_in
        out_specs += w_out
        out_shape += w_shape
        scratch = [pltpu.VMEM((seq, 128), BF16), pltpu.VMEM((KV_SWA * DH_SWA, PAST_LEN), BF16)]
    else:
        n_prev = 0 if prev is None else (1 if prev[0].ndim == 4 else prev[0].shape[1])
        one = (KV_SWA, seq, DH_SWA)
        if n_prev:
            lead = (None,) if n_prev == 1 else (None, n_prev)
            in_specs += [pl.BlockSpec(lead + one, lambda b, i: (b,) + (0,) * (len(lead) + 2))] * 2
            args += list(prev)
        lead_out = (None, n_prev + 1) if n_prev else (None,)
        cache_spec = pl.BlockSpec(lead_out + one, lambda b, i: (b,) + (0,) * (len(lead_out) + 2))
        cache_shape = jax.ShapeDtypeStruct((n_seq,) + lead_out[1:] + one, F32)
        out_specs += [cache_spec, cache_spec]
        out_shape += [cache_shape, cache_shape]
    return pl.pallas_call(
        functools.partial(_swa_body, latent=latent, seq=seq, n_prev=n_prev, n_cast=len(cast or ())),
        grid=(n_seq, nq),
        in_specs=in_specs,
        out_specs=out_specs,
        out_shape=out_shape,
        scratch_shapes=scratch,
        compiler_params=_cparams(2),
        name="swa_attention",
    )(*args)


GLA_BLOCK = 128
GLA_LEVELS = 7
GLA_SAFE = 60.0
GLA_GROUP = 8
GLA_GATE_ROWS = 1024


def _split3(x):
    hi = x.astype(BF16)
    r1 = x - hi.astype(F32)
    mid = r1.astype(BF16)
    lo = (r1 - mid.astype(F32)).astype(BF16)
    return hi, mid, lo


def _gla_gate_body(r_ref, w2_ref, b2_ref, x_ref, tot_ref):
    T = GLA_BLOCK
    ti = lax.broadcasted_iota(jnp.int32, (T, T), 0)
    si = lax.broadcasted_iota(jnp.int32, (T, T), 1)
    r_hi, r_mid, _ = _split3(r_ref[...])
    zs = []
    for d in range(2):
        w_hi, w_mid, _ = _split3(w2_ref[d])
        zs.append(b2_ref[d] + _dot(r_hi, w_hi) + _dot(r_hi, w_mid) + _dot(r_mid, w_hi))
    las = [_split3((jnp.minimum(z, 0.0) - jnp.log(1.0 + jnp.exp(-jnp.abs(z)))) * (1.0 / GLA_TAU)) for z in zs]
    for d in range(2):
        tri = jnp.where((si >= ti) if d else (si <= ti), 1.0, 0.0).astype(BF16)
        l_hi, l_mid, l_lo = las[d]
        for j in range(GLA_GATE_ROWS // T):
            rows = slice(j * T, (j + 1) * T)
            x = _dot(tri, l_hi[rows]) + _dot(tri, l_mid[rows]) + _dot(tri, l_lo[rows])
            x_ref[d, rows, :] = x
            tot_ref[d, j:j + 1, :] = x[0:1, :] if d else x[T - 1:T, :]


def _gla_gates(proj, w2, b2):
    n_rows = proj.shape[0]
    hk = H_GLA * DK_GLA
    tr = GLA_GATE_ROWS
    w2p = jnp.zeros((2, 128, hk), F32)
    w2p = w2p.at[0, 0:GLA_RANK].set(w2[0]).at[1, GLA_RANK:2 * GLA_RANK].set(w2[1])
    return pl.pallas_call(
        _gla_gate_body,
        grid=(n_rows // tr,),
        in_specs=[pl.BlockSpec((tr, 128), lambda i: (i, O_GLR // 128)),
                  pl.BlockSpec((2, 128, hk), lambda i: (0, 0, 0)),
                  pl.BlockSpec((2, 1, hk), lambda i: (0, 0, 0))],
        out_specs=[pl.BlockSpec((2, tr, hk), lambda i: (0, i, 0)),
                   pl.BlockSpec((2, tr // GLA_BLOCK, hk), lambda i: (0, i, 0))],
        out_shape=[jax.ShapeDtypeStruct((2, n_rows, hk), F32),
                   jax.ShapeDtypeStruct((2, n_rows // GLA_BLOCK, hk), F32)],
        compiler_params=_cparams(1),
        name="gla_gates",
    )(proj, w2p, b2.reshape(2, 1, hk))


def _gla_body(*refs, rev, n_blocks, finish):
    if finish:
        (flag_ref, q_ref, k_ref, v_ref, x_ref, s0_ref, of_ref, g_ref, o_ref, sfin_ref, st_ref, att_ref) = refs
    else:
        flag_ref, q_ref, k_ref, v_ref, x_ref, s0_ref, o_ref, sfin_ref, st_ref, att_ref = refs
    n = pl.program_id(1)
    blk = (n_blocks - 1 - n) if rev else n
    safe = flag_ref[pl.program_id(0) * n_blocks + blk] != 0
    T = GLA_BLOCK
    G = q_ref.shape[0]
    hk = H_GLA * DK_GLA
    heads = [slice(h * DK_GLA, (h + 1) * DK_GLA) for h in range(H_GLA)]
    scale = DK_GLA ** -0.5
    ti = lax.broadcasted_iota(jnp.int32, (T, T), 0)
    si = lax.broadcasted_iota(jnp.int32, (T, T), 1)
    causal = (si >= ti) if rev else (si <= ti)

    @pl.when(n == 0)
    def _():
        st_ref[...] = s0_ref[...]

    @pl.when(safe)
    def _():
        for j in range(G):
            x = x_ref[j]
            qs = (q_ref[j] * scale * jnp.exp(x)).astype(BF16)
            ks = (k_ref[j] * jnp.exp(-x)).astype(BF16)
            for h, hs in enumerate(heads):
                att_ref[j, h] = jnp.where(causal, _dot_nt(qs[:, hs], ks[:, hs]), 0.0)

    @pl.when(jnp.logical_not(safe))
    def _():
        t_idx = lax.broadcasted_iota(jnp.int32, (T, hk), 0)
        xr = ti ^ si
        for j in range(G):
            q = q_ref[j] * scale
            k = k_ref[j]
            x = x_ref[j]
            if rev:
                la = jnp.where(t_idx == T - 1, x, x - pltpu.roll(x, T - 1, 0))
            else:
                la = jnp.where(t_idx == 0, x, x - pltpu.roll(x, 1, 0))
            qb, kb = q.astype(BF16), k.astype(BF16)
            att = [_dot_nt(qb[:, hs], kb[:, hs]) for hs in heads]
            xg = la
            tg = la
            for lvl in range(GLA_LEVELS):
                sz = 1 << lvl
                upper = ((t_idx >> lvl) & 1) == 1
                is_q = jnp.logical_not(upper) if rev else upper
                partner = jnp.where(upper, pltpu.roll(tg, sz, 0), pltpu.roll(tg, T - sz, 0))
                e = jnp.exp(jnp.where(is_q, xg, tg - xg))
                qs = jnp.where(is_q, q * e, 0.0).astype(BF16)
                ks = jnp.where(is_q, 0.0, k * e).astype(BF16)
                for h, hs in enumerate(heads):
                    att[h] = jnp.where(xr >= sz, _dot_nt(qs[:, hs], ks[:, hs]), att[h])
                xg = xg + jnp.where(is_q, partner, 0.0)
                tg = tg + partner
            for h in range(H_GLA):
                att_ref[j, h] = att[h]

    prep = []
    for j in range(G):
        x = x_ref[j]
        tot = x[0:1, :] if rev else x[T - 1:T, :]
        v = v_ref[j]
        prep.append(dict(qe=(q_ref[j] * scale * jnp.exp(x)).astype(BF16),
                         kw=(k_ref[j] * jnp.exp(tot - x)).astype(BF16),
                         vt=v.T.astype(BF16),
                         vb=v.astype(BF16), dec=jnp.exp(tot)))
    outs = {}
    for j, p in enumerate(prep):
        for h, hs in enumerate(heads):
            vs = slice(h * DV_GLA, (h + 1) * DV_GLA)
            o = _dot(att_ref[j, h].astype(BF16), p["vb"][:, vs]) + _dot_nt(p["qe"][:, hs], st_ref[j, h].astype(BF16))
            if finish:
                outs[j, h] = o + of_ref[j, :, vs]
            else:
                o_ref[j, :, vs] = o
    for j, p in enumerate(prep):
        for h, hs in enumerate(heads):
            vs = slice(h * DV_GLA, (h + 1) * DV_GLA)
            st_ref[j, h] = st_ref[j, h] * p["dec"][:, hs] + _dot(p["vt"][vs, :], p["kw"][:, hs])
    if finish:
        keys = sorted(outs)
        normed = _norm_rows_many([outs[k] for k in keys])
        for (j, h), y in zip(keys, normed):
            vs = slice(h * DV_GLA, (h + 1) * DV_GLA)
            o_ref[j, :, vs] = (y * _silu(g_ref[j, :, vs])).astype(BF16)

    @pl.when(n == n_blocks - 1)
    def _():
        sfin_ref[...] = st_ref[...]


def _gla(proj, w2, b2, s0t, *, n_seq, seq):
    nb = seq // GLA_BLOCK
    G = min(GLA_GROUP, n_seq)
    hk, hv = H_GLA * DK_GLA, H_GLA * DV_GLA
    x, tot = _gla_gates(proj, w2, b2)
    x = x.reshape(2, n_seq, seq, hk)
    safe = (tot.min(axis=-1) > -GLA_SAFE).reshape(2, n_seq // G, G, nb).all(axis=2)
    flags = safe.astype(jnp.int32).reshape(2, (n_seq // G) * nb)
    p3 = proj.reshape(n_seq, seq, ODD_IN_PAD)
    state_shape = jax.ShapeDtypeStruct((n_seq, H_GLA, DV_GLA, DK_GLA), F32)
    state_spec = pl.BlockSpec((G, H_GLA, DV_GLA, DK_GLA), lambda g, n, f: (g, 0, 0, 0))
    scratch = [pltpu.VMEM((G, H_GLA, DV_GLA, DK_GLA), F32), pltpu.VMEM((G, H_GLA, GLA_BLOCK, GLA_BLOCK), F32)]

    def run(rev, extra_args, extra_cols, out_dtype):
        blk = (lambda n: nb - 1 - n) if rev else (lambda n: n)
        d = 1 if rev else 0
        tok = lambda width, col: pl.BlockSpec((G, GLA_BLOCK, width), lambda g, n, f: (g, blk(n), col))
        grid_spec = pltpu.PrefetchScalarGridSpec(
            num_scalar_prefetch=1,
            grid=(n_seq // G, nb),
            in_specs=[tok(hk, O_GQ // hk), tok(hk, O_GK // hk), tok(hv, O_GV // hv),
                      pl.BlockSpec((None, G, GLA_BLOCK, hk), lambda g, n, f: (d, g, blk(n), 0)),
                      pl.BlockSpec((None, G, H_GLA, DV_GLA, DK_GLA), lambda g, n, f: (d, g, 0, 0, 0))]
            + [tok(hv, col) for col in extra_cols],
            out_specs=[tok(hv, 0), state_spec],
            scratch_shapes=scratch)
        return pl.pallas_call(
            functools.partial(_gla_body, rev=rev, n_blocks=nb, finish=rev),
            grid_spec=grid_spec,
            out_shape=[jax.ShapeDtypeStruct((n_seq, seq, hv), out_dtype), state_shape],
            compiler_params=_cparams(2),
            name="gla_bwd" if rev else "gla_fwd",
        )(flags[d], p3, p3, p3, x, s0t, *extra_args)

    o_f, s_f = run(False, [], [], F32)
    out, s_b = run(True, [o_f, p3], [0, O_GR // hv], BF16)
    return out.reshape(n_seq * seq, hv), jnp.stack([s_f, s_b], axis=0)


def _stacked(caches, n_layers):
    return [c[:, None] for c in caches] if n_layers == 1 else caches


def kernel(x_prompt, x_sample, c, state_ret, cache_diff_k, cache_diff_v, cache_swa_k, cache_swa_v, state_gla,
           c_ctx, w_mod, b_mod, ln_g, ln_b, w_in_even, w_out_even, ret_decay, diff_lam, w_in_odd, w_out_odd,
           swa_sink, gla_w2, gla_b, w_ff1, w_ff2):
    xc = x_prompt.reshape(N_CTX, D_MODEL)
    xl = x_sample.reshape(N_LAT, D_MODEL)
    cond = jnp.concatenate([c_ctx[None, :], c, jnp.zeros((N_COND - 1 - DEC_BATCH, D_MODEL), F32)], axis=0)
    mod = _modulation(cond, w_mod, b_mod).reshape(DEPTH, N_COND, 1, 6 * D_MODEL)
    rope = _rope_tables(DEC_SEQ // GRID_W, DH_DIFF)

    w_in_even_b = w_in_even.astype(BF16)
    w_in_odd_b = _reorder_odd(jnp.swapaxes(w_in_odd, 1, 2).astype(BF16), axis=1)

    ctx = dict(n_seq=BATCH, seq=SEQ)
    lat = dict(n_seq=DEC_BATCH, seq=DEC_SEQ)
    new_ret, new_gla = [], []
    new_diff = new_swa = None
    for l in range(DEPTH):
        if l % 2 == 0:
            e = l // 2
            lam_init = 0.8 - 0.6 * math.exp(-0.3 * l)
            pc = _in_projection(xc, mod[l], w_in_even_b[e], False)
            pl_ = _in_projection(xl, mod[l], w_in_even_b[e], True)
            zero = jnp.zeros((BATCH, 2, H_RET, DK_RET, DV_RET), F32)
            a_ctx, s_ret = _retention(pc, ret_decay[e], zero, **ctx)
            a_lat, _ = _retention(pl_, ret_decay[e], state_ret[:, e], **lat)
            b_ctx, *new_diff = _diff_attention_ctx(pc, diff_lam[e], lam_init, prev=new_diff, **ctx)
            b_lat, *w_post = _diff_attention_latent(pl_, diff_lam[e], lam_init,
                                                    cache=(cache_diff_k[:, e], cache_diff_v[:, e]), rope=rope,
                                                    cast=[(w_out_even, e), (w_ff1, l), (w_ff2, l)], **lat)
            new_ret.append(s_ret)
        else:
            o = l // 2
            pc = _in_projection(xc, mod[l], w_in_odd_b[o], False, w_transposed=True)
            pl_ = _in_projection(xl, mod[l], w_in_odd_b[o], True, w_transposed=True)
            a_ctx, *new_swa = _swa_attention(pc, swa_sink[o], prev=new_swa, **ctx)
            a_lat, *w_post = _swa_attention(pl_, swa_sink[o], cache=(cache_swa_k[:, o], cache_swa_v[:, o]),
                                            rope=rope, cast=[(w_out_odd, o), (w_ff1, l), (w_ff2, l)], **lat)
            zero = jnp.zeros((2, BATCH, H_GLA, DV_GLA, DK_GLA), F32)
            b_ctx, s_gla = _gla(pc, gla_w2[o], gla_b[o], zero, **ctx)
            s0t = state_gla[:, o].transpose(1, 0, 2, 4, 3)
            b_lat, _ = _gla(pl_, gla_w2[o], gla_b[o], s0t, **lat)
            new_gla.append(s_gla.transpose(1, 0, 2, 4, 3))
        w_out_b, w1_b, w2_b = w_post
        xc = _post_mixer(xc, a_ctx, b_ctx, mod[l], w_out_b, ln_g, ln_b, w1_b, w2_b, l, False)
        xl = _post_mixer(xl, a_lat, b_lat, mod[l], w_out_b, ln_g, ln_b, w1_b, w2_b, l, True)
    return (xc.reshape(BATCH, SEQ, D_MODEL), xl.reshape(DEC_BATCH, DEC_SEQ, D_MODEL),
            jnp.stack(new_ret, axis=1), *_stacked(new_diff, N_EVEN), *_stacked(new_swa, N_ODD),
            jnp.stack(new_gla, axis=1))
```
